```python
import math
import jax, jax.numpy as jnp
from jax import lax
import numpy as np

D_MODEL = 2048
BATCH = 8
SEQ = 2048
DEPTH = 1

SSM_WIDTH = D_MODEL // 2
SSM_GROUP_CH = 16
SSM_GROUPS = SSM_WIDTH // SSM_GROUP_CH
SSM_STATE = 64
N_HEADS = 8
HEAD_DIM = 128
N_KV_HEADS = 2
GQA_GROUP = N_HEADS // N_KV_HEADS
ATTN_WIDTH = N_HEADS * HEAD_DIM
KV_WIDTH = N_KV_HEADS * HEAD_DIM
MIX_WIDTH = SSM_WIDTH + ATTN_WIDTH
N_IDX_HEADS = 16
IDX_DIM = 64
INDEX_TOPK = 256
Q_BLOCK = 128
D_FF = 5632
CONV_WIDTH = 3
LN_EPS = 1e-5
DEEPNORM_ALPHA = (2.0 * DEPTH) ** 0.25
DEEPNORM_BETA = (8.0 * DEPTH) ** -0.25
IN_SPLITS = (SSM_WIDTH, ATTN_WIDTH, KV_WIDTH, KV_WIDTH, N_IDX_HEADS * IDX_DIM, IDX_DIM, N_IDX_HEADS)
N_IN = sum(IN_SPLITS)

kernel_name = "hymba_s5_dsa_alibi_convffn_deepnorm"


def _layer_norm(x, g, b):
    xf = x.astype(jnp.float32)
    mu = jnp.mean(xf, axis=-1, keepdims=True)
    xc = xf - mu
    var = jnp.mean(xc * xc, axis=-1, keepdims=True)
    y = xc * lax.rsqrt(var + LN_EPS) * g.astype(jnp.float32) + b.astype(jnp.float32)
    return y.astype(x.dtype)


def _complex_linear_combine(left, right):
    a1r, a1i, b1r, b1i = left
    a2r, a2i, b2r, b2i = right
    ar = a2r * a1r - a2i * a1i
    ai = a2r * a1i + a2i * a1r
    br = a2r * b1r - a2i * b1i + b2r
    bi = a2r * b1i + a2i * b1r + b2i
    return (ar, ai, br, bi)


def _s5_mixer(u, a_re, a_im, log_dt, b_re, b_im, c_re, c_im, d, w_glu, b_glu):
    bsz, seq, _ = u.shape
    f32 = jnp.float32
    uf = u.astype(f32).reshape(bsz, seq, SSM_GROUPS, SSM_GROUP_CH)
    a_re = a_re.astype(f32)
    a_im = a_im.astype(f32)
    dt = jnp.exp(log_dt.astype(f32))[:, None]
    mag = jnp.exp(dt * a_re)
    ang = dt * a_im
    abar_re = mag * jnp.cos(ang)
    abar_im = mag * jnp.sin(ang)
    num_re = abar_re - 1.0
    num_im = abar_im
    den = a_re * a_re + a_im * a_im
    f_re = (num_re * a_re + num_im * a_im) / den
    f_im = (num_im * a_re - num_re * a_im) / den
    b_re = b_re.astype(f32)
    b_im = b_im.astype(f32)
    bbar_re = f_re[..., None] * b_re - f_im[..., None] * b_im
    bbar_im = f_re[..., None] * b_im + f_im[..., None] * b_re
    bu_re = jnp.einsum('bsgh,gph->bsgp', uf, bbar_re)
    bu_im = jnp.einsum('bsgh,gph->bsgp', uf, bbar_im)
    a_full_re = jnp.broadcast_to(abar_re, bu_re.shape)
    a_full_im = jnp.broadcast_to(abar_im, bu_re.shape)
    _, _, h_re, h_im = lax.associative_scan(
        _complex_linear_combine, (a_full_re, a_full_im, bu_re, bu_im), axis=1)
    y = (jnp.einsum('bsgp,ghp->bsgh', h_re, c_re.astype(f32))
         - jnp.einsum('bsgp,ghp->bsgh', h_im, c_im.astype(f32))
         + d.astype(f32) * uf)
    y = jax.nn.gelu(y.reshape(bsz, seq, SSM_WIDTH), approximate=False).astype(u.dtype)
    return y * jax.nn.sigmoid(y @ w_glu + b_glu)


def _alibi_slopes():
    h = jnp.arange(1, N_HEADS + 1, dtype=jnp.float32)
    return jnp.exp2(-8.0 * h / N_HEADS)


def _dsa_attention(q, k, v, q_idx, k_idx, w_idx):
    bsz, seq = q.shape[0], q.shape[1]
    n_keep = min(INDEX_TOPK, seq // 4)
    n_blocks = seq // Q_BLOCK
    slopes = _alibi_slopes().reshape(N_KV_HEADS, GQA_GROUP)
    key_pos = jnp.arange(seq, dtype=jnp.int32)
    scale = HEAD_DIM ** -0.5

    def to_blocks(a):
        return a.reshape(bsz, n_blocks, Q_BLOCK, *a.shape[2:]).swapaxes(0, 1)

    def block_fn(args):
        qb, qib, wb, blk = args
        t = blk * Q_BLOCK + jnp.arange(Q_BLOCK, dtype=jnp.int32)
        rel = jax.nn.relu(jnp.einsum('bqhd,bsd->bqhs', qib, k_idx))
        idx_score = jnp.einsum('bqh,bqhs->bqs', wb, rel).astype(jnp.float32)
        causal = key_pos[None, :] <= t[:, None]
        idx_score = jnp.where(causal[None], idx_score, -jnp.inf)
        _, sel = lax.top_k(idx_score, n_keep)
        valid = sel <= t[None, :, None]
        kg = jax.vmap(lambda kb, ib: kb[ib])(k, sel)
        vg = jax.vmap(lambda vb, ib: vb[ib])(v, sel)
        qg = qb.reshape(bsz, Q_BLOCK, N_KV_HEADS, GQA_GROUP, HEAD_DIM)
        logits = jnp.einsum('bqcgd,bqkcd->bqcgk', qg, kg).astype(jnp.float32) * scale
        dist = (t[None, :, None] - sel).astype(jnp.float32)
        logits = logits - slopes[None, None, :, :, None] * dist[:, :, None, None, :]
        logits = jnp.where(valid[:, :, None, None, :], logits, -jnp.inf)
        probs = jax.nn.softmax(logits, axis=-1).astype(vg.dtype)
        o = jnp.einsum('bqcgk,bqkcd->bqcgd', probs, vg)
        return o.reshape(bsz, Q_BLOCK, ATTN_WIDTH)

    out = lax.map(block_fn, (to_blocks(q), to_blocks(q_idx), to_blocks(w_idx),
                             jnp.arange(n_blocks, dtype=jnp.int32)))
    return out.swapaxes(0, 1).reshape(bsz, seq, ATTN_WIDTH)


def _conv_ffn(x, w_up, w_gate, conv_w, conv_b, w_down):
    seq = x.shape[1]
    h = x @ w_up
    hp = jnp.pad(h, ((0, 0), (CONV_WIDTH - 1, 0), (0, 0)))
    hc = conv_b + sum(conv_w[i] * hp[:, i:i + seq] for i in range(CONV_WIDTH))
    return (jax.nn.gelu(hc, approximate=False) * (x @ w_gate)) @ w_down


def setup_inputs(seed: int = 0) -> dict:
    key = jax.random.key(seed)
    ks = jax.random.split(key, 24)
    f32 = jnp.float32
    L = DEPTH

    def nrm(k, shape, scale):
        return jax.random.normal(k, shape, f32) * scale

    n = jnp.arange(SSM_STATE, dtype=f32)
    G, P, HG = SSM_GROUPS, SSM_STATE, SSM_GROUP_CH
    return {
        "x": nrm(ks[0], (BATCH, SEQ, D_MODEL), 1.0),
        "w_in": nrm(ks[1], (L, D_MODEL, N_IN), D_MODEL ** -0.5),
        "ssm_a_re": -0.5 * jnp.exp(nrm(ks[2], (L, G, P), 0.02)),
        "ssm_a_im": math.pi * n + nrm(ks[3], (L, G, P), 0.02),
        "ssm_log_dt": jax.random.uniform(ks[4], (L, G), f32, math.log(1e-3), math.log(1e-1)),
        "ssm_b_re": nrm(ks[5], (L, G, P, HG), (2.0 * HG) ** -0.5),
        "ssm_b_im": nrm(ks[6], (L, G, P, HG), (2.0 * HG) ** -0.5),
        "ssm_c_re": nrm(ks[7], (L, G, HG, P), P ** -0.5),
        "ssm_c_im": nrm(ks[8], (L, G, HG, P), P ** -0.5),
        "ssm_d": nrm(ks[9], (L, G, HG), 1.0),
        "w_glu": nrm(ks[10], (L, SSM_WIDTH, SSM_WIDTH), SSM_WIDTH ** -0.5),
        "b_glu": nrm(ks[11], (L, SSM_WIDTH), 0.01),
        "w_out": nrm(ks[12], (L, MIX_WIDTH, D_MODEL), MIX_WIDTH ** -0.5 * DEEPNORM_BETA),
        "ln1_g": 1.0 + nrm(ks[13], (L, D_MODEL), 0.02),
        "ln1_b": nrm(ks[14], (L, D_MODEL), 0.02),
        "w_up": nrm(ks[15], (L, D_MODEL, D_FF), D_MODEL ** -0.5),
        "w_gate": nrm(ks[16], (L, D_MODEL, D_FF), D_MODEL ** -0.5),
        "conv_w": nrm(ks[17], (L, CONV_WIDTH, D_FF), CONV_WIDTH ** -0.5),
        "conv_b": nrm(ks[18], (L, D_FF), 0.01),
        "w_down": nrm(ks[19], (L, D_FF, D_MODEL), D_FF ** -0.5 * DEEPNORM_BETA),
        "ln2_g": 1.0 + nrm(ks[20], (L, D_MODEL), 0.02),
        "ln2_b": nrm(ks[21], (L, D_MODEL), 0.02),
    }


def reference(x, w_in, ssm_a_re, ssm_a_im, ssm_log_dt, ssm_b_re, ssm_b_im, ssm_c_re,
              ssm_c_im, ssm_d, w_glu, b_glu, w_out, ln1_g, ln1_b, w_up, w_gate,
              conv_w, conv_b, w_down, ln2_g, ln2_b):
    bsz, seq, _ = x.shape
    split_at = [int(s) for s in np.cumsum(IN_SPLITS)[:-1]]
    idx_w_scale = (N_IDX_HEADS ** -0.5) * (IDX_DIM ** -0.5)
    h = x
    for l in range(DEPTH):
        proj = h @ w_in[l]
        u_ssm, q, k, v, qi, ki, wi = jnp.split(proj, split_at, axis=-1)
        q = q.reshape(bsz, seq, N_HEADS, HEAD_DIM)
        k = k.reshape(bsz, seq, N_KV_HEADS, HEAD_DIM)
        v = v.reshape(bsz, seq, N_KV_HEADS, HEAD_DIM)
        qi = qi.reshape(bsz, seq, N_IDX_HEADS, IDX_DIM)
        wi = wi * idx_w_scale
        y_ssm = _s5_mixer(u_ssm, ssm_a_re[l], ssm_a_im[l], ssm_log_dt[l], ssm_b_re[l],
                          ssm_b_im[l], ssm_c_re[l], ssm_c_im[l], ssm_d[l], w_glu[l], b_glu[l])
        y_attn = _dsa_attention(q, k, v, qi, ki, wi)
        mix = jnp.concatenate([y_ssm.astype(h.dtype), y_attn.astype(h.dtype)], axis=-1) @ w_out[l]
        h = _layer_norm(DEEPNORM_ALPHA * h + mix, ln1_g[l], ln1_b[l])
        f = _conv_ffn(h, w_up[l], w_gate[l], conv_w[l], conv_b[l], w_down[l])
        h = _layer_norm(DEEPNORM_ALPHA * h + f, ln2_g[l], ln2_b[l])
    return h
```

```python
import functools
import math

import jax
import jax.numpy as jnp
from jax import lax
from jax.experimental import pallas as pl
from jax.experimental.pallas import tpu as pltpu

F32 = jnp.float32
BF16 = jnp.bfloat16
I32 = jnp.int32

LANES = 128
SUBLANES = 8
VMEM_LIMIT_BYTES = 56 * 1024 * 1024

SSM_GROUP_CH = 16
SSM_STATE = 64
N_HEADS = 8
HEAD_DIM = 128
N_KV_HEADS = 2
GQA_GROUP = N_HEADS // N_KV_HEADS
N_IDX_HEADS = 16
IDX_DIM = 64
INDEX_TOPK = 256
CONV_WIDTH = 3
LN_EPS = 1e-5
DEPTH = 1
DEEPNORM_ALPHA = (2.0 * DEPTH) ** 0.25

GROUPS_PER_SLAB = LANES // SSM_GROUP_CH
SLAB_STATE = GROUPS_PER_SLAB * SSM_STATE
INT_MIN = -(2 ** 31)


def _params(*sem):
    return pltpu.CompilerParams(dimension_semantics=sem, vmem_limit_bytes=VMEM_LIMIT_BYTES)


def _const_spec(shape):
    zeros = (0,) * len(shape)
    return pl.BlockSpec(shape, lambda *_: zeros, pipeline_mode=pl.Buffered(1))


def _layer_norm(v, g, b):
    mu = jnp.mean(v, axis=-1, keepdims=True)
    vc = v - mu
    var = jnp.mean(vc * vc, axis=-1, keepdims=True)
    return vc * lax.rsqrt(var + LN_EPS) * g + b


def _gelu(v):
    return 0.5 * v * (1.0 + lax.erf(v * (1.0 / math.sqrt(2.0))))


def _in_proj_kernel(x_ref, wu_ref, wq_ref, wk_ref, wv_ref, wqi_ref, wkw_ref,
                    u_ref, q_ref, k_ref, v_ref, qi_ref, kw_ref, *, n_batch):
    b = pl.program_id(1)
    xb = x_ref[...].astype(BF16)
    ts = xb.shape[0]
    u = jnp.dot(xb, wu_ref[...], preferred_element_type=F32)
    for s in range(u.shape[1] // LANES):
        u_ref[s, pl.ds(b, ts, stride=n_batch), :] = u[:, s * LANES:(s + 1) * LANES]
    q_ref[...] = jnp.dot(xb, wq_ref[...], preferred_element_type=F32).astype(BF16)
    k_ref[...] = jnp.dot(xb, wk_ref[...], preferred_element_type=F32).astype(BF16)
    v_ref[...] = jnp.dot(xb, wv_ref[...], preferred_element_type=F32).astype(BF16)
    qi_ref[...] = jnp.dot(xb, wqi_ref[...], preferred_element_type=F32).astype(BF16)
    kw_ref[...] = jnp.dot(xb, wkw_ref[...], preferred_element_type=F32)


def _in_proj(x, wu, wq, wk, wv, wqi, wkw, *, ts):
    n_batch, seq, d = x.shape
    n_slabs = wu.shape[1] // LANES
    grid = (seq // ts, n_batch)
    row = lambda w: pl.BlockSpec((None, ts, w), lambda i, b: (b, i, 0))
    out_shape = (
        jax.ShapeDtypeStruct((n_slabs, seq * n_batch, LANES), F32),
        jax.ShapeDtypeStruct((n_batch, seq, wq.shape[1]), BF16),
        jax.ShapeDtypeStruct((n_batch, seq, wk.shape[1]), BF16),
        jax.ShapeDtypeStruct((n_batch, seq, wv.shape[1]), BF16),
        jax.ShapeDtypeStruct((n_batch, seq, wqi.shape[1]), BF16),
        jax.ShapeDtypeStruct((n_batch, seq, wkw.shape[1]), F32),
    )
    return pl.pallas_call(
        functools.partial(_in_proj_kernel, n_batch=n_batch),
        grid=grid,
        in_specs=[row(d)] + [_const_spec(w.shape) for w in (wu, wq, wk, wv, wqi, wkw)],
        out_specs=(
            pl.BlockSpec((n_slabs, ts * n_batch, LANES), lambda i, b: (0, i, 0)),
            row(wq.shape[1]), row(wk.shape[1]), row(wv.shape[1]), row(wqi.shape[1]),
            row(wkw.shape[1]),
        ),
        out_shape=out_shape,
        compiler_params=_params("arbitrary", "arbitrary"),
        name="in_proj",
    )(x, wu, wq, wk, wv, wqi, wkw)


def _s5_discretize_kernel(a_re_ref, a_im_ref, log_dt_ref, bt_re_ref, bt_im_ref,
                          abar_re_ref, abar_im_ref, bbar_re_ref, bbar_im_ref):
    a_re = a_re_ref[...]
    a_im = a_im_ref[...]
    dt = jnp.exp(log_dt_ref[...])
    mag = jnp.exp(dt * a_re)
    ang = dt * a_im
    abar_re = mag * jnp.cos(ang)
    abar_im = mag * jnp.sin(ang)
    num_re = abar_re - 1.0
    num_im = abar_im
    den = a_re * a_re + a_im * a_im
    f_re = (num_re * a_re + num_im * a_im) / den
    f_im = (num_im * a_re - num_re * a_im) / den
    abar_re_ref[...] = abar_re
    abar_im_ref[...] = abar_im
    bt_re = bt_re_ref[...]
    bt_im = bt_im_ref[...]
    bbar_re_ref[...] = f_re * bt_re - f_im * bt_im
    bbar_im_ref[...] = f_re * bt_im + f_im * bt_re


def _s5_discretize(a_re, a_im, log_dt, bt_re, bt_im):
    g, p = a_re.shape
    abar_re, abar_im, bbar_re, bbar_im = pl.pallas_call(
        _s5_discretize_kernel,
        out_shape=(jax.ShapeDtypeStruct((g, 1, p), F32), jax.ShapeDtypeStruct((g, 1, p), F32),
                   jax.ShapeDtypeStruct(bt_re.shape, F32), jax.ShapeDtypeStruct(bt_re.shape, F32)),
        name="s5_discretize",
    )(a_re.reshape(g, 1, p), a_im.reshape(g, 1, p), log_dt.reshape(g, 1, 1), bt_re, bt_im)
    return abar_re.reshape(g, p), abar_im.reshape(g, p), bbar_re, bbar_im


def _block_diag_slabs(m):
    n, gl, a, b = m.shape
    eye = jnp.eye(gl, dtype=m.dtype)
    full = m[:, :, :, None, :] * eye[None, :, None, :, None]
    return full.reshape(n, gl * a, gl * b)


def _s5_scan_kernel(u_ref, bbd_ref, ar_ref, ai_ref, cbd_ref, d_ref, y_ref, h_ref, carry_ref,
                    *, ts, n_batch):
    @pl.when(pl.program_id(1) == 0)
    def _():
        carry_ref[...] = jnp.zeros_like(carry_ref)

    u = u_ref[...]
    h_ref[...] = jnp.dot(u.astype(BF16), bbd_ref[...], preferred_element_type=F32)
    ar = ar_ref[...]
    ai = ai_ref[...]
    half = ar.shape[-1]

    def step(t, c):
        re, im = c
        r0 = pl.multiple_of(t * n_batch, n_batch)
        bu = h_ref[pl.ds(r0, n_batch), :]
        nre = ar * re - ai * im + bu[:, :half]
        nim = ar * im + ai * re + bu[:, half:]
        h_ref[pl.ds(r0, n_batch), :] = jnp.concatenate([nre, nim], axis=-1)
        return nre, nim

    re, im = lax.fori_loop(0, ts, step, (carry_ref[:, :half], carry_ref[:, half:]), unroll=4)
    carry_ref[...] = jnp.concatenate([re, im], axis=-1)
    y = jnp.dot(h_ref[...].astype(BF16), cbd_ref[...], preferred_element_type=F32) + d_ref[...] * u
    y_ref[...] = _gelu(y)


def _s5_scan(u_tm, bbd, ar, ai, cbd, d, *, ts, n_batch):
    n_slabs, rows, _ = u_tm.shape
    tr = ts * n_batch
    n_state2 = bbd.shape[-1]
    slab = lambda shape: pl.BlockSpec((None,) + shape, lambda s, i: (s, 0, 0))
    return pl.pallas_call(
        functools.partial(_s5_scan_kernel, ts=ts, n_batch=n_batch),
        grid=(n_slabs, rows // tr),
        in_specs=[
            pl.BlockSpec((None, tr, LANES), lambda s, i: (s, i, 0)),
            slab((LANES, n_state2)),
            slab((n_batch, n_state2 // 2)),
            slab((n_batch, n_state2 // 2)),
            slab((n_state2, LANES)),
            slab((1, LANES)),
        ],
        out_specs=pl.BlockSpec((None, tr, LANES), lambda s, i: (s, i, 0)),
        out_shape=jax.ShapeDtypeStruct(u_tm.shape, F32),
        scratch_shapes=[pltpu.VMEM((tr, n_state2), F32), pltpu.VMEM((n_batch, n_state2), F32)],
        compiler_params=_params("arbitrary", "arbitrary"),
        name="s5_scan",
    )(u_tm, bbd, ar, ai, cbd, d)


def _glu_kernel(y_ref, w_ref, b_ref, o_ref, *, ts, n_batch):
    n_slabs = y_ref.shape[0]
    w = w_ref[...]
    bias = b_ref[...]
    for b in range(n_batch):
        yb = jnp.concatenate(
            [y_ref[s, pl.ds(b, ts, stride=n_batch), :] for s in range(n_slabs)], axis=-1)
        z = jnp.dot(yb.astype(BF16), w, preferred_element_type=F32) + bias
        o_ref[b] = (yb * jax.nn.sigmoid(z)).astype(BF16)


def _glu(y_tm, w_glu, b_glu, *, ts, n_batch):
    n_slabs, rows, _ = y_tm.shape
    seq = rows // n_batch
    width = n_slabs * LANES
    return pl.pallas_call(
        functools.partial(_glu_kernel, ts=ts, n_batch=n_batch),
        grid=(seq // ts,),
        in_specs=[
            pl.BlockSpec((n_slabs, ts * n_batch, LANES), lambda i: (0, i, 0)),
            _const_spec(w_glu.shape),
            _const_spec((1, width)),
        ],
        out_specs=pl.BlockSpec((n_batch, ts, width), lambda i: (0, i, 0)),
        out_shape=jax.ShapeDtypeStruct((n_batch, seq, width), BF16),
        compiler_params=_params("arbitrary"),
        name="glu",
    )(y_tm, w_glu, b_glu.reshape(1, width))


def _attn_kernel(qi_ref, kwq_ref, kwf_ref, q_ref, k_ref, v_ref, o_ref, keys_ref,
                 *, tq, n_keep, idx_w_scale, qk_scale):
    i = pl.program_id(1)
    n_chunks = i + 1
    nt = (((1,), (1,)), ((), ()))
    row = lax.broadcasted_iota(I32, (tq, tq), 0)
    col = lax.broadcasted_iota(I32, (tq, tq), 1)
    wi = kwq_ref[:, IDX_DIM:IDX_DIM + N_IDX_HEADS] * idx_w_scale

    def score_chunk(j, carry):
        off = pl.multiple_of(j * tq, tq)
        kic = kwf_ref[pl.ds(off, tq), :][:, :IDX_DIM].astype(BF16)
        s = jnp.zeros((tq, tq), F32)
        for h in range(N_IDX_HEADS):
            r = lax.dot_general(qi_ref[:, h * IDX_DIM:(h + 1) * IDX_DIM], kic, nt,
                                preferred_element_type=F32)
            s = s + wi[:, h:h + 1] * jnp.maximum(r, 0.0)
        bits = pltpu.bitcast(s, I32)
        key = jnp.where(bits < 0, bits ^ jnp.int32(0x7FFFFFFF), bits)
        causal = (j - i) * tq + col <= row
        keys_ref[j] = jnp.where(causal, key, jnp.int32(INT_MIN))
        return carry

    lax.fori_loop(0, n_chunks, score_chunk, 0)

    def count_ge(cand):
        def body(j, acc):
            m = jnp.where(keys_ref[j] >= cand, 1.0, 0.0)
            part = m[:, :LANES]
            for c in range(1, tq // LANES):
                part = part + m[:, c * LANES:(c + 1) * LANES]
            return acc + part
        acc = lax.fori_loop(0, n_chunks, body, jnp.zeros((tq, LANES), F32))
        return jnp.sum(acc, axis=-1, keepdims=True)

    def full(v):
        return jnp.broadcast_to(v, (tq, tq))

    keep = jnp.float32(n_keep)
    tau = jnp.where(count_ge(jnp.zeros((tq, tq), I32)) >= keep, jnp.int32(0), jnp.int32(INT_MIN))

    def bit_body(bi, tau):
        cand = tau + jnp.left_shift(jnp.int32(1), jnp.int32(30) - bi)
        return jnp.where(count_ge(full(cand)) >= keep, cand, tau)

    tau = lax.fori_loop(0, 31, bit_body, tau)
    tau_full = full(jnp.maximum(tau, jnp.int32(INT_MIN + 1)))

    for hd in range(N_HEADS):
        c = hd // GQA_GROUP
        slope = 2.0 ** (-8.0 * (hd + 1) / N_HEADS)
        qh = q_ref[:, hd * HEAD_DIM:(hd + 1) * HEAD_DIM]

        def kv_body(j, carry, qh=qh, c=c, slope=slope):
            m_i, l_i, acc = carry
            off = pl.multiple_of(j * tq, tq)
            kc = k_ref[pl.ds(off, tq), :][:, c * HEAD_DIM:(c + 1) * HEAD_DIM]
            vc = v_ref[pl.ds(off, tq), :][:, c * HEAD_DIM:(c + 1) * HEAD_DIM]
            lg = lax.dot_general(qh, kc, nt, preferred_element_type=F32) * qk_scale
            dist = ((i - j) * tq + row - col).astype(F32)
            lg = lg - slope * dist
            sel = keys_ref[j] >= tau_full
            m_new = jnp.maximum(m_i, jnp.max(jnp.where(sel, lg, -1e30), axis=-1, keepdims=True))
            alpha = jnp.exp(m_i - m_new)
            p = jnp.where(sel, jnp.exp(lg - m_new), 0.0)
            l_new = alpha * l_i + jnp.sum(p, axis=-1, keepdims=True)
            acc = alpha * acc + jnp.dot(p.astype(BF16), vc, preferred_element_type=F32)
            return m_new, l_new, acc

        init = (jnp.full((tq, 1), -1e30, F32), jnp.zeros((tq, 1), F32), jnp.zeros((tq, HEAD_DIM), F32))
        _, l_f, acc_f = lax.fori_loop(0, n_chunks, kv_body, init)
        o_ref[:, hd * HEAD_DIM:(hd + 1) * HEAD_DIM] = (acc_f / l_f).astype(BF16)


def _attention(qi, kw, q, k, v, *, tq):
    n_batch, seq, _ = q.shape
    n_keep = min(INDEX_TOPK, seq // 4)
    blk = lambda w: pl.BlockSpec((None, tq, w), lambda b, i: (b, i, 0))
    whole = lambda w: pl.BlockSpec((None, seq, w), lambda b, i: (b, 0, 0))
    return pl.pallas_call(
        functools.partial(_attn_kernel, tq=tq, n_keep=n_keep,
                          idx_w_scale=(N_IDX_HEADS ** -0.5) * (IDX_DIM ** -0.5),
                          qk_scale=HEAD_DIM ** -0.5),
        grid=(n_batch, seq // tq),
        in_specs=[blk(qi.shape[2]), blk(kw.shape[2]), whole(kw.shape[2]),
                  blk(q.shape[2]), whole(k.shape[2]), whole(v.shape[2])],
        out_specs=blk(q.shape[2]),
        out_shape=jax.ShapeDtypeStruct(q.shape, BF16),
        scratch_shapes=[pltpu.VMEM((seq // tq, tq, tq), I32)],
        compiler_params=_params("arbitrary", "arbitrary"),
        name="sparse_attn",
    )(qi, kw, kw, q, k, v)


def _out_proj_kernel(ys_ref, ya_ref, x_ref, ws_ref, wa_ref, g_ref, b_ref, o_ref):
    mix = jnp.dot(ys_ref[...], ws_ref[...], preferred_element_type=F32)
    mix = mix + jnp.dot(ya_ref[...], wa_ref[...], preferred_element_type=F32)
    o_ref[...] = _layer_norm(DEEPNORM_ALPHA * x_ref[...] + mix, g_ref[...], b_ref[...])


def _out_proj(ys, ya, x, ws, wa, g, b, *, tm):
    m, d = x.shape
    blk = lambda w: pl.BlockSpec((tm, w), lambda i: (i, 0))
    return pl.pallas_call(
        _out_proj_kernel,
        grid=(m // tm,),
        in_specs=[blk(ys.shape[1]), blk(ya.shape[1]), blk(d), _const_spec(ws.shape),
                  _const_spec(wa.shape), _const_spec((1, d)), _const_spec((1, d))],
        out_specs=blk(d),
        out_shape=jax.ShapeDtypeStruct((m, d), F32),
        compiler_params=_params("arbitrary"),
        name="out_proj_ln1",
    )(ys, ya, x, ws, wa, g.reshape(1, d), b.reshape(1, d))


def _ffn_kernel(h_ref, halo_ref, wup_ref, wgate_ref, wdown_ref, cw_ref, cb_ref, g_ref, b_ref,
                o_ref, hb_ref, halo_b_ref, acc_ref):
    i = pl.program_id(1)
    j = pl.program_id(2)

    @pl.when(j == 0)
    def _():
        hb_ref[...] = h_ref[...].astype(BF16)
        halo_b_ref[...] = jnp.where(i == 0, 0.0, halo_ref[...]).astype(BF16)
        acc_ref[...] = jnp.zeros_like(acc_ref)

    hb = hb_ref[...]
    tm = hb.shape[0]
    wup = wup_ref[...]
    up = jnp.dot(hb, wup, preferred_element_type=F32)
    up_halo = jnp.dot(halo_b_ref[...], wup, preferred_element_type=F32)
    ext = jnp.concatenate([up_halo, up], axis=0)
    cw = cw_ref[...]
    hc = cb_ref[...] + cw[2:3, :] * up
    for lag in range(1, CONV_WIDTH):
        start = SUBLANES - lag
        hc = hc + cw[CONV_WIDTH - 1 - lag:CONV_WIDTH - lag, :] * ext[start:start + tm, :]
    gate = jnp.dot(hb, wgate_ref[...], preferred_element_type=F32)
    act = (_gelu(hc) * gate).astype(BF16)
    acc_ref[...] += jnp.dot(act, wdown_ref[...], preferred_element_type=F32)

    @pl.when(j == pl.num_programs(2) - 1)
    def _():
        o_ref[...] = _layer_norm(DEEPNORM_ALPHA * h_ref[...] + acc_ref[...], g_ref[...], b_ref[...])


def _ffn(h, w_up, w_gate, w_down, conv_w, conv_b, g, b, *, tm, tf):
    n_batch, seq, d = h.shape
    f = w_up.shape[1]
    halo_blocks = tm // SUBLANES
    return pl.pallas_call(
        _ffn_kernel,
        grid=(n_batch, seq // tm, f // tf),
        in_specs=[
            pl.BlockSpec((None, tm, d), lambda bb, i, j: (bb, i, 0)),
            pl.BlockSpec((None, SUBLANES, d),
                         lambda bb, i, j: (bb, jnp.maximum(i * halo_blocks - 1, 0), 0)),
            pl.BlockSpec((d, tf), lambda bb, i, j: (0, j)),
            pl.BlockSpec((d, tf), lambda bb, i, j: (0, j)),
            pl.BlockSpec((tf, d), lambda bb, i, j: (j, 0)),
            pl.BlockSpec((CONV_WIDTH, tf), lambda bb, i, j: (0, j)),
            pl.BlockSpec((1, tf), lambda bb, i, j: (0, j)),
            _const_spec((1, d)),
            _const_spec((1, d)),
        ],
        out_specs=pl.BlockSpec((None, tm, d), lambda bb, i, j: (bb, i, 0)),
        out_shape=jax.ShapeDtypeStruct((n_batch, seq, d), F32),
        scratch_shapes=[pltpu.VMEM((tm, d), BF16), pltpu.VMEM((SUBLANES, d), BF16),
                        pltpu.VMEM((tm, d), F32)],
        compiler_params=_params("arbitrary", "arbitrary", "arbitrary"),
        name="ffn_ln2",
    )(h, h, w_up, w_gate, w_down, conv_w, conv_b.reshape(1, f), g.reshape(1, d), b.reshape(1, d))


def _layer(h, w_in, a_re, a_im, log_dt, b_re, b_im, c_re, c_im, ssm_d, w_glu, b_glu, w_out,
           ln1_g, ln1_b, w_up, w_gate, conv_w, conv_b, w_down, ln2_g, ln2_b):
    n_batch, seq, d = h.shape
    assert n_batch == SUBLANES, "the time-major scan layout puts the batch on the sublanes"
    n_groups, n_state = a_re.shape
    ssm_w = n_groups * SSM_GROUP_CH
    attn_w = N_HEADS * HEAD_DIM
    kv_w = N_KV_HEADS * HEAD_DIM
    qi_w = N_IDX_HEADS * IDX_DIM
    n_slabs = ssm_w // LANES
    assert n_state == SSM_STATE and ssm_w % LANES == 0

    o = 0
    parts = []
    for w in (ssm_w, attn_w, kv_w, kv_w, qi_w):
        parts.append(w_in[:, o:o + w].astype(BF16))
        o += w
    tail = w_in[:, o:]
    assert tail.shape[1] == IDX_DIM + N_IDX_HEADS
    wkw = jnp.pad(tail, ((0, 0), (0, LANES - tail.shape[1]))).astype(BF16)
    u_tm, q, k, v, qi, kw = _in_proj(h, *parts, wkw, ts=256)

    abar_re, abar_im, bbar_re, bbar_im = _s5_discretize(
        a_re, a_im, log_dt, jnp.swapaxes(b_re, 1, 2), jnp.swapaxes(b_im, 1, 2))
    gl = GROUPS_PER_SLAB
    to_slabs = lambda m: m.reshape(n_slabs, gl, *m.shape[1:])
    bbd = jnp.concatenate([_block_diag_slabs(to_slabs(bbar_re)),
                           _block_diag_slabs(to_slabs(bbar_im))], axis=-1).astype(BF16)
    cbd_re = _block_diag_slabs(to_slabs(jnp.swapaxes(c_re, 1, 2)))
    cbd_im = _block_diag_slabs(to_slabs(jnp.swapaxes(c_im, 1, 2)))
    cbd = jnp.concatenate([cbd_re, -cbd_im], axis=1).astype(BF16)
    bcast = lambda m: jnp.broadcast_to(m.reshape(n_slabs, 1, gl * n_state), (n_slabs, n_batch, gl * n_state))
    y_tm = _s5_scan(u_tm, bbd, bcast(abar_re), bcast(abar_im), cbd,
                    ssm_d.reshape(n_slabs, 1, LANES), ts=256, n_batch=n_batch)
    y_ssm = _glu(y_tm, w_glu.astype(BF16), b_glu, ts=256, n_batch=n_batch)

    y_attn = _attention(qi, kw, q, k, v, tq=256)

    w_out_b = w_out.astype(BF16)
    h1 = _out_proj(y_ssm.reshape(n_batch * seq, ssm_w), y_attn.reshape(n_batch * seq, attn_w),
                   h.reshape(n_batch * seq, d), w_out_b[:ssm_w], w_out_b[ssm_w:], ln1_g, ln1_b, tm=512)
    return _ffn(h1.reshape(n_batch, seq, d), w_up.astype(BF16), w_gate.astype(BF16),
                w_down.astype(BF16), conv_w, conv_b, ln2_g, ln2_b, tm=512, tf=512)


def kernel(x, w_in, ssm_a_re, ssm_a_im, ssm_log_dt, ssm_b_re, ssm_b_im, ssm_c_re, ssm_c_im, ssm_d,
           w_glu, b_glu, w_out, ln1_g, ln1_b, w_up, w_gate, conv_w, conv_b, w_down, ln2_g, ln2_b):
    h = x
    for l in range(w_in.shape[0]):
        h = _layer(h, w_in[l], ssm_a_re[l], ssm_a_im[l], ssm_log_dt[l], ssm_b_re[l], ssm_b_im[l],
                   ssm_c_re[l], ssm_c_im[l], ssm_d[l], w_glu[l], b_glu[l], w_out[l], ln1_g[l],
                   ln1_b[l], w_up[l], w_gate[l], conv_w[l], conv_b[l], w_down[l], ln2_g[l], ln2_b[l])
    return h
```

```python
import functools
import math

import jax
import jax.numpy as jnp
from jax import lax
from jax.experimental import pallas as pl
from jax.experimental.pallas import tpu as pltpu

F32 = jnp.float32
BF16 = jnp.bfloat16
I32 = jnp.int32

LANES = 128
SUBLANES = 8
VMEM_LIMIT_BYTES = 56 * 1024 * 1024

SSM_GROUP_CH = 16
SSM_STATE = 64
N_HEADS = 8
HEAD_DIM = 128
N_KV_HEADS = 2
GQA_GROUP = N_HEADS // N_KV_HEADS
N_IDX_HEADS = 16
IDX_DIM = 64
INDEX_TOPK = 256
CONV_WIDTH = 3
LN_EPS = 1e-5
DEPTH = 1
DEEPNORM_ALPHA = (2.0 * DEPTH) ** 0.25

GROUPS_PER_SLAB = LANES // SSM_GROUP_CH
SLAB_STATE = GROUPS_PER_SLAB * SSM_STATE
INT_MIN = -(2 ** 31)
ACC_ROWS = 4 * SUBLANES


def _tiles(seq, d_ff):
    t_seq = min(256, seq)
    t_row = min(512, seq)
    t_ff = 512 if d_ff % 512 == 0 else LANES
    assert seq % t_seq == 0 and seq % t_row == 0 and d_ff % t_ff == 0
    return t_seq, t_row, t_ff


def _params(*sem):
    return pltpu.CompilerParams(dimension_semantics=sem, vmem_limit_bytes=VMEM_LIMIT_BYTES)


def _const_spec(shape):
    zeros = (0,) * len(shape)
    return pl.BlockSpec(shape, lambda *_: zeros, pipeline_mode=pl.Buffered(1))


def _layer_norm(v, g, b):
    mu = jnp.mean(v, axis=-1, keepdims=True)
    vc = v - mu
    var = jnp.mean(vc * vc, axis=-1, keepdims=True)
    return vc * lax.rsqrt(var + LN_EPS) * g + b


def _gelu(v):
    return 0.5 * v * (1.0 + lax.erf(v * (1.0 / math.sqrt(2.0))))


def _in_proj_kernel(x_ref, wu_ref, wq_ref, wk_ref, wv_ref, wqi_ref, wkw_ref,
                    u_ref, q_ref, k_ref, v_ref, qi_ref, kw_ref, *, n_batch):
    b = pl.program_id(1)
    xb = x_ref[...].astype(BF16)
    ts = xb.shape[0]
    u = jnp.dot(xb, wu_ref[...], preferred_element_type=F32)
    for s in range(u.shape[1] // LANES):
        u_ref[s, pl.ds(b, ts, stride=n_batch), :] = u[:, s * LANES:(s + 1) * LANES]
    q_ref[...] = jnp.dot(xb, wq_ref[...], preferred_element_type=F32).astype(BF16)
    k_ref[...] = jnp.dot(xb, wk_ref[...], preferred_element_type=F32).astype(BF16)
    v_ref[...] = jnp.dot(xb, wv_ref[...], preferred_element_type=F32).T.astype(BF16)
    qi_ref[...] = jnp.dot(xb, wqi_ref[...], preferred_element_type=F32).astype(BF16)
    kw_ref[...] = jnp.dot(xb, wkw_ref[...], preferred_element_type=F32)


def _in_proj(x, wu, wq, wk, wv, wqi, wkw, *, ts):
    n_batch, seq, d = x.shape
    n_slabs = wu.shape[1] // LANES
    grid = (seq // ts, n_batch)
    row = lambda w: pl.BlockSpec((None, ts, w), lambda i, b: (b, i, 0))
    out_shape = (
        jax.ShapeDtypeStruct((n_slabs, seq * n_batch, LANES), F32),
        jax.ShapeDtypeStruct((n_batch, seq, wq.shape[1]), BF16),
        jax.ShapeDtypeStruct((n_batch, seq, wk.shape[1]), BF16),
        jax.ShapeDtypeStruct((n_batch, seq // ts, wv.shape[1], ts), BF16),
        jax.ShapeDtypeStruct((n_batch, seq, wqi.shape[1]), BF16),
        jax.ShapeDtypeStruct((n_batch, seq, wkw.shape[1]), F32),
    )
    return pl.pallas_call(
        functools.partial(_in_proj_kernel, n_batch=n_batch),
        grid=grid,
        in_specs=[row(d)] + [_const_spec(w.shape) for w in (wu, wq, wk, wv, wqi, wkw)],
        out_specs=(
            pl.BlockSpec((n_slabs, ts * n_batch, LANES), lambda i, b: (0, i, 0)),
            row(wq.shape[1]), row(wk.shape[1]),
            pl.BlockSpec((None, None, wv.shape[1], ts), lambda i, b: (b, i, 0, 0)),
            row(wqi.shape[1]), row(wkw.shape[1]),
        ),
        out_shape=out_shape,
        compiler_params=_params("arbitrary", "arbitrary"),
        name="in_proj",
    )(x, wu, wq, wk, wv, wqi, wkw)


def _s5_discretize_kernel(a_re_ref, a_im_ref, log_dt_ref, bt_re_ref, bt_im_ref,
                          abar_re_ref, abar_im_ref, bbar_re_ref, bbar_im_ref):
    a_re = a_re_ref[...]
    a_im = a_im_ref[...]
    dt = jnp.exp(log_dt_ref[...])
    mag = jnp.exp(dt * a_re)
    ang = dt * a_im
    abar_re = mag * jnp.cos(ang)
    abar_im = mag * jnp.sin(ang)
    num_re = abar_re - 1.0
    num_im = abar_im
    den = a_re * a_re + a_im * a_im
    f_re = (num_re * a_re + num_im * a_im) / den
    f_im = (num_im * a_re - num_re * a_im) / den
    abar_re_ref[...] = abar_re
    abar_im_ref[...] = abar_im
    bt_re = bt_re_ref[...]
    bt_im = bt_im_ref[...]
    bbar_re_ref[...] = f_re * bt_re - f_im * bt_im
    bbar_im_ref[...] = f_re * bt_im + f_im * bt_re


def _s5_discretize(a_re, a_im, log_dt, bt_re, bt_im):
    g, p = a_re.shape
    abar_re, abar_im, bbar_re, bbar_im = pl.pallas_call(
        _s5_discretize_kernel,
        out_shape=(jax.ShapeDtypeStruct((g, 1, p), F32), jax.ShapeDtypeStruct((g, 1, p), F32),
                   jax.ShapeDtypeStruct(bt_re.shape, F32), jax.ShapeDtypeStruct(bt_re.shape, F32)),
        name="s5_discretize",
    )(a_re.reshape(g, 1, p), a_im.reshape(g, 1, p), log_dt.reshape(g, 1, 1), bt_re, bt_im)
    return abar_re.reshape(g, p), abar_im.reshape(g, p), bbar_re, bbar_im


def _block_diag_slabs(m):
    n, gl, a, b = m.shape
    eye = jnp.eye(gl, dtype=m.dtype)
    full = m[:, :, :, None, :] * eye[None, :, None, :, None]
    return full.reshape(n, gl * a, gl * b)


def _s5_scan_kernel(u_ref, bbd_ref, ar_ref, ai_ref, cbd_ref, d_ref, y_ref, h_ref, carry_ref,
                    *, ts, n_batch):
    @pl.when(pl.program_id(1) == 0)
    def _():
        carry_ref[...] = jnp.zeros_like(carry_ref)

    u = u_ref[...]
    h_ref[...] = jnp.dot(u.astype(BF16), bbd_ref[...], preferred_element_type=F32)
    ar = ar_ref[...]
    ai = ai_ref[...]
    half = ar.shape[-1]

    def step(t, c):
        re, im = c
        r0 = pl.multiple_of(t * n_batch, n_batch)
        bu = h_ref[pl.ds(r0, n_batch), :]
        nre = ar * re - ai * im + bu[:, :half]
        nim = ar * im + ai * re + bu[:, half:]
        h_ref[pl.ds(r0, n_batch), :] = jnp.concatenate([nre, nim], axis=-1)
        return nre, nim

    re, im = lax.fori_loop(0, ts, step, (carry_ref[:, :half], carry_ref[:, half:]), unroll=4)
    carry_ref[...] = jnp.concatenate([re, im], axis=-1)
    y = jnp.dot(h_ref[...].astype(BF16), cbd_ref[...], preferred_element_type=F32) + d_ref[...] * u
    y_ref[...] = _gelu(y)


def _s5_scan(u_tm, bbd, ar, ai, cbd, d, *, ts, n_batch):
    n_slabs, rows, _ = u_tm.shape
    tr = ts * n_batch
    n_state2 = bbd.shape[-1]
    slab = lambda shape: pl.BlockSpec((None,) + shape, lambda s, i: (s, 0, 0))
    return pl.pallas_call(
        functools.partial(_s5_scan_kernel, ts=ts, n_batch=n_batch),
        grid=(n_slabs, rows // tr),
        in_specs=[
            pl.BlockSpec((None, tr, LANES), lambda s, i: (s, i, 0)),
            slab((LANES, n_state2)),
            slab((n_batch, n_state2 // 2)),
            slab((n_batch, n_state2 // 2)),
            slab((n_state2, LANES)),
            slab((1, LANES)),
        ],
        out_specs=pl.BlockSpec((None, tr, LANES), lambda s, i: (s, i, 0)),
        out_shape=jax.ShapeDtypeStruct(u_tm.shape, F32),
        scratch_shapes=[pltpu.VMEM((tr, n_state2), F32), pltpu.VMEM((n_batch, n_state2), F32)],
        compiler_params=_params("arbitrary", "arbitrary"),
        name="s5_scan",
    )(u_tm, bbd, ar, ai, cbd, d)


def _glu_kernel(y_ref, w_ref, b_ref, o_ref, *, ts, n_batch):
    n_slabs = y_ref.shape[0]
    w = w_ref[...]
    bias = b_ref[...]
    for b in range(n_batch):
        yb = jnp.concatenate(
            [y_ref[s, pl.ds(b, ts, stride=n_batch), :] for s in range(n_slabs)], axis=-1)
        z = jnp.dot(yb.astype(BF16), w, preferred_element_type=F32) + bias
        o_ref[b] = (yb * jax.nn.sigmoid(z)).astype(BF16)


def _glu(y_tm, w_glu, b_glu, *, ts, n_batch):
    n_slabs, rows, _ = y_tm.shape
    seq = rows // n_batch
    width = n_slabs * LANES
    return pl.pallas_call(
        functools.partial(_glu_kernel, ts=ts, n_batch=n_batch),
        grid=(seq // ts,),
        in_specs=[
            pl.BlockSpec((n_slabs, ts * n_batch, LANES), lambda i: (0, i, 0)),
            _const_spec(w_glu.shape),
            _const_spec((1, width)),
        ],
        out_specs=pl.BlockSpec((n_batch, ts, width), lambda i: (0, i, 0)),
        out_shape=jax.ShapeDtypeStruct((n_batch, seq, width), BF16),
        compiler_params=_params("arbitrary"),
        name="glu",
    )(y_tm, w_glu, b_glu.reshape(1, width))


def _attn_kernel(qi_ref, kwq_ref, kwf_ref, q_ref, k_ref, vt_ref, o_ref,
                 sc_ref, lg_ref, pos_ref, ot_ref, *, tq, n_keep, idx_w_scale, qk_scale):
    i = pl.program_id(1)
    n_chunks = i + 1
    nt = (((1,), (1,)), ((), ()))
    fold = lambda a: a.reshape(tq // ACC_ROWS, ACC_ROWS, tq)

    key_pos = lax.broadcasted_iota(I32, (tq, tq), 0)
    qry_pos = lax.broadcasted_iota(I32, (tq, tq), 1)
    pos_ref[...] = key_pos.astype(F32)
    wi_t = kwq_ref[...].T[IDX_DIM:IDX_DIM + N_IDX_HEADS, :] * idx_w_scale

    def score_chunk(j, carry):
        off = pl.multiple_of(j * tq, tq)
        kic = kwf_ref[pl.ds(off, tq), :][:, :IDX_DIM].astype(BF16)
        s = jnp.zeros((tq, tq), F32)
        for h in range(N_IDX_HEADS):
            r = lax.dot_general(kic, qi_ref[:, h * IDX_DIM:(h + 1) * IDX_DIM], nt,
                                preferred_element_type=F32)
            s = s + wi_t[h:h + 1, :] * jnp.maximum(r, 0.0)
        causal = (j - i) * tq + key_pos <= qry_pos
        sc_ref[j] = jnp.where(causal, s, -jnp.inf)
        return carry

    lax.fori_loop(0, n_chunks, score_chunk, 0)

    def decode(key):
        return pltpu.bitcast(jnp.where(key < 0, key ^ jnp.int32(0x7FFFFFFF), key), F32)

    def count_ge(thr):
        thr_b = jnp.broadcast_to(thr, (ACC_ROWS, tq))

        def body(j, acc):
            return acc + jnp.sum(jnp.where(fold(sc_ref[j]) >= thr_b, 1.0, 0.0), axis=0)

        acc = lax.fori_loop(0, n_chunks, body, jnp.zeros((ACC_ROWS, tq), F32))
        return jnp.sum(acc, axis=0, keepdims=True)

    keep = jnp.float32(n_keep)
    tau = jnp.where(count_ge(jnp.zeros((1, tq), F32)) >= keep, jnp.int32(0), jnp.int32(INT_MIN))

    def bit_body(bi, tau):
        cand = tau + jnp.left_shift(jnp.int32(1), jnp.int32(30) - bi)
        return jnp.where(count_ge(decode(cand)) >= keep, cand, tau)

    tau = lax.fori_loop(0, 31, bit_body, tau)
    thr = jnp.where(tau == jnp.int32(INT_MIN), jnp.finfo(F32).min, decode(tau))

    def mask_chunk(j, carry):
        sc_ref[j] = jnp.where(sc_ref[j] >= thr, 0.0, -1e30)
        return carry

    lax.fori_loop(0, n_chunks, mask_chunk, 0)

    for c in range(N_KV_HEADS):
        heads = range(c * GQA_GROUP, (c + 1) * GQA_GROUP)
        slopes = [2.0 ** (-8.0 * (hd + 1) / N_HEADS) for hd in heads]
        q4 = jnp.concatenate([q_ref[:, hd * HEAD_DIM:(hd + 1) * HEAD_DIM] for hd in heads], axis=0)

        def chunk_shift(j, slope):
            return ((j - i) * tq).astype(F32) * slope

        def logits_chunk(j, mx, c=c, slopes=slopes, q4=q4):
            off = pl.multiple_of(j * tq, tq)
            kc = k_ref[pl.ds(off, tq), :][:, c * HEAD_DIM:(c + 1) * HEAD_DIM]
            lg4 = lax.dot_general(kc, q4, nt, preferred_element_type=F32) * qk_scale
            mask = sc_ref[j]
            pos = pos_ref[...]
            out = []
            for g, slope in enumerate(slopes):
                lg = lg4[:, g * tq:(g + 1) * tq] + slope * pos + mask
                lg_ref[j, g] = lg
                cmax = jnp.max(fold(lg), axis=0) + chunk_shift(j, slope)
                out.append(jnp.maximum(mx[g], cmax))
            return tuple(out)

        mx0 = tuple(jnp.full((ACC_ROWS, tq), -jnp.inf, F32) for _ in heads)
        mx = lax.fori_loop(0, n_chunks, logits_chunk, mx0)
        m = [jnp.max(a, axis=0, keepdims=True) for a in mx]

        ot_ref[...] = jnp.zeros_like(ot_ref)

        def pv_chunk(j, ls, c=c, slopes=slopes, m=m):
            vt = vt_ref[j, c * HEAD_DIM:(c + 1) * HEAD_DIM, :]
            out = []
            for g, slope in enumerate(slopes):
                p = jnp.exp(lg_ref[j, g] - (m[g] - chunk_shift(j, slope)))
                out.append(ls[g] + jnp.sum(fold(p), axis=0))
                ot_ref[g] += jnp.dot(vt, p.astype(BF16), preferred_element_type=F32)
            return tuple(out)

        ls0 = tuple(jnp.zeros((ACC_ROWS, tq), F32) for _ in heads)
        ls = lax.fori_loop(0, n_chunks, pv_chunk, ls0)
        for g, hd in enumerate(heads):
            denom = jnp.sum(ls[g], axis=0, keepdims=True)
            o_ref[:, hd * HEAD_DIM:(hd + 1) * HEAD_DIM] = (ot_ref[g] / denom).T.astype(BF16)


def _attention(qi, kw, q, k, vt):
    n_batch, seq, _ = q.shape
    n_chunks, kv_w, tq = vt.shape[1:]
    n_keep = min(INDEX_TOPK, seq // 4)
    blk = lambda w: pl.BlockSpec((None, tq, w), lambda b, i: (b, i, 0))
    whole = lambda w: pl.BlockSpec((None, seq, w), lambda b, i: (b, 0, 0))
    return pl.pallas_call(
        functools.partial(_attn_kernel, tq=tq, n_keep=n_keep,
                          idx_w_scale=(N_IDX_HEADS ** -0.5) * (IDX_DIM ** -0.5),
                          qk_scale=HEAD_DIM ** -0.5),
        grid=(n_batch, n_chunks),
        in_specs=[blk(qi.shape[2]), blk(kw.shape[2]), whole(kw.shape[2]), blk(q.shape[2]),
                  whole(k.shape[2]),
                  pl.BlockSpec((None, n_chunks, kv_w, tq), lambda b, i: (b, 0, 0, 0))],
        out_specs=blk(q.shape[2]),
        out_shape=jax.ShapeDtypeStruct(q.shape, BF16),
        scratch_shapes=[pltpu.VMEM((n_chunks, tq, tq), F32),
                        pltpu.VMEM((n_chunks, GQA_GROUP, tq, tq), F32),
                        pltpu.VMEM((tq, tq), F32),
                        pltpu.VMEM((GQA_GROUP, HEAD_DIM, tq), F32)],
        compiler_params=_params("arbitrary", "arbitrary"),
        name="sparse_attn",
    )(qi, kw, kw, q, k, vt)


def _out_proj_kernel(ys_ref, ya_ref, x_ref, ws_ref, wa_ref, g_ref, b_ref, o_ref):
    mix = jnp.dot(ys_ref[...], ws_ref[...], preferred_element_type=F32)
    mix = mix + jnp.dot(ya_ref[...], wa_ref[...], preferred_element_type=F32)
    o_ref[...] = _layer_norm(DEEPNORM_ALPHA * x_ref[...] + mix, g_ref[...], b_ref[...])


def _out_proj(ys, ya, x, ws, wa, g, b, *, tm):
    m, d = x.shape
    blk = lambda w: pl.BlockSpec((tm, w), lambda i: (i, 0))
    return pl.pallas_call(
        _out_proj_kernel,
        grid=(m // tm,),
        in_specs=[blk(ys.shape[1]), blk(ya.shape[1]), blk(d), _const_spec(ws.shape),
                  _const_spec(wa.shape), _const_spec((1, d)), _const_spec((1, d))],
        out_specs=blk(d),
        out_shape=jax.ShapeDtypeStruct((m, d), F32),
        compiler_params=_params("arbitrary"),
        name="out_proj_ln1",
    )(ys, ya, x, ws, wa, g.reshape(1, d), b.reshape(1, d))


def _ffn_kernel(h_ref, halo_ref, wup_ref, wgate_ref, wdown_ref, cw_ref, cb_ref, g_ref, b_ref,
                o_ref, hb_ref, halo_b_ref, acc_ref):
    i = pl.program_id(1)
    j = pl.program_id(2)

    @pl.when(j == 0)
    def _():
        hb_ref[...] = h_ref[...].astype(BF16)
        halo_b_ref[...] = jnp.where(i == 0, 0.0, halo_ref[...]).astype(BF16)
        acc_ref[...] = jnp.zeros_like(acc_ref)

    hb = hb_ref[...]
    tm = hb.shape[0]
    wup = wup_ref[...]
    up = jnp.dot(hb, wup, preferred_element_type=F32)
    up_halo = jnp.dot(halo_b_ref[...], wup, preferred_element_type=F32)
    ext = jnp.concatenate([up_halo, up], axis=0)
    cw = cw_ref[...]
    hc = cb_ref[...] + cw[2:3, :] * up
    for lag in range(1, CONV_WIDTH):
        start = SUBLANES - lag
        hc = hc + cw[CONV_WIDTH - 1 - lag:CONV_WIDTH - lag, :] * ext[start:start + tm, :]
    gate = jnp.dot(hb, wgate_ref[...], preferred_element_type=F32)
    act = (_gelu(hc) * gate).astype(BF16)
    acc_ref[...] += jnp.dot(act, wdown_ref[...], preferred_element_type=F32)

    @pl.when(j == pl.num_programs(2) - 1)
    def _():
        o_ref[...] = _layer_norm(DEEPNORM_ALPHA * h_ref[...] + acc_ref[...], g_ref[...], b_ref[...])


def _ffn(h, w_up, w_gate, w_down, conv_w, conv_b, g, b, *, tm, tf):
    n_batch, seq, d = h.shape
    f = w_up.shape[1]
    halo_blocks = tm // SUBLANES
    return pl.pallas_call(
        _ffn_kernel,
        grid=(n_batch, seq // tm, f // tf),
        in_specs=[
            pl.BlockSpec((None, tm, d), lambda bb, i, j: (bb, i, 0)),
            pl.BlockSpec((None, SUBLANES, d),
                         lambda bb, i, j: (bb, jnp.maximum(i * halo_blocks - 1, 0), 0)),
            pl.BlockSpec((d, tf), lambda bb, i, j: (0, j)),
            pl.BlockSpec((d, tf), lambda bb, i, j: (0, j)),
            pl.BlockSpec((tf, d), lambda bb, i, j: (j, 0)),
            pl.BlockSpec((CONV_WIDTH, tf), lambda bb, i, j: (0, j)),
            pl.BlockSpec((1, tf), lambda bb, i, j: (0, j)),
            _const_spec((1, d)),
            _const_spec((1, d)),
        ],
        out_specs=pl.BlockSpec((None, tm, d), lambda bb, i, j: (bb, i, 0)),
        out_shape=jax.ShapeDtypeStruct((n_batch, seq, d), F32),
        scratch_shapes=[pltpu.VMEM((tm, d), BF16), pltpu.VMEM((SUBLANES, d), BF16),
                        pltpu.VMEM((tm, d), F32)],
        compiler_params=_params("arbitrary", "arbitrary", "arbitrary"),
        name="ffn_ln2",
    )(h, h, w_up, w_gate, w_down, conv_w, conv_b.reshape(1, f), g.reshape(1, d), b.reshape(1, d))


def _layer(h, w_in, a_re, a_im, log_dt, b_re, b_im, c_re, c_im, ssm_d, w_glu, b_glu, w_out,
           ln1_g, ln1_b, w_up, w_gate, conv_w, conv_b, w_down, ln2_g, ln2_b):
    n_batch, seq, d = h.shape
    assert n_batch == SUBLANES, "the time-major scan layout puts the batch on the sublanes"
    n_groups, n_state = a_re.shape
    ssm_w = n_groups * SSM_GROUP_CH
    attn_w = N_HEADS * HEAD_DIM
    kv_w = N_KV_HEADS * HEAD_DIM
    qi_w = N_IDX_HEADS * IDX_DIM
    n_slabs = ssm_w // LANES
    assert n_state == SSM_STATE and ssm_w % LANES == 0
    t_seq, t_row, t_ff = _tiles(seq, w_up.shape[1])

    o = 0
    parts = []
    for w in (ssm_w, attn_w, kv_w, kv_w, qi_w):
        parts.append(w_in[:, o:o + w].astype(BF16))
        o += w
    tail = w_in[:, o:]
    assert tail.shape[1] == IDX_DIM + N_IDX_HEADS
    wkw = jnp.pad(tail, ((0, 0), (0, LANES - tail.shape[1]))).astype(BF16)
    u_tm, q, k, vt, qi, kw = _in_proj(h, *parts, wkw, ts=t_seq)

    abar_re, abar_im, bbar_re, bbar_im = _s5_discretize(
        a_re, a_im, log_dt, jnp.swapaxes(b_re, 1, 2), jnp.swapaxes(b_im, 1, 2))
    gl = GROUPS_PER_SLAB
    to_slabs = lambda m: m.reshape(n_slabs, gl, *m.shape[1:])
    bbd = jnp.concatenate([_block_diag_slabs(to_slabs(bbar_re)),
                           _block_diag_slabs(to_slabs(bbar_im))], axis=-1).astype(BF16)
    cbd_re = _block_diag_slabs(to_slabs(jnp.swapaxes(c_re, 1, 2)))
    cbd_im = _block_diag_slabs(to_slabs(jnp.swapaxes(c_im, 1, 2)))
    cbd = jnp.concatenate([cbd_re, -cbd_im], axis=1).astype(BF16)
    bcast = lambda m: jnp.broadcast_to(m.reshape(n_slabs, 1, gl * n_state), (n_slabs, n_batch, gl * n_state))
    y_tm = _s5_scan(u_tm, bbd, bcast(abar_re), bcast(abar_im), cbd,
                    ssm_d.reshape(n_slabs, 1, LANES), ts=t_seq, n_batch=n_batch)
    y_ssm = _glu(y_tm, w_glu.astype(BF16), b_glu, ts=t_seq, n_batch=n_batch)

    y_attn = _attention(qi, kw, q, k, vt)

    w_out_b = w_out.astype(BF16)
    h1 = _out_proj(y_ssm.reshape(n_batch * seq, ssm_w), y_attn.reshape(n_batch * seq, attn_w),
                   h.reshape(n_batch * seq, d), w_out_b[:ssm_w], w_out_b[ssm_w:], ln1_g, ln1_b, tm=t_row)
    return _ffn(h1.reshape(n_batch, seq, d), w_up.astype(BF16), w_gate.astype(BF16),
                w_down.astype(BF16), conv_w, conv_b, ln2_g, ln2_b, tm=t_row, tf=t_ff)


def kernel(x, w_in, ssm_a_re, ssm_a_im, ssm_log_dt, ssm_b_re, ssm_b_im, ssm_c_re, ssm_c_im, ssm_d,
           w_glu, b_glu, w_out, ln1_g, ln1_b, w_up, w_gate, conv_w, conv_b, w_down, ln2_g, ln2_b):
    h = x
    for l in range(w_in.shape[0]):
        h = _layer(h, w_in[l], ssm_a_re[l], ssm_a_im[l], ssm_log_dt[l], ssm_b_re[l], ssm_b_im[l],
                   ssm_c_re[l], ssm_c_im[l], ssm_d[l], w_glu[l], b_glu[l], w_out[l], ln1_g[l],
                   ln1_b[l], w_up[l], w_gate[l], conv_w[l], conv_b[l], w_down[l], ln2_g[l], ln2_b[l])
    return h
```

```python
import functools
import math

import jax
import jax.numpy as jnp
from jax import lax
from jax.experimental import pallas as pl
from jax.experimental.pallas import tpu as pltpu

F32 = jnp.float32
BF16 = jnp.bfloat16
I32 = jnp.int32

LANES = 128
SUBLANES = 8
VMEM_LIMIT_BYTES = 56 * 1024 * 1024

SSM_GROUP_CH = 16
SSM_STATE = 64
N_HEADS = 8
HEAD_DIM = 128
N_KV_HEADS = 2
GQA_GROUP = N_HEADS // N_KV_HEADS
N_IDX_HEADS = 16
IDX_DIM = 64
INDEX_TOPK = 256
CONV_WIDTH = 3
LN_EPS = 1e-5
DEPTH = 1
DEEPNORM_ALPHA = (2.0 * DEPTH) ** 0.25

GROUPS_PER_SLAB = LANES // SSM_GROUP_CH
SLAB_STATE = GROUPS_PER_SLAB * SSM_STATE
INT_MIN = -(2 ** 31)
ACC_ROWS = 4 * SUBLANES


def _tiles(seq, d_ff):
    t_seq = min(256, seq)
    t_row = min(512, seq)
    t_ff = 512 if d_ff % 512 == 0 else LANES
    assert seq % t_seq == 0 and seq % t_row == 0 and d_ff % t_ff == 0
    return t_seq, t_row, t_ff


def _params(*sem):
    return pltpu.CompilerParams(dimension_semantics=sem, vmem_limit_bytes=VMEM_LIMIT_BYTES)


def _const_spec(shape):
    zeros = (0,) * len(shape)
    return pl.BlockSpec(shape, lambda *_: zeros, pipeline_mode=pl.Buffered(1))


def _layer_norm(v, g, b):
    mu = jnp.mean(v, axis=-1, keepdims=True)
    vc = v - mu
    var = jnp.mean(vc * vc, axis=-1, keepdims=True)
    return vc * lax.rsqrt(var + LN_EPS) * g + b


def _gelu(v):
    return 0.5 * v * (1.0 + lax.erf(v * (1.0 / math.sqrt(2.0))))


def _in_proj_kernel(x_ref, wu_ref, wq_ref, wk_ref, wv_ref, wqi_ref, wkw_ref,
                    u_ref, q_ref, k_ref, v_ref, qi_ref, kw_ref, *, n_batch):
    b = pl.program_id(1)
    xb = x_ref[...].astype(BF16)
    ts = xb.shape[0]
    u = jnp.dot(xb, wu_ref[...], preferred_element_type=F32)
    for s in range(u.shape[1] // LANES):
        u_ref[s, pl.ds(b, ts, stride=n_batch), :] = u[:, s * LANES:(s + 1) * LANES]
    q_ref[...] = jnp.dot(xb, wq_ref[...], preferred_element_type=F32).astype(BF16)
    k_ref[...] = jnp.dot(xb, wk_ref[...], preferred_element_type=F32).astype(BF16)
    v_ref[...] = jnp.dot(xb, wv_ref[...], preferred_element_type=F32).T.astype(BF16)
    qi_ref[...] = jnp.dot(xb, wqi_ref[...], preferred_element_type=F32).astype(BF16)
    kw_ref[...] = jnp.dot(xb, wkw_ref[...], preferred_element_type=F32)


def _in_proj(x, wu, wq, wk, wv, wqi, wkw, *, ts):
    n_batch, seq, d = x.shape
    n_slabs = wu.shape[1] // LANES
    grid = (seq // ts, n_batch)
    row = lambda w: pl.BlockSpec((None, ts, w), lambda i, b: (b, i, 0))
    out_shape = (
        jax.ShapeDtypeStruct((n_slabs, seq * n_batch, LANES), F32),
        jax.ShapeDtypeStruct((n_batch, seq, wq.shape[1]), BF16),
        jax.ShapeDtypeStruct((n_batch, seq, wk.shape[1]), BF16),
        jax.ShapeDtypeStruct((n_batch, seq // ts, wv.shape[1], ts), BF16),
        jax.ShapeDtypeStruct((n_batch, seq, wqi.shape[1]), BF16),
        jax.ShapeDtypeStruct((n_batch, seq, wkw.shape[1]), F32),
    )
    return pl.pallas_call(
        functools.partial(_in_proj_kernel, n_batch=n_batch),
        grid=grid,
        in_specs=[row(d)] + [_const_spec(w.shape) for w in (wu, wq, wk, wv, wqi, wkw)],
        out_specs=(
            pl.BlockSpec((n_slabs, ts * n_batch, LANES), lambda i, b: (0, i, 0)),
            row(wq.shape[1]), row(wk.shape[1]),
            pl.BlockSpec((None, None, wv.shape[1], ts), lambda i, b: (b, i, 0, 0)),
            row(wqi.shape[1]), row(wkw.shape[1]),
        ),
        out_shape=out_shape,
        compiler_params=_params("arbitrary", "arbitrary"),
        name="in_proj",
    )(x, wu, wq, wk, wv, wqi, wkw)


def _s5_discretize_kernel(a_re_ref, a_im_ref, log_dt_ref, bt_re_ref, bt_im_ref,
                          abar_re_ref, abar_im_ref, bbar_re_ref, bbar_im_ref):
    a_re = a_re_ref[...]
    a_im = a_im_ref[...]
    dt = jnp.exp(log_dt_ref[...])
    mag = jnp.exp(dt * a_re)
    ang = dt * a_im
    abar_re = mag * jnp.cos(ang)
    abar_im = mag * jnp.sin(ang)
    num_re = abar_re - 1.0
    num_im = abar_im
    den = a_re * a_re + a_im * a_im
    f_re = (num_re * a_re + num_im * a_im) / den
    f_im = (num_im * a_re - num_re * a_im) / den
    abar_re_ref[...] = abar_re
    abar_im_ref[...] = abar_im
    bt_re = bt_re_ref[...]
    bt_im = bt_im_ref[...]
    bbar_re_ref[...] = f_re * bt_re - f_im * bt_im
    bbar_im_ref[...] = f_re * bt_im + f_im * bt_re


def _s5_discretize(a_re, a_im, log_dt, bt_re, bt_im):
    g, p = a_re.shape
    abar_re, abar_im, bbar_re, bbar_im = pl.pallas_call(
        _s5_discretize_kernel,
        out_shape=(jax.ShapeDtypeStruct((g, 1, p), F32), jax.ShapeDtypeStruct((g, 1, p), F32),
                   jax.ShapeDtypeStruct(bt_re.shape, F32), jax.ShapeDtypeStruct(bt_re.shape, F32)),
        name="s5_discretize",
    )(a_re.reshape(g, 1, p), a_im.reshape(g, 1, p), log_dt.reshape(g, 1, 1), bt_re, bt_im)
    return abar_re.reshape(g, p), abar_im.reshape(g, p), bbar_re, bbar_im


def _block_diag_slabs(m):
    n, gl, a, b = m.shape
    eye = jnp.eye(gl, dtype=m.dtype)
    full = m[:, :, :, None, :] * eye[None, :, None, :, None]
    return full.reshape(n, gl * a, gl * b)


def _s5_scan_kernel(u_ref, bbd_ref, ar_ref, ai_ref, cbd_ref, d_ref, y_ref, h_ref, carry_ref,
                    *, ts, n_batch):
    @pl.when(pl.program_id(1) == 0)
    def _():
        carry_ref[...] = jnp.zeros_like(carry_ref)

    u = u_ref[...]
    h_ref[...] = jnp.dot(u.astype(BF16), bbd_ref[...], preferred_element_type=F32)
    ar = ar_ref[...]
    ai = ai_ref[...]
    half = ar.shape[-1]

    def step(t, c):
        re, im = c
        r0 = pl.multiple_of(t * n_batch, n_batch)
        bu = h_ref[pl.ds(r0, n_batch), :]
        nre = ar * re - ai * im + bu[:, :half]
        nim = ar * im + ai * re + bu[:, half:]
        h_ref[pl.ds(r0, n_batch), :] = jnp.concatenate([nre, nim], axis=-1)
        return nre, nim

    re, im = lax.fori_loop(0, ts, step, (carry_ref[:, :half], carry_ref[:, half:]), unroll=4)
    carry_ref[...] = jnp.concatenate([re, im], axis=-1)
    y = jnp.dot(h_ref[...].astype(BF16), cbd_ref[...], preferred_element_type=F32) + d_ref[...] * u
    y_ref[...] = _gelu(y)


def _s5_scan(u_tm, bbd, ar, ai, cbd, d, *, ts, n_batch):
    n_slabs, rows, _ = u_tm.shape
    tr = ts * n_batch
    n_state2 = bbd.shape[-1]
    slab = lambda shape: pl.BlockSpec((None,) + shape, lambda s, i: (s, 0, 0))
    return pl.pallas_call(
        functools.partial(_s5_scan_kernel, ts=ts, n_batch=n_batch),
        grid=(n_slabs, rows // tr),
        in_specs=[
            pl.BlockSpec((None, tr, LANES), lambda s, i: (s, i, 0)),
            slab((LANES, n_state2)),
            slab((n_batch, n_state2 // 2)),
            slab((n_batch, n_state2 // 2)),
            slab((n_state2, LANES)),
            slab((1, LANES)),
        ],
        out_specs=pl.BlockSpec((None, tr, LANES), lambda s, i: (s, i, 0)),
        out_shape=jax.ShapeDtypeStruct(u_tm.shape, F32),
        scratch_shapes=[pltpu.VMEM((tr, n_state2), F32), pltpu.VMEM((n_batch, n_state2), F32)],
        compiler_params=_params("arbitrary", "arbitrary"),
        name="s5_scan",
    )(u_tm, bbd, ar, ai, cbd, d)


def _glu_kernel(y_ref, w_ref, b_ref, o_ref, *, ts, n_batch):
    n_slabs = y_ref.shape[0]
    w = w_ref[...]
    bias = b_ref[...]
    for b in range(n_batch):
        yb = jnp.concatenate(
            [y_ref[s, pl.ds(b, ts, stride=n_batch), :] for s in range(n_slabs)], axis=-1)
        z = jnp.dot(yb.astype(BF16), w, preferred_element_type=F32) + bias
        o_ref[b] = (yb * jax.nn.sigmoid(z)).astype(BF16)


def _glu(y_tm, w_glu, b_glu, *, ts, n_batch):
    n_slabs, rows, _ = y_tm.shape
    seq = rows // n_batch
    width = n_slabs * LANES
    return pl.pallas_call(
        functools.partial(_glu_kernel, ts=ts, n_batch=n_batch),
        grid=(seq // ts,),
        in_specs=[
            pl.BlockSpec((n_slabs, ts * n_batch, LANES), lambda i: (0, i, 0)),
            _const_spec(w_glu.shape),
            _const_spec((1, width)),
        ],
        out_specs=pl.BlockSpec((n_batch, ts, width), lambda i: (0, i, 0)),
        out_shape=jax.ShapeDtypeStruct((n_batch, seq, width), BF16),
        compiler_params=_params("arbitrary"),
        name="glu",
    )(y_tm, w_glu, b_glu.reshape(1, width))


def _attn_kernel(qi_ref, kwq_ref, kwf_ref, q_ref, k_ref, vt_ref, o_ref,
                 sc_ref, lg_ref, pos_ref, ot_ref, *, tq, n_keep, idx_w_scale, qk_scale):
    i = pl.program_id(1)
    n_chunks = i + 1
    nt = (((1,), (1,)), ((), ()))
    fold = lambda a: a.reshape(tq // ACC_ROWS, ACC_ROWS, tq)

    key_pos = lax.broadcasted_iota(I32, (tq, tq), 0)
    qry_pos = lax.broadcasted_iota(I32, (tq, tq), 1)
    pos_ref[...] = key_pos.astype(F32)
    wi_t = kwq_ref[...].T[IDX_DIM:IDX_DIM + N_IDX_HEADS, :] * idx_w_scale

    def score_chunk(j, carry):
        off = pl.multiple_of(j * tq, tq)
        kic = kwf_ref[pl.ds(off, tq), :][:, :IDX_DIM].astype(BF16)
        s = jnp.zeros((tq, tq), F32)
        for h in range(N_IDX_HEADS):
            r = lax.dot_general(kic, qi_ref[:, h * IDX_DIM:(h + 1) * IDX_DIM], nt,
                                preferred_element_type=F32)
            s = s + wi_t[h:h + 1, :] * jnp.maximum(r, 0.0)
        causal = (j - i) * tq + key_pos <= qry_pos
        sc_ref[j] = jnp.where(causal, s, -jnp.inf)
        return carry

    lax.fori_loop(0, n_chunks, score_chunk, 0)

    def decode(key):
        return pltpu.bitcast(jnp.where(key < 0, key ^ jnp.int32(0x7FFFFFFF), key), F32)

    def count_ge(thr, strict=False):
        thr_b = jnp.broadcast_to(thr, (ACC_ROWS, tq))

        def body(j, acc):
            s = fold(sc_ref[j])
            hit = s > thr_b if strict else s >= thr_b
            return acc + jnp.sum(jnp.where(hit, 1.0, 0.0), axis=0)

        acc = lax.fori_loop(0, n_chunks, body, jnp.zeros((ACC_ROWS, tq), F32))
        return jnp.sum(acc, axis=0, keepdims=True)

    keep = jnp.float32(n_keep)
    cnt0 = count_ge(jnp.zeros((1, tq), F32))
    tau = jnp.where(cnt0 >= keep, jnp.int32(0), jnp.int32(INT_MIN))
    cnt = jnp.where(cnt0 >= keep, cnt0, 0.0)

    def bit_body(bi, carry):
        tau, cnt = carry
        cand = tau + jnp.left_shift(jnp.int32(1), jnp.int32(30) - bi)
        c = count_ge(decode(cand))
        return jnp.where(c >= keep, cand, tau), jnp.where(c >= keep, c, cnt)

    tau, cnt = lax.fori_loop(0, 31, bit_body, (tau, cnt))
    thr = jnp.where(tau == jnp.int32(INT_MIN), jnp.finfo(F32).min, decode(tau))

    @pl.when(jnp.max(cnt) > keep)
    def _():
        need = keep - count_ge(thr, strict=True)
        lower = jnp.where(key_pos >= qry_pos, 1.0, 0.0).astype(BF16)

        def tie_chunk(j, run):
            s = sc_ref[j]
            eq = s == thr
            seen = run + jnp.dot(lower, jnp.where(eq, 1.0, 0.0).astype(BF16),
                                 preferred_element_type=F32)
            sc_ref[j] = jnp.where(eq, jnp.where(seen > need, -jnp.inf, s), s)
            return seen[tq - 1:tq, :]

        lax.fori_loop(0, n_chunks, tie_chunk, jnp.zeros((1, tq), F32))

    def mask_chunk(j, carry):
        sc_ref[j] = jnp.where(sc_ref[j] >= thr, 0.0, -1e30)
        return carry

    lax.fori_loop(0, n_chunks, mask_chunk, 0)

    for c in range(N_KV_HEADS):
        heads = range(c * GQA_GROUP, (c + 1) * GQA_GROUP)
        slopes = [2.0 ** (-8.0 * (hd + 1) / N_HEADS) for hd in heads]
        q4 = jnp.concatenate([q_ref[:, hd * HEAD_DIM:(hd + 1) * HEAD_DIM] for hd in heads], axis=0)

        def chunk_shift(j, slope):
            return ((j - i) * tq).astype(F32) * slope

        def logits_chunk(j, mx, c=c, slopes=slopes, q4=q4):
            off = pl.multiple_of(j * tq, tq)
            kc = k_ref[pl.ds(off, tq), :][:, c * HEAD_DIM:(c + 1) * HEAD_DIM]
            lg4 = lax.dot_general(kc, q4, nt, preferred_element_type=F32) * qk_scale
            mask = sc_ref[j]
            pos = pos_ref[...]
            out = []
            for g, slope in enumerate(slopes):
                lg = lg4[:, g * tq:(g + 1) * tq] + slope * pos + mask
                lg_ref[j, g] = lg
                cmax = jnp.max(fold(lg), axis=0) + chunk_shift(j, slope)
                out.append(jnp.maximum(mx[g], cmax))
            return tuple(out)

        mx0 = tuple(jnp.full((ACC_ROWS, tq), -jnp.inf, F32) for _ in heads)
        mx = lax.fori_loop(0, n_chunks, logits_chunk, mx0)
        m = [jnp.max(a, axis=0, keepdims=True) for a in mx]

        ot_ref[...] = jnp.zeros_like(ot_ref)

        def pv_chunk(j, ls, c=c, slopes=slopes, m=m):
            vt = vt_ref[j, c * HEAD_DIM:(c + 1) * HEAD_DIM, :]
            out = []
            for g, slope in enumerate(slopes):
                p = jnp.exp(lg_ref[j, g] - (m[g] - chunk_shift(j, slope)))
                out.append(ls[g] + jnp.sum(fold(p), axis=0))
                ot_ref[g] += jnp.dot(vt, p.astype(BF16), preferred_element_type=F32)
            return tuple(out)

        ls0 = tuple(jnp.zeros((ACC_ROWS, tq), F32) for _ in heads)
        ls = lax.fori_loop(0, n_chunks, pv_chunk, ls0)
        for g, hd in enumerate(heads):
            denom = jnp.sum(ls[g], axis=0, keepdims=True)
            o_ref[:, hd * HEAD_DIM:(hd + 1) * HEAD_DIM] = (ot_ref[g] / denom).T.astype(BF16)


def _attention(qi, kw, q, k, vt):
    n_batch, seq, _ = q.shape
    n_chunks, kv_w, tq = vt.shape[1:]
    n_keep = min(INDEX_TOPK, seq // 4)
    blk = lambda w: pl.BlockSpec((None, tq, w), lambda b, i: (b, i, 0))
    whole = lambda w: pl.BlockSpec((None, seq, w), lambda b, i: (b, 0, 0))
    return pl.pallas_call(
        functools.partial(_attn_kernel, tq=tq, n_keep=n_keep,
                          idx_w_scale=(N_IDX_HEADS ** -0.5) * (IDX_DIM ** -0.5),
                          qk_scale=HEAD_DIM ** -0.5),
        grid=(n_batch, n_chunks),
        in_specs=[blk(qi.shape[2]), blk(kw.shape[2]), whole(kw.shape[2]), blk(q.shape[2]),
                  whole(k.shape[2]),
                  pl.BlockSpec((None, n_chunks, kv_w, tq), lambda b, i: (b, 0, 0, 0))],
        out_specs=blk(q.shape[2]),
        out_shape=jax.ShapeDtypeStruct(q.shape, BF16),
        scratch_shapes=[pltpu.VMEM((n_chunks, tq, tq), F32),
                        pltpu.VMEM((n_chunks, GQA_GROUP, tq, tq), F32),
                        pltpu.VMEM((tq, tq), F32),
                        pltpu.VMEM((GQA_GROUP, HEAD_DIM, tq), F32)],
        compiler_params=_params("arbitrary", "arbitrary"),
        name="sparse_attn",
    )(qi, kw, kw, q, k, vt)


def _out_proj_kernel(ys_ref, ya_ref, x_ref, ws_ref, wa_ref, g_ref, b_ref, o_ref):
    mix = jnp.dot(ys_ref[...], ws_ref[...], preferred_element_type=F32)
    mix = mix + jnp.dot(ya_ref[...], wa_ref[...], preferred_element_type=F32)
    o_ref[...] = _layer_norm(DEEPNORM_ALPHA * x_ref[...] + mix, g_ref[...], b_ref[...])


def _out_proj(ys, ya, x, ws, wa, g, b, *, tm):
    m, d = x.shape
    blk = lambda w: pl.BlockSpec((tm, w), lambda i: (i, 0))
    return pl.pallas_call(
        _out_proj_kernel,
        grid=(m // tm,),
        in_specs=[blk(ys.shape[1]), blk(ya.shape[1]), blk(d), _const_spec(ws.shape),
                  _const_spec(wa.shape), _const_spec((1, d)), _const_spec((1, d))],
        out_specs=blk(d),
        out_shape=jax.ShapeDtypeStruct((m, d), F32),
        compiler_params=_params("arbitrary"),
        name="out_proj_ln1",
    )(ys, ya, x, ws, wa, g.reshape(1, d), b.reshape(1, d))


def _ffn_kernel(h_ref, halo_ref, wup_ref, wgate_ref, wdown_ref, cw_ref, cb_ref, g_ref, b_ref,
                o_ref, hb_ref, halo_b_ref, acc_ref):
    i = pl.program_id(1)
    j = pl.program_id(2)

    @pl.when(j == 0)
    def _():
        hb_ref[...] = h_ref[...].astype(BF16)
        halo_b_ref[...] = jnp.where(i == 0, 0.0, halo_ref[...]).astype(BF16)
        acc_ref[...] = jnp.zeros_like(acc_ref)

    hb = hb_ref[...]
    tm = hb.shape[0]
    wup = wup_ref[...]
    up = jnp.dot(hb, wup, preferred_element_type=F32)
    up_halo = jnp.dot(halo_b_ref[...], wup, preferred_element_type=F32)
    ext = jnp.concatenate([up_halo, up], axis=0)
    cw = cw_ref[...]
    hc = cb_ref[...] + cw[2:3, :] * up
    for lag in range(1, CONV_WIDTH):
        start = SUBLANES - lag
        hc = hc + cw[CONV_WIDTH - 1 - lag:CONV_WIDTH - lag, :] * ext[start:start + tm, :]
    gate = jnp.dot(hb, wgate_ref[...], preferred_element_type=F32)
    act = (_gelu(hc) * gate).astype(BF16)
    acc_ref[...] += jnp.dot(act, wdown_ref[...], preferred_element_type=F32)

    @pl.when(j == pl.num_programs(2) - 1)
    def _():
        o_ref[...] = _layer_norm(DEEPNORM_ALPHA * h_ref[...] + acc_ref[...], g_ref[...], b_ref[...])


def _ffn(h, w_up, w_gate, w_down, conv_w, conv_b, g, b, *, tm, tf):
    n_batch, seq, d = h.shape
    f = w_up.shape[1]
    halo_blocks = tm // SUBLANES
    return pl.pallas_call(
        _ffn_kernel,
        grid=(n_batch, seq // tm, f // tf),
        in_specs=[
            pl.BlockSpec((None, tm, d), lambda bb, i, j: (bb, i, 0)),
            pl.BlockSpec((None, SUBLANES, d),
                         lambda bb, i, j: (bb, jnp.maximum(i * halo_blocks - 1, 0), 0)),
            pl.BlockSpec((d, tf), lambda bb, i, j: (0, j)),
            pl.BlockSpec((d, tf), lambda bb, i, j: (0, j)),
            pl.BlockSpec((tf, d), lambda bb, i, j: (j, 0)),
            pl.BlockSpec((CONV_WIDTH, tf), lambda bb, i, j: (0, j)),
            pl.BlockSpec((1, tf), lambda bb, i, j: (0, j)),
            _const_spec((1, d)),
            _const_spec((1, d)),
        ],
        out_specs=pl.BlockSpec((None, tm, d), lambda bb, i, j: (bb, i, 0)),
        out_shape=jax.ShapeDtypeStruct((n_batch, seq, d), F32),
        scratch_shapes=[pltpu.VMEM((tm, d), BF16), pltpu.VMEM((SUBLANES, d), BF16),
                        pltpu.VMEM((tm, d), F32)],
        compiler_params=_params("arbitrary", "arbitrary", "arbitrary"),
        name="ffn_ln2",
    )(h, h, w_up, w_gate, w_down, conv_w, conv_b.reshape(1, f), g.reshape(1, d), b.reshape(1, d))


def _layer(h, w_in, a_re, a_im, log_dt, b_re, b_im, c_re, c_im, ssm_d, w_glu, b_glu, w_out,
           ln1_g, ln1_b, w_up, w_gate, conv_w, conv_b, w_down, ln2_g, ln2_b):
    n_batch, seq, d = h.shape
    assert n_batch == SUBLANES, "the time-major scan layout puts the batch on the sublanes"
    n_groups, n_state = a_re.shape
    ssm_w = n_groups * SSM_GROUP_CH
    attn_w = N_HEADS * HEAD_DIM
    kv_w = N_KV_HEADS * HEAD_DIM
    qi_w = N_IDX_HEADS * IDX_DIM
    n_slabs = ssm_w // LANES
    assert n_state == SSM_STATE and ssm_w % LANES == 0
    t_seq, t_row, t_ff = _tiles(seq, w_up.shape[1])

    o = 0
    parts = []
    for w in (ssm_w, attn_w, kv_w, kv_w, qi_w):
        parts.append(w_in[:, o:o + w].astype(BF16))
        o += w
    tail = w_in[:, o:]
    assert tail.shape[1] == IDX_DIM + N_IDX_HEADS
    wkw = jnp.pad(tail, ((0, 0), (0, LANES - tail.shape[1]))).astype(BF16)
    u_tm, q, k, vt, qi, kw = _in_proj(h, *parts, wkw, ts=t_seq)

    abar_re, abar_im, bbar_re, bbar_im = _s5_discretize(
        a_re, a_im, log_dt, jnp.swapaxes(b_re, 1, 2), jnp.swapaxes(b_im, 1, 2))
    gl = GROUPS_PER_SLAB
    to_slabs = lambda m: m.reshape(n_slabs, gl, *m.shape[1:])
    bbd = jnp.concatenate([_block_diag_slabs(to_slabs(bbar_re)),
                           _block_diag_slabs(to_slabs(bbar_im))], axis=-1).astype(BF16)
    cbd_re = _block_diag_slabs(to_slabs(jnp.swapaxes(c_re, 1, 2)))
    cbd_im = _block_diag_slabs(to_slabs(jnp.swapaxes(c_im, 1, 2)))
    cbd = jnp.concatenate([cbd_re, -cbd_im], axis=1).astype(BF16)
    bcast = lambda m: jnp.broadcast_to(m.reshape(n_slabs, 1, gl * n_state), (n_slabs, n_batch, gl * n_state))
    y_tm = _s5_scan(u_tm, bbd, bcast(abar_re), bcast(abar_im), cbd,
                    ssm_d.reshape(n_slabs, 1, LANES), ts=t_seq, n_batch=n_batch)
    y_ssm = _glu(y_tm, w_glu.astype(BF16), b_glu, ts=t_seq, n_batch=n_batch)

    y_attn = _attention(qi, kw, q, k, vt)

    w_out_b = w_out.astype(BF16)
    h1 = _out_proj(y_ssm.reshape(n_batch * seq, ssm_w), y_attn.reshape(n_batch * seq, attn_w),
                   h.reshape(n_batch * seq, d), w_out_b[:ssm_w], w_out_b[ssm_w:], ln1_g, ln1_b, tm=t_row)
    return _ffn(h1.reshape(n_batch, seq, d), w_up.astype(BF16), w_gate.astype(BF16),
                w_down.astype(BF16), conv_w, conv_b, ln2_g, ln2_b, tm=t_row, tf=t_ff)


def kernel(x, w_in, ssm_a_re, ssm_a_im, ssm_log_dt, ssm_b_re, ssm_b_im, ssm_c_re, ssm_c_im, ssm_d,
           w_glu, b_glu, w_out, ln1_g, ln1_b, w_up, w_gate, conv_w, conv_b, w_down, ln2_g, ln2_b):
    h = x
    for l in range(w_in.shape[0]):
        h = _layer(h, w_in[l], ssm_a_re[l], ssm_a_im[l], ssm_log_dt[l], ssm_b_re[l], ssm_b_im[l],
                   ssm_c_re[l], ssm_c_im[l], ssm_d[l], w_glu[l], b_glu[l], w_out[l], ln1_g[l],
                   ln1_b[l], w_up[l], w_gate[l], conv_w[l], conv_b[l], w_down[l], ln2_g[l], ln2_b[l])
    return h
```

```python
import functools
import math

import jax
import jax.numpy as jnp
from jax import lax
from jax.experimental import pallas as pl
from jax.experimental.pallas import tpu as pltpu

F32 = jnp.float32
BF16 = jnp.bfloat16
I32 = jnp.int32

LANES = 128
SUBLANES = 8
VMEM_LIMIT_BYTES = 56 * 1024 * 1024

SSM_GROUP_CH = 16
SSM_STATE = 64
N_HEADS = 8
HEAD_DIM = 128
N_KV_HEADS = 2
GQA_GROUP = N_HEADS // N_KV_HEADS
N_IDX_HEADS = 16
IDX_DIM = 64
INDEX_TOPK = 256
CONV_WIDTH = 3
LN_EPS = 1e-5
DEPTH = 1
DEEPNORM_ALPHA = (2.0 * DEPTH) ** 0.25

GROUPS_PER_SLAB = LANES // SSM_GROUP_CH
SLAB_STATE = GROUPS_PER_SLAB * SSM_STATE
INT_MIN = -(2 ** 31)
ACC_ROWS = 4 * SUBLANES
LN_ROWS = 128


def _tiles(seq, d_ff):
    t_seq = min(256, seq)
    t_row = min(512, seq)
    t_ff = 512 if d_ff % 512 == 0 else LANES
    t_scan = min(128, seq)
    assert seq % t_seq == 0 and seq % t_row == 0 and d_ff % t_ff == 0 and seq % t_scan == 0
    return t_seq, t_row, t_ff, t_scan


def _params(*sem):
    return pltpu.CompilerParams(dimension_semantics=sem, vmem_limit_bytes=VMEM_LIMIT_BYTES)


def _const_spec(shape):
    zeros = (0,) * len(shape)
    return pl.BlockSpec(shape, lambda *_: zeros, pipeline_mode=pl.Buffered(1))


def _layer_norm(v, g, b):
    mu = jnp.mean(v, axis=-1, keepdims=True)
    vc = v - mu
    var = jnp.mean(vc * vc, axis=-1, keepdims=True)
    return vc * lax.rsqrt(var + LN_EPS) * g + b


def _gelu(v):
    return 0.5 * v * (1.0 + lax.erf(v * (1.0 / math.sqrt(2.0))))


def _in_proj_kernel(x_ref, wu_ref, wq_ref, wk_ref, wv_ref, wqi_ref, wkw_ref,
                    u_ref, q_ref, k_ref, v_ref, qi_ref, kw_ref, *, n_batch):
    b = pl.program_id(1)
    xb = x_ref[...].astype(BF16)
    ts = xb.shape[0]
    u = jnp.dot(xb, wu_ref[...], preferred_element_type=F32)
    for s in range(u.shape[1] // LANES):
        u_ref[s, pl.ds(b, ts, stride=n_batch), :] = u[:, s * LANES:(s + 1) * LANES]
    q_ref[...] = jnp.dot(xb, wq_ref[...], preferred_element_type=F32).astype(BF16)
    k_ref[...] = jnp.dot(xb, wk_ref[...], preferred_element_type=F32).astype(BF16)
    v_ref[...] = jnp.dot(xb, wv_ref[...], preferred_element_type=F32).T.astype(BF16)
    qi_ref[...] = jnp.dot(xb, wqi_ref[...], preferred_element_type=F32).astype(BF16)
    kw_ref[...] = jnp.dot(xb, wkw_ref[...], preferred_element_type=F32)


def _in_proj(x, wu, wq, wk, wv, wqi, wkw, *, ts):
    n_batch, seq, d = x.shape
    n_slabs = wu.shape[1] // LANES
    grid = (seq // ts, n_batch)
    row = lambda w: pl.BlockSpec((None, ts, w), lambda i, b: (b, i, 0))
    out_shape = (
        jax.ShapeDtypeStruct((n_slabs, seq * n_batch, LANES), F32),
        jax.ShapeDtypeStruct((n_batch, seq, wq.shape[1]), BF16),
        jax.ShapeDtypeStruct((n_batch, seq, wk.shape[1]), BF16),
        jax.ShapeDtypeStruct((n_batch, seq // ts, wv.shape[1], ts), BF16),
        jax.ShapeDtypeStruct((n_batch, seq, wqi.shape[1]), BF16),
        jax.ShapeDtypeStruct((n_batch, seq, wkw.shape[1]), F32),
    )
    return pl.pallas_call(
        functools.partial(_in_proj_kernel, n_batch=n_batch),
        grid=grid,
        in_specs=[row(d)] + [_const_spec(w.shape) for w in (wu, wq, wk, wv, wqi, wkw)],
        out_specs=(
            pl.BlockSpec((n_slabs, ts * n_batch, LANES), lambda i, b: (0, i, 0)),
            row(wq.shape[1]), row(wk.shape[1]),
            pl.BlockSpec((None, None, wv.shape[1], ts), lambda i, b: (b, i, 0, 0)),
            row(wqi.shape[1]), row(wkw.shape[1]),
        ),
        out_shape=out_shape,
        compiler_params=_params("arbitrary", "arbitrary"),
        name="in_proj",
    )(x, wu, wq, wk, wv, wqi, wkw)


def _s5_discretize_kernel(a_re_ref, a_im_ref, log_dt_ref, bt_re_ref, bt_im_ref,
                          abar_re_ref, abar_im_ref, bbar_re_ref, bbar_im_ref):
    a_re = a_re_ref[...]
    a_im = a_im_ref[...]
    dt = jnp.exp(log_dt_ref[...])
    mag = jnp.exp(dt * a_re)
    ang = dt * a_im
    abar_re = mag * jnp.cos(ang)
    abar_im = mag * jnp.sin(ang)
    num_re = abar_re - 1.0
    num_im = abar_im
    den = a_re * a_re + a_im * a_im
    f_re = (num_re * a_re + num_im * a_im) / den
    f_im = (num_im * a_re - num_re * a_im) / den
    abar_re_ref[...] = abar_re
    abar_im_ref[...] = abar_im
    bt_re = bt_re_ref[...]
    bt_im = bt_im_ref[...]
    bbar_re_ref[...] = f_re * bt_re - f_im * bt_im
    bbar_im_ref[...] = f_re * bt_im + f_im * bt_re


def _s5_discretize(a_re, a_im, log_dt, bt_re, bt_im):
    g, p = a_re.shape
    abar_re, abar_im, bbar_re, bbar_im = pl.pallas_call(
        _s5_discretize_kernel,
        out_shape=(jax.ShapeDtypeStruct((g, 1, p), F32), jax.ShapeDtypeStruct((g, 1, p), F32),
                   jax.ShapeDtypeStruct(bt_re.shape, F32), jax.ShapeDtypeStruct(bt_re.shape, F32)),
        name="s5_discretize",
    )(a_re.reshape(g, 1, p), a_im.reshape(g, 1, p), log_dt.reshape(g, 1, 1), bt_re, bt_im)
    return abar_re.reshape(g, p), abar_im.reshape(g, p), bbar_re, bbar_im


def _block_diag_slabs(m):
    n, gl, a, b = m.shape
    eye = jnp.eye(gl, dtype=m.dtype)
    full = m[:, :, :, None, :] * eye[None, :, None, :, None]
    return full.reshape(n, gl * a, gl * b)


def _s5_scan_kernel(u_next_ref, u_prev_ref, bbd_ref, ar_ref, ai_ref, cbd_ref, d_ref, y_ref,
                    buf_a, buf_b, carry_ref, *, ts, n_batch, tiles_per_slab, n_tiles):
    n = pl.program_id(0)
    last = n_tiles - 1
    cur = jnp.minimum(n, last)
    slab_next = jnp.minimum(n + 1, last) // tiles_per_slab
    slab_cur = cur // tiles_per_slab
    slab_prev = jnp.maximum(n - 1, 0) // tiles_per_slab
    half = ar_ref.shape[-1]

    @pl.when(n == 0)
    def _():
        buf_a[...] = jnp.dot(u_prev_ref[...].astype(BF16), bbd_ref[0], preferred_element_type=F32)
        buf_b[...] = jnp.zeros_like(buf_b)
        carry_ref[...] = jnp.zeros_like(carry_ref)

    def stages(cur_buf, other_buf):
        y = jnp.dot(other_buf[...].astype(BF16), cbd_ref[slab_prev], preferred_element_type=F32)
        y_ref[...] = _gelu(y + d_ref[slab_prev] * u_prev_ref[...])
        other_buf[...] = jnp.dot(u_next_ref[...].astype(BF16), bbd_ref[slab_next],
                                 preferred_element_type=F32)
        ar = ar_ref[slab_cur]
        ai = ai_ref[slab_cur]
        first = cur % tiles_per_slab == 0
        re = jnp.where(first, 0.0, carry_ref[:, :half])
        im = jnp.where(first, 0.0, carry_ref[:, half:])
        for t in range(ts):
            rows = slice(t * n_batch, (t + 1) * n_batch)
            bu = cur_buf[rows, :]
            re, im = ar * re - ai * im + bu[:, :half], ar * im + ai * re + bu[:, half:]
            cur_buf[rows, :] = jnp.concatenate([re, im], axis=-1)
        carry_ref[...] = jnp.concatenate([re, im], axis=-1)

    @pl.when(n % 2 == 0)
    def _():
        stages(buf_a, buf_b)

    @pl.when(n % 2 == 1)
    def _():
        stages(buf_b, buf_a)


def _s5_scan(u_tm, bbd, ar, ai, cbd, d, *, ts, n_batch):
    n_slabs, rows, _ = u_tm.shape
    tr = ts * n_batch
    tiles_per_slab = rows // tr
    n_tiles = n_slabs * tiles_per_slab
    n_state2 = bbd.shape[-1]
    tile = lambda shift: pl.BlockSpec(
        (tr, LANES), lambda n: (jnp.clip(n + shift, 0, n_tiles - 1), 0))
    u_flat = u_tm.reshape(n_slabs * rows, LANES)
    y = pl.pallas_call(
        functools.partial(_s5_scan_kernel, ts=ts, n_batch=n_batch,
                          tiles_per_slab=tiles_per_slab, n_tiles=n_tiles),
        grid=(n_tiles + 1,),
        in_specs=[tile(1), tile(-1), _const_spec(bbd.shape), _const_spec(ar.shape),
                  _const_spec(ai.shape), _const_spec(cbd.shape), _const_spec(d.shape)],
        out_specs=tile(-1),
        out_shape=jax.ShapeDtypeStruct(u_flat.shape, F32),
        scratch_shapes=[pltpu.VMEM((tr, n_state2), F32), pltpu.VMEM((tr, n_state2), F32),
                        pltpu.VMEM((n_batch, n_state2), F32)],
        compiler_params=_params("arbitrary"),
        name="s5_scan",
    )(u_flat, u_flat, bbd, ar, ai, cbd, d)
    return y.reshape(u_tm.shape)


def _glu_kernel(y_ref, w_ref, b_ref, o_ref, *, ts, n_batch):
    n_slabs = y_ref.shape[0]
    w = w_ref[...]
    bias = b_ref[...]
    for b in range(n_batch):
        yb = jnp.concatenate(
            [y_ref[s, pl.ds(b, ts, stride=n_batch), :] for s in range(n_slabs)], axis=-1)
        z = jnp.dot(yb.astype(BF16), w, preferred_element_type=F32) + bias
        o_ref[b] = (yb * jax.nn.sigmoid(z)).astype(BF16)


def _glu(y_tm, w_glu, b_glu, *, ts, n_batch):
    n_slabs, rows, _ = y_tm.shape
    seq = rows // n_batch
    width = n_slabs * LANES
    return pl.pallas_call(
        functools.partial(_glu_kernel, ts=ts, n_batch=n_batch),
        grid=(seq // ts,),
        in_specs=[
            pl.BlockSpec((n_slabs, ts * n_batch, LANES), lambda i: (0, i, 0)),
            _const_spec(w_glu.shape),
            _const_spec((1, width)),
        ],
        out_specs=pl.BlockSpec((n_batch, ts, width), lambda i: (0, i, 0)),
        out_shape=jax.ShapeDtypeStruct((n_batch, seq, width), BF16),
        compiler_params=_params("arbitrary"),
        name="glu",
    )(y_tm, w_glu, b_glu.reshape(1, width))


def _attn_kernel(qi_ref, kwq_ref, kwf_ref, q_ref, k_ref, vt_ref, o_ref,
                 sc_ref, lg_ref, pos_ref, ot_ref, *, tq, n_keep, idx_w_scale, qk_scale):
    i = pl.program_id(1)
    n_chunks = i + 1
    nt = (((1,), (1,)), ((), ()))
    fold = lambda a: a.reshape(tq // ACC_ROWS, ACC_ROWS, tq)

    key_pos = lax.broadcasted_iota(I32, (tq, tq), 0)
    qry_pos = lax.broadcasted_iota(I32, (tq, tq), 1)
    pos_ref[...] = key_pos.astype(F32)
    wi_t = kwq_ref[...].T[IDX_DIM:IDX_DIM + N_IDX_HEADS, :] * idx_w_scale

    def score_chunk(j, carry):
        off = pl.multiple_of(j * tq, tq)
        kic = kwf_ref[pl.ds(off, tq), :][:, :IDX_DIM].astype(BF16)
        s = jnp.zeros((tq, tq), F32)
        for h in range(N_IDX_HEADS):
            r = lax.dot_general(kic, qi_ref[:, h * IDX_DIM:(h + 1) * IDX_DIM], nt,
                                preferred_element_type=F32)
            s = s + wi_t[h:h + 1, :] * jnp.maximum(r, 0.0)
        causal = (j - i) * tq + key_pos <= qry_pos
        sc_ref[j] = jnp.where(causal, s, -jnp.inf)
        return carry

    lax.fori_loop(0, n_chunks, score_chunk, 0)

    def decode(key):
        return pltpu.bitcast(jnp.where(key < 0, key ^ jnp.int32(0x7FFFFFFF), key), F32)

    def count_ge(thr, strict=False):
        thr_b = jnp.broadcast_to(thr, (ACC_ROWS, tq))

        def body(j, acc):
            s = fold(sc_ref[j])
            hit = s > thr_b if strict else s >= thr_b
            return acc + jnp.sum(jnp.where(hit, 1.0, 0.0), axis=0)

        acc = lax.fori_loop(0, n_chunks, body, jnp.zeros((ACC_ROWS, tq), F32))
        return jnp.sum(acc, axis=0, keepdims=True)

    keep = jnp.float32(n_keep)
    cnt0 = count_ge(jnp.zeros((1, tq), F32))
    tau = jnp.where(cnt0 >= keep, jnp.int32(0), jnp.int32(INT_MIN))
    cnt = jnp.where(cnt0 >= keep, cnt0, 0.0)

    def bit_body(bi, carry):
        tau, cnt = carry
        cand = tau + jnp.left_shift(jnp.int32(1), jnp.int32(30) - bi)
        c = count_ge(decode(cand))
        return jnp.where(c >= keep, cand, tau), jnp.where(c >= keep, c, cnt)

    tau, cnt = lax.fori_loop(0, 31, bit_body, (tau, cnt))
    thr = jnp.where(tau == jnp.int32(INT_MIN), jnp.finfo(F32).min, decode(tau))

    @pl.when(jnp.max(cnt) > keep)
    def _():
        need = keep - count_ge(thr, strict=True)
        lower = jnp.where(key_pos >= qry_pos, 1.0, 0.0).astype(BF16)

        def tie_chunk(j, run):
            s = sc_ref[j]
            eq = s == thr
            seen = run + jnp.dot(lower, jnp.where(eq, 1.0, 0.0).astype(BF16),
                                 preferred_element_type=F32)
            sc_ref[j] = jnp.where(eq, jnp.where(seen > need, -jnp.inf, s), s)
            return seen[tq - 1:tq, :]

        lax.fori_loop(0, n_chunks, tie_chunk, jnp.zeros((1, tq), F32))

    def mask_chunk(j, carry):
        sc_ref[j] = jnp.where(sc_ref[j] >= thr, 0.0, -1e30)
        return carry

    lax.fori_loop(0, n_chunks, mask_chunk, 0)

    for c in range(N_KV_HEADS):
        heads = range(c * GQA_GROUP, (c + 1) * GQA_GROUP)
        slopes = [2.0 ** (-8.0 * (hd + 1) / N_HEADS) for hd in heads]
        q4 = jnp.concatenate([q_ref[:, hd * HEAD_DIM:(hd + 1) * HEAD_DIM] for hd in heads], axis=0)

        def chunk_shift(j, slope):
            return ((j - i) * tq).astype(F32) * slope

        def logits_chunk(j, mx, c=c, slopes=slopes, q4=q4):
            off = pl.multiple_of(j * tq, tq)
            kc = k_ref[pl.ds(off, tq), :][:, c * HEAD_DIM:(c + 1) * HEAD_DIM]
            lg4 = lax.dot_general(kc, q4, nt, preferred_element_type=F32) * qk_scale
            mask = sc_ref[j]
            pos = pos_ref[...]
            out = []
            for g, slope in enumerate(slopes):
                lg = lg4[:, g * tq:(g + 1) * tq] + slope * pos + mask
                lg_ref[j, g] = lg
                cmax = jnp.max(fold(lg), axis=0) + chunk_shift(j, slope)
                out.append(jnp.maximum(mx[g], cmax))
            return tuple(out)

        mx0 = tuple(jnp.full((ACC_ROWS, tq), -jnp.inf, F32) for _ in heads)
        mx = lax.fori_loop(0, n_chunks, logits_chunk, mx0)
        m = [jnp.max(a, axis=0, keepdims=True) for a in mx]

        ot_ref[...] = jnp.zeros_like(ot_ref)

        def pv_chunk(j, ls, c=c, slopes=slopes, m=m):
            vt = vt_ref[j, c * HEAD_DIM:(c + 1) * HEAD_DIM, :]
            out = []
            for g, slope in enumerate(slopes):
                p = jnp.exp(lg_ref[j, g] - (m[g] - chunk_shift(j, slope)))
                out.append(ls[g] + jnp.sum(fold(p), axis=0))
                ot_ref[g] += jnp.dot(vt, p.astype(BF16), preferred_element_type=F32)
            return tuple(out)

        ls0 = tuple(jnp.zeros((ACC_ROWS, tq), F32) for _ in heads)
        ls = lax.fori_loop(0, n_chunks, pv_chunk, ls0)
        for g, hd in enumerate(heads):
            denom = jnp.sum(ls[g], axis=0, keepdims=True)
            o_ref[:, hd * HEAD_DIM:(hd + 1) * HEAD_DIM] = (ot_ref[g] / denom).T.astype(BF16)


def _attention(qi, kw, q, k, vt):
    n_batch, seq, _ = q.shape
    n_chunks, kv_w, tq = vt.shape[1:]
    n_keep = min(INDEX_TOPK, seq // 4)
    blk = lambda w: pl.BlockSpec((None, tq, w), lambda b, i: (b, i, 0))
    whole = lambda w: pl.BlockSpec((None, seq, w), lambda b, i: (b, 0, 0))
    return pl.pallas_call(
        functools.partial(_attn_kernel, tq=tq, n_keep=n_keep,
                          idx_w_scale=(N_IDX_HEADS ** -0.5) * (IDX_DIM ** -0.5),
                          qk_scale=HEAD_DIM ** -0.5),
        grid=(n_batch, n_chunks),
        in_specs=[blk(qi.shape[2]), blk(kw.shape[2]), whole(kw.shape[2]), blk(q.shape[2]),
                  whole(k.shape[2]),
                  pl.BlockSpec((None, n_chunks, kv_w, tq), lambda b, i: (b, 0, 0, 0))],
        out_specs=blk(q.shape[2]),
        out_shape=jax.ShapeDtypeStruct(q.shape, BF16),
        scratch_shapes=[pltpu.VMEM((n_chunks, tq, tq), F32),
                        pltpu.VMEM((n_chunks, GQA_GROUP, tq, tq), F32),
                        pltpu.VMEM((tq, tq), F32),
                        pltpu.VMEM((GQA_GROUP, HEAD_DIM, tq), F32)],
        compiler_params=_params("arbitrary", "arbitrary"),
        name="sparse_attn",
    )(qi, kw, kw, q, k, vt)


def _out_proj_kernel(ys_ref, ya_ref, x_ref, ws_ref, wa_ref, g_ref, b_ref, o_ref):
    for r in range(0, x_ref.shape[0], LN_ROWS):
        rows = slice(r, r + LN_ROWS)
        mix = jnp.dot(ys_ref[rows, :], ws_ref[...], preferred_element_type=F32)
        mix = mix + jnp.dot(ya_ref[rows, :], wa_ref[...], preferred_element_type=F32)
        o_ref[rows, :] = _layer_norm(DEEPNORM_ALPHA * x_ref[rows, :] + mix, g_ref[...], b_ref[...])


def _out_proj(ys, ya, x, ws, wa, g, b, *, tm):
    m, d = x.shape
    blk = lambda w: pl.BlockSpec((tm, w), lambda i: (i, 0))
    return pl.pallas_call(
        _out_proj_kernel,
        grid=(m // tm,),
        in_specs=[blk(ys.shape[1]), blk(ya.shape[1]), blk(d), _const_spec(ws.shape),
                  _const_spec(wa.shape), _const_spec((1, d)), _const_spec((1, d))],
        out_specs=blk(d),
        out_shape=jax.ShapeDtypeStruct((m, d), F32),
        compiler_params=_params("arbitrary"),
        name="out_proj_ln1",
    )(ys, ya, x, ws, wa, g.reshape(1, d), b.reshape(1, d))


def _ffn_kernel(h_ref, halo_ref, wup_ref, wgate_ref, wdown_ref, cw_ref, cb_ref, g_ref, b_ref,
                o_ref, hb_ref, halo_b_ref, acc_ref):
    i = pl.program_id(1)
    j = pl.program_id(2)

    @pl.when(j == 0)
    def _():
        hb_ref[...] = h_ref[...].astype(BF16)
        halo_b_ref[...] = jnp.where(i == 0, 0.0, halo_ref[...]).astype(BF16)
        acc_ref[...] = jnp.zeros_like(acc_ref)

    hb = hb_ref[...]
    tm = hb.shape[0]
    wup = wup_ref[...]
    up = jnp.dot(hb, wup, preferred_element_type=F32)
    up_halo = jnp.dot(halo_b_ref[...], wup, preferred_element_type=F32)
    ext = jnp.concatenate([up_halo, up], axis=0)
    cw = cw_ref[...]
    hc = cb_ref[...] + cw[2:3, :] * up
    for lag in range(1, CONV_WIDTH):
        start = SUBLANES - lag
        hc = hc + cw[CONV_WIDTH - 1 - lag:CONV_WIDTH - lag, :] * ext[start:start + tm, :]
    gate = jnp.dot(hb, wgate_ref[...], preferred_element_type=F32)
    act = (_gelu(hc) * gate).astype(BF16)
    acc_ref[...] += jnp.dot(act, wdown_ref[...], preferred_element_type=F32)

    @pl.when(j == pl.num_programs(2) - 1)
    def _():
        o_ref[...] = _layer_norm(DEEPNORM_ALPHA * h_ref[...] + acc_ref[...], g_ref[...], b_ref[...])


def _ffn(h, w_up, w_gate, w_down, conv_w, conv_b, g, b, *, tm, tf):
    n_batch, seq, d = h.shape
    f = w_up.shape[1]
    halo_blocks = tm // SUBLANES
    return pl.pallas_call(
        _ffn_kernel,
        grid=(n_batch, seq // tm, f // tf),
        in_specs=[
            pl.BlockSpec((None, tm, d), lambda bb, i, j: (bb, i, 0)),
            pl.BlockSpec((None, SUBLANES, d),
                         lambda bb, i, j: (bb, jnp.maximum(i * halo_blocks - 1, 0), 0)),
            pl.BlockSpec((d, tf), lambda bb, i, j: (0, j)),
            pl.BlockSpec((d, tf), lambda bb, i, j: (0, j)),
            pl.BlockSpec((tf, d), lambda bb, i, j: (j, 0)),
            pl.BlockSpec((CONV_WIDTH, tf), lambda bb, i, j: (0, j)),
            pl.BlockSpec((1, tf), lambda bb, i, j: (0, j)),
            _const_spec((1, d)),
            _const_spec((1, d)),
        ],
        out_specs=pl.BlockSpec((None, tm, d), lambda bb, i, j: (bb, i, 0)),
        out_shape=jax.ShapeDtypeStruct((n_batch, seq, d), F32),
        scratch_shapes=[pltpu.VMEM((tm, d), BF16), pltpu.VMEM((SUBLANES, d), BF16),
                        pltpu.VMEM((tm, d), F32)],
        compiler_params=_params("arbitrary", "arbitrary", "arbitrary"),
        name="ffn_ln2",
    )(h, h, w_up, w_gate, w_down, conv_w, conv_b.reshape(1, f), g.reshape(1, d), b.reshape(1, d))


def _layer(h, w_in, a_re, a_im, log_dt, b_re, b_im, c_re, c_im, ssm_d, w_glu, b_glu, w_out,
           ln1_g, ln1_b, w_up, w_gate, conv_w, conv_b, w_down, ln2_g, ln2_b):
    n_batch, seq, d = h.shape
    assert n_batch == SUBLANES, "the time-major scan layout puts the batch on the sublanes"
    n_groups, n_state = a_re.shape
    ssm_w = n_groups * SSM_GROUP_CH
    attn_w = N_HEADS * HEAD_DIM
    kv_w = N_KV_HEADS * HEAD_DIM
    qi_w = N_IDX_HEADS * IDX_DIM
    n_slabs = ssm_w // LANES
    assert n_state == SSM_STATE and ssm_w % LANES == 0
    t_seq, t_row, t_ff, t_scan = _tiles(seq, w_up.shape[1])

    o = 0
    parts = []
    for w in (ssm_w, attn_w, kv_w, kv_w, qi_w):
        parts.append(w_in[:, o:o + w].astype(BF16))
        o += w
    tail = w_in[:, o:]
    assert tail.shape[1] == IDX_DIM + N_IDX_HEADS
    wkw = jnp.pad(tail, ((0, 0), (0, LANES - tail.shape[1]))).astype(BF16)
    u_tm, q, k, vt, qi, kw = _in_proj(h, *parts, wkw, ts=t_seq)

    abar_re, abar_im, bbar_re, bbar_im = _s5_discretize(
        a_re, a_im, log_dt, jnp.swapaxes(b_re, 1, 2), jnp.swapaxes(b_im, 1, 2))
    gl = GROUPS_PER_SLAB
    to_slabs = lambda m: m.reshape(n_slabs, gl, *m.shape[1:])
    bbd = jnp.concatenate([_block_diag_slabs(to_slabs(bbar_re)),
                           _block_diag_slabs(to_slabs(bbar_im))], axis=-1).astype(BF16)
    cbd_re = _block_diag_slabs(to_slabs(jnp.swapaxes(c_re, 1, 2)))
    cbd_im = _block_diag_slabs(to_slabs(jnp.swapaxes(c_im, 1, 2)))
    cbd = jnp.concatenate([cbd_re, -cbd_im], axis=1).astype(BF16)
    bcast = lambda m: jnp.broadcast_to(m.reshape(n_slabs, 1, gl * n_state), (n_slabs, n_batch, gl * n_state))
    y_tm = _s5_scan(u_tm, bbd, bcast(abar_re), bcast(abar_im), cbd,
                    ssm_d.reshape(n_slabs, 1, LANES), ts=t_scan, n_batch=n_batch)
    y_ssm = _glu(y_tm, w_glu.astype(BF16), b_glu, ts=t_seq, n_batch=n_batch)

    y_attn = _attention(qi, kw, q, k, vt)

    w_out_b = w_out.astype(BF16)
    h1 = _out_proj(y_ssm.reshape(n_batch * seq, ssm_w), y_attn.reshape(n_batch * seq, attn_w),
                   h.reshape(n_batch * seq, d), w_out_b[:ssm_w], w_out_b[ssm_w:], ln1_g, ln1_b, tm=t_row)
    return _ffn(h1.reshape(n_batch, seq, d), w_up.astype(BF16), w_gate.astype(BF16),
                w_down.astype(BF16), conv_w, conv_b, ln2_g, ln2_b, tm=t_row, tf=t_ff)


def kernel(x, w_in, ssm_a_re, ssm_a_im, ssm_log_dt, ssm_b_re, ssm_b_im, ssm_c_re, ssm_c_im, ssm_d,
           w_glu, b_glu, w_out, ln1_g, ln1_b, w_up, w_gate, conv_w, conv_b, w_down, ln2_g, ln2_b):
    h = x
    for l in range(w_in.shape[0]):
        h = _layer(h, w_in[l], ssm_a_re[l], ssm_a_im[l], ssm_log_dt[l], ssm_b_re[l], ssm_b_im[l],
                   ssm_c_re[l], ssm_c_im[l], ssm_d[l], w_glu[l], b_glu[l], w_out[l], ln1_g[l],
                   ln1_b[l], w_up[l], w_gate[l], conv_w[l], conv_b[l], w_down[l], ln2_g[l], ln2_b[l])
    return h
```

```python
import functools
import math

import jax
import jax.numpy as jnp
from jax import lax
from jax.experimental import pallas as pl
from jax.experimental.pallas import tpu as pltpu

F32 = jnp.float32
BF16 = jnp.bfloat16
I32 = jnp.int32

LANES = 128
SUBLANES = 8
VMEM_LIMIT_BYTES = 56 * 1024 * 1024

SSM_GROUP_CH = 16
SSM_STATE = 64
N_HEADS = 8
HEAD_DIM = 128
N_KV_HEADS = 2
GQA_GROUP = N_HEADS // N_KV_HEADS
N_IDX_HEADS = 16
IDX_DIM = 64
INDEX_TOPK = 256
CONV_WIDTH = 3
LN_EPS = 1e-5
DEPTH = 1
DEEPNORM_ALPHA = (2.0 * DEPTH) ** 0.25

GROUPS_PER_SLAB = LANES // SSM_GROUP_CH
SLAB_STATE = GROUPS_PER_SLAB * SSM_STATE
INT_MIN = -(2 ** 31)
ACC_ROWS = 4 * SUBLANES
LN_ROWS = 128


def _tiles(seq, d_ff):
    t_seq = min(256, seq)
    t_row = min(512, seq)
    t_ff = 512 if d_ff % 512 == 0 else LANES
    t_scan = min(128, seq)
    assert seq % t_seq == 0 and seq % t_row == 0 and d_ff % t_ff == 0 and seq % t_scan == 0
    return t_seq, t_row, t_ff, t_scan


def _params(*sem):
    return pltpu.CompilerParams(dimension_semantics=sem, vmem_limit_bytes=VMEM_LIMIT_BYTES)


def _const_spec(shape):
    zeros = (0,) * len(shape)
    return pl.BlockSpec(shape, lambda *_: zeros, pipeline_mode=pl.Buffered(1))


def _layer_norm(v, g, b):
    mu = jnp.mean(v, axis=-1, keepdims=True)
    vc = v - mu
    var = jnp.mean(vc * vc, axis=-1, keepdims=True)
    return vc * lax.rsqrt(var + LN_EPS) * g + b


def _gelu(v):
    return 0.5 * v * (1.0 + lax.erf(v * (1.0 / math.sqrt(2.0))))


def _in_proj_kernel(x_ref, wu_ref, wq_ref, wk_ref, wv_ref, wqi_ref, wkw_ref,
                    u_ref, q_ref, k_ref, v_ref, qi_ref, kw_ref, *, n_batch):
    b = pl.program_id(1)
    xb = x_ref[...].astype(BF16)
    ts = xb.shape[0]
    u = jnp.dot(xb, wu_ref[...], preferred_element_type=F32)
    for s in range(u.shape[1] // LANES):
        u_ref[s, pl.ds(b, ts, stride=n_batch), :] = u[:, s * LANES:(s + 1) * LANES]
    q_ref[...] = jnp.dot(xb, wq_ref[...], preferred_element_type=F32).astype(BF16)
    k_ref[...] = jnp.dot(xb, wk_ref[...], preferred_element_type=F32).astype(BF16)
    v_ref[...] = jnp.dot(xb, wv_ref[...], preferred_element_type=F32).T.astype(BF16)
    qi_ref[...] = jnp.dot(xb, wqi_ref[...], preferred_element_type=F32).astype(BF16)
    kw_ref[...] = jnp.dot(xb, wkw_ref[...], preferred_element_type=F32)


def _in_proj(x, wu, wq, wk, wv, wqi, wkw, *, ts):
    n_batch, seq, d = x.shape
    n_slabs = wu.shape[1] // LANES
    grid = (seq // ts, n_batch)
    row = lambda w: pl.BlockSpec((None, ts, w), lambda i, b: (b, i, 0))
    out_shape = (
        jax.ShapeDtypeStruct((n_slabs, seq * n_batch, LANES), F32),
        jax.ShapeDtypeStruct((n_batch, seq, wq.shape[1]), BF16),
        jax.ShapeDtypeStruct((n_batch, seq, wk.shape[1]), BF16),
        jax.ShapeDtypeStruct((n_batch, seq // ts, wv.shape[1], ts), BF16),
        jax.ShapeDtypeStruct((n_batch, seq, wqi.shape[1]), BF16),
        jax.ShapeDtypeStruct((n_batch, seq, wkw.shape[1]), F32),
    )
    return pl.pallas_call(
        functools.partial(_in_proj_kernel, n_batch=n_batch),
        grid=grid,
        in_specs=[row(d)] + [_const_spec(w.shape) for w in (wu, wq, wk, wv, wqi, wkw)],
        out_specs=(
            pl.BlockSpec((n_slabs, ts * n_batch, LANES), lambda i, b: (0, i, 0)),
            row(wq.shape[1]), row(wk.shape[1]),
            pl.BlockSpec((None, None, wv.shape[1], ts), lambda i, b: (b, i, 0, 0)),
            row(wqi.shape[1]), row(wkw.shape[1]),
        ),
        out_shape=out_shape,
        compiler_params=_params("arbitrary", "arbitrary"),
        name="in_proj",
    )(x, wu, wq, wk, wv, wqi, wkw)


def _s5_discretize_kernel(a_re_ref, a_im_ref, log_dt_ref, bt_re_ref, bt_im_ref,
                          abar_re_ref, abar_im_ref, bbar_re_ref, bbar_im_ref):
    a_re = a_re_ref[...]
    a_im = a_im_ref[...]
    dt = jnp.exp(log_dt_ref[...])
    mag = jnp.exp(dt * a_re)
    ang = dt * a_im
    abar_re = mag * jnp.cos(ang)
    abar_im = mag * jnp.sin(ang)
    num_re = abar_re - 1.0
    num_im = abar_im
    den = a_re * a_re + a_im * a_im
    f_re = (num_re * a_re + num_im * a_im) / den
    f_im = (num_im * a_re - num_re * a_im) / den
    abar_re_ref[...] = abar_re
    abar_im_ref[...] = abar_im
    bt_re = bt_re_ref[...]
    bt_im = bt_im_ref[...]
    bbar_re_ref[...] = f_re * bt_re - f_im * bt_im
    bbar_im_ref[...] = f_re * bt_im + f_im * bt_re


def _s5_discretize(a_re, a_im, log_dt, bt_re, bt_im):
    g, p = a_re.shape
    abar_re, abar_im, bbar_re, bbar_im = pl.pallas_call(
        _s5_discretize_kernel,
        out_shape=(jax.ShapeDtypeStruct((g, 1, p), F32), jax.ShapeDtypeStruct((g, 1, p), F32),
                   jax.ShapeDtypeStruct(bt_re.shape, F32), jax.ShapeDtypeStruct(bt_re.shape, F32)),
        name="s5_discretize",
    )(a_re.reshape(g, 1, p), a_im.reshape(g, 1, p), log_dt.reshape(g, 1, 1), bt_re, bt_im)
    return abar_re.reshape(g, p), abar_im.reshape(g, p), bbar_re, bbar_im


def _block_diag_slabs(m):
    n, gl, a, b = m.shape
    eye = jnp.eye(gl, dtype=m.dtype)
    full = m[:, :, :, None, :] * eye[None, :, None, :, None]
    return full.reshape(n, gl * a, gl * b)


def _s5_scan_kernel(u_next_ref, u_prev_ref, bbd_ref, ar_ref, ai_ref, cbd_ref, d_ref, y_ref,
                    buf_a, buf_b, carry_ref, *, ts, n_batch, tiles_per_slab, n_tiles):
    n = pl.program_id(0)
    last = n_tiles - 1
    cur = jnp.minimum(n, last)
    slab_next = jnp.minimum(n + 1, last) // tiles_per_slab
    slab_cur = cur // tiles_per_slab
    slab_prev = jnp.maximum(n - 1, 0) // tiles_per_slab
    half = ar_ref.shape[-1]

    @pl.when(n == 0)
    def _():
        buf_a[...] = jnp.dot(u_prev_ref[...].astype(BF16), bbd_ref[0], preferred_element_type=F32)
        buf_b[...] = jnp.zeros_like(buf_b)
        carry_ref[...] = jnp.zeros_like(carry_ref)

    def stages(cur_buf, other_buf):
        y = jnp.dot(other_buf[...].astype(BF16), cbd_ref[slab_prev], preferred_element_type=F32)
        y_ref[...] = _gelu(y + d_ref[slab_prev] * u_prev_ref[...])
        other_buf[...] = jnp.dot(u_next_ref[...].astype(BF16), bbd_ref[slab_next],
                                 preferred_element_type=F32)
        ar = ar_ref[slab_cur]
        ai = ai_ref[slab_cur]
        first = cur % tiles_per_slab == 0
        re = jnp.where(first, 0.0, carry_ref[:, :half])
        im = jnp.where(first, 0.0, carry_ref[:, half:])
        for t in range(ts):
            rows = slice(t * n_batch, (t + 1) * n_batch)
            bu = cur_buf[rows, :]
            re, im = ar * re - ai * im + bu[:, :half], ar * im + ai * re + bu[:, half:]
            cur_buf[rows, :] = jnp.concatenate([re, im], axis=-1)
        carry_ref[...] = jnp.concatenate([re, im], axis=-1)

    @pl.when(n % 2 == 0)
    def _():
        stages(buf_a, buf_b)

    @pl.when(n % 2 == 1)
    def _():
        stages(buf_b, buf_a)


def _s5_scan(u_tm, bbd, ar, ai, cbd, d, *, ts, n_batch):
    n_slabs, rows, _ = u_tm.shape
    tr = ts * n_batch
    tiles_per_slab = rows // tr
    n_tiles = n_slabs * tiles_per_slab
    n_state2 = bbd.shape[-1]
    tile = lambda shift: pl.BlockSpec(
        (tr, LANES), lambda n: (jnp.clip(n + shift, 0, n_tiles - 1), 0))
    u_flat = u_tm.reshape(n_slabs * rows, LANES)
    y = pl.pallas_call(
        functools.partial(_s5_scan_kernel, ts=ts, n_batch=n_batch,
                          tiles_per_slab=tiles_per_slab, n_tiles=n_tiles),
        grid=(n_tiles + 1,),
        in_specs=[tile(1), tile(-1), _const_spec(bbd.shape), _const_spec(ar.shape),
                  _const_spec(ai.shape), _const_spec(cbd.shape), _const_spec(d.shape)],
        out_specs=tile(-1),
        out_shape=jax.ShapeDtypeStruct(u_flat.shape, F32),
        scratch_shapes=[pltpu.VMEM((tr, n_state2), F32), pltpu.VMEM((tr, n_state2), F32),
                        pltpu.VMEM((n_batch, n_state2), F32)],
        compiler_params=_params("arbitrary"),
        name="s5_scan",
    )(u_flat, u_flat, bbd, ar, ai, cbd, d)
    return y.reshape(u_tm.shape)


def _glu_kernel(y_ref, w_ref, b_ref, o_ref, *, ts, n_batch):
    n_slabs = y_ref.shape[0]
    w = w_ref[...]
    bias = b_ref[...]
    for b in range(n_batch):
        yb = jnp.concatenate(
            [y_ref[s, pl.ds(b, ts, stride=n_batch), :] for s in range(n_slabs)], axis=-1)
        z = jnp.dot(yb.astype(BF16), w, preferred_element_type=F32) + bias
        o_ref[b] = (yb * jax.nn.sigmoid(z)).astype(BF16)


def _glu(y_tm, w_glu, b_glu, *, ts, n_batch):
    n_slabs, rows, _ = y_tm.shape
    seq = rows // n_batch
    width = n_slabs * LANES
    return pl.pallas_call(
        functools.partial(_glu_kernel, ts=ts, n_batch=n_batch),
        grid=(seq // ts,),
        in_specs=[
            pl.BlockSpec((n_slabs, ts * n_batch, LANES), lambda i: (0, i, 0)),
            _const_spec(w_glu.shape),
            _const_spec((1, width)),
        ],
        out_specs=pl.BlockSpec((n_batch, ts, width), lambda i: (0, i, 0)),
        out_shape=jax.ShapeDtypeStruct((n_batch, seq, width), BF16),
        compiler_params=_params("arbitrary"),
        name="glu",
    )(y_tm, w_glu, b_glu.reshape(1, width))


def _attn_kernel(qi_ref, kwq_ref, kwf_ref, q_ref, k_ref, vt_ref, o_ref,
                 sc_ref, lg_ref, bias_ref, q4_ref, mx_ref, m_ref, ls_ref, ot_ref,
                 *, tq, n_keep, idx_w_scale, qk_scale):
    i = pl.program_id(1)
    n_chunks = i + 1
    nt = (((1,), (1,)), ((), ()))
    fold = lambda a: a.reshape(tq // ACC_ROWS, ACC_ROWS, tq)

    key_pos = lax.broadcasted_iota(I32, (tq, tq), 0)
    qry_pos = lax.broadcasted_iota(I32, (tq, tq), 1)
    wi_t = kwq_ref[...].T[IDX_DIM:IDX_DIM + N_IDX_HEADS, :] * idx_w_scale

    def score_chunk(j):
        off = pl.multiple_of(j * tq, tq)
        kic = kwf_ref[pl.ds(off, tq), :][:, :IDX_DIM].astype(BF16)
        s = jnp.zeros((tq, tq), F32)
        for h in range(N_IDX_HEADS):
            r = lax.dot_general(kic, qi_ref[:, h * IDX_DIM:(h + 1) * IDX_DIM], nt,
                                preferred_element_type=F32)
            s = s + wi_t[h:h + 1, :] * jnp.maximum(r, 0.0)
        causal = (j - i) * tq + key_pos <= qry_pos
        sc_ref[j] = jnp.where(causal, s, -jnp.inf)

    _for_each_chunk(n_chunks, score_chunk)

    def decode(key):
        return pltpu.bitcast(jnp.where(key < 0, key ^ jnp.int32(0x7FFFFFFF), key), F32)

    def count_ge(thr, strict=False):
        thr_b = jnp.broadcast_to(thr, (ACC_ROWS, tq))

        def body(j, acc):
            s = fold(sc_ref[j])
            hit = s > thr_b if strict else s >= thr_b
            return acc + jnp.sum(jnp.where(hit, 1.0, 0.0), axis=0)

        acc = lax.fori_loop(0, n_chunks, body, jnp.zeros((ACC_ROWS, tq), F32))
        return jnp.sum(acc, axis=0, keepdims=True)

    keep = jnp.float32(n_keep)
    cnt0 = count_ge(jnp.zeros((1, tq), F32))
    tau = jnp.where(cnt0 >= keep, jnp.int32(0), jnp.int32(INT_MIN))
    cnt = jnp.where(cnt0 >= keep, cnt0, 0.0)

    def bit_body(bi, carry):
        tau, cnt = carry
        cand = tau + jnp.left_shift(jnp.int32(1), jnp.int32(30) - bi)
        c = count_ge(decode(cand))
        return jnp.where(c >= keep, cand, tau), jnp.where(c >= keep, c, cnt)

    tau, cnt = lax.fori_loop(0, 31, bit_body, (tau, cnt))
    thr = jnp.where(tau == jnp.int32(INT_MIN), jnp.finfo(F32).min, decode(tau))

    @pl.when(jnp.max(cnt) > keep)
    def _():
        need = keep - count_ge(thr, strict=True)
        lower = jnp.where(key_pos >= qry_pos, 1.0, 0.0).astype(BF16)

        def tie_chunk(j, run):
            s = sc_ref[j]
            eq = s == thr
            seen = run + jnp.dot(lower, jnp.where(eq, 1.0, 0.0).astype(BF16),
                                 preferred_element_type=F32)
            sc_ref[j] = jnp.where(eq, jnp.where(seen > need, -jnp.inf, s), s)
            return seen[tq - 1:tq, :]

        lax.fori_loop(0, n_chunks, tie_chunk, jnp.zeros((1, tq), F32))

    def mask_chunk(j, carry):
        sc_ref[j] = jnp.where(sc_ref[j] >= thr, 0.0, -1e30)
        return carry

    lax.fori_loop(0, n_chunks, mask_chunk, 0)

    slopes = [2.0 ** (-8.0 * (hd + 1) / N_HEADS) for hd in range(N_HEADS)]
    for hd in range(N_HEADS):
        c, g = divmod(hd, GQA_GROUP)
        q4_ref[c, g * tq:(g + 1) * tq, :] = q_ref[:, hd * HEAD_DIM:(hd + 1) * HEAD_DIM]
        bias_ref[hd] = slopes[hd] * key_pos.astype(F32)
    mx_ref[...] = jnp.full(mx_ref.shape, -jnp.inf, F32)
    ls_ref[...] = jnp.zeros_like(ls_ref)
    ot_ref[...] = jnp.zeros_like(ot_ref)

    def chunk_shift(j, hd):
        return ((j - i) * tq).astype(F32) * slopes[hd]

    def logits_chunk(j):
        off = pl.multiple_of(j * tq, tq)
        mask = sc_ref[j]
        for c in range(N_KV_HEADS):
            kc = k_ref[pl.ds(off, tq), :][:, c * HEAD_DIM:(c + 1) * HEAD_DIM]
            lg4 = lax.dot_general(kc, q4_ref[c], nt, preferred_element_type=F32) * qk_scale
            for g in range(GQA_GROUP):
                hd = c * GQA_GROUP + g
                lg = lg4[:, g * tq:(g + 1) * tq] + bias_ref[hd] + mask
                lg_ref[j, hd] = lg
                mx_ref[hd] = jnp.maximum(mx_ref[hd], jnp.max(fold(lg), axis=0) + chunk_shift(j, hd))

    _for_each_chunk(n_chunks, logits_chunk)
    for hd in range(N_HEADS):
        m_ref[hd] = jnp.max(mx_ref[hd], axis=0, keepdims=True)

    def pv_chunk(j):
        for c in range(N_KV_HEADS):
            vt = vt_ref[j, c * HEAD_DIM:(c + 1) * HEAD_DIM, :]
            for g in range(GQA_GROUP):
                hd = c * GQA_GROUP + g
                p = jnp.exp(lg_ref[j, hd] - (m_ref[hd] - chunk_shift(j, hd)))
                ls_ref[hd] += jnp.sum(fold(p), axis=0)
                ot_ref[hd] += jnp.dot(vt, p.astype(BF16), preferred_element_type=F32)

    _for_each_chunk(n_chunks, pv_chunk)
    for hd in range(N_HEADS):
        denom = jnp.sum(ls_ref[hd], axis=0, keepdims=True)
        o_ref[:, hd * HEAD_DIM:(hd + 1) * HEAD_DIM] = (ot_ref[hd] / denom).T.astype(BF16)


def _for_each_chunk(n_chunks, fn):
    def pair(p, carry):
        fn(2 * p)
        fn(2 * p + 1)
        return carry

    lax.fori_loop(0, n_chunks // 2, pair, 0)

    @pl.when(n_chunks % 2 == 1)
    def _():
        fn(n_chunks - 1)


def _attention(qi, kw, q, k, vt):
    n_batch, seq, _ = q.shape
    n_chunks, kv_w, tq = vt.shape[1:]
    n_keep = min(INDEX_TOPK, seq // 4)
    blk = lambda w: pl.BlockSpec((None, tq, w), lambda b, i: (b, i, 0))
    whole = lambda w: pl.BlockSpec((None, seq, w), lambda b, i: (b, 0, 0))
    return pl.pallas_call(
        functools.partial(_attn_kernel, tq=tq, n_keep=n_keep,
                          idx_w_scale=(N_IDX_HEADS ** -0.5) * (IDX_DIM ** -0.5),
                          qk_scale=HEAD_DIM ** -0.5),
        grid=(n_batch, n_chunks),
        in_specs=[blk(qi.shape[2]), blk(kw.shape[2]), whole(kw.shape[2]), blk(q.shape[2]),
                  whole(k.shape[2]),
                  pl.BlockSpec((None, n_chunks, kv_w, tq), lambda b, i: (b, 0, 0, 0))],
        out_specs=blk(q.shape[2]),
        out_shape=jax.ShapeDtypeStruct(q.shape, BF16),
        scratch_shapes=[pltpu.VMEM((n_chunks, tq, tq), F32),
                        pltpu.VMEM((n_chunks, N_HEADS, tq, tq), F32),
                        pltpu.VMEM((N_HEADS, tq, tq), F32),
                        pltpu.VMEM((N_KV_HEADS, GQA_GROUP * tq, HEAD_DIM), BF16),
                        pltpu.VMEM((N_HEADS, ACC_ROWS, tq), F32),
                        pltpu.VMEM((N_HEADS, 1, tq), F32),
                        pltpu.VMEM((N_HEADS, ACC_ROWS, tq), F32),
                        pltpu.VMEM((N_HEADS, HEAD_DIM, tq), F32)],
        compiler_params=_params("arbitrary", "arbitrary"),
        name="sparse_attn",
    )(qi, kw, kw, q, k, vt)


def _out_proj_kernel(ys_ref, ya_ref, x_ref, ws_ref, wa_ref, g_ref, b_ref, o_ref):
    for r in range(0, x_ref.shape[0], LN_ROWS):
        rows = slice(r, r + LN_ROWS)
        mix = jnp.dot(ys_ref[rows, :], ws_ref[...], preferred_element_type=F32)
        mix = mix + jnp.dot(ya_ref[rows, :], wa_ref[...], preferred_element_type=F32)
        o_ref[rows, :] = _layer_norm(DEEPNORM_ALPHA * x_ref[rows, :] + mix, g_ref[...], b_ref[...])


def _out_proj(ys, ya, x, ws, wa, g, b, *, tm):
    m, d = x.shape
    blk = lambda w: pl.BlockSpec((tm, w), lambda i: (i, 0))
    return pl.pallas_call(
        _out_proj_kernel,
        grid=(m // tm,),
        in_specs=[blk(ys.shape[1]), blk(ya.shape[1]), blk(d), _const_spec(ws.shape),
                  _const_spec(wa.shape), _const_spec((1, d)), _const_spec((1, d))],
        out_specs=blk(d),
        out_shape=jax.ShapeDtypeStruct((m, d), F32),
        compiler_params=_params("arbitrary"),
        name="out_proj_ln1",
    )(ys, ya, x, ws, wa, g.reshape(1, d), b.reshape(1, d))


def _ffn_kernel(h_ref, halo_ref, wup_ref, wgate_ref, wdown_ref, cw_ref, cb_ref, g_ref, b_ref,
                o_ref, hb_ref, halo_b_ref, acc_ref):
    i = pl.program_id(1)
    j = pl.program_id(2)

    @pl.when(j == 0)
    def _():
        hb_ref[...] = h_ref[...].astype(BF16)
        halo_b_ref[...] = jnp.where(i == 0, 0.0, halo_ref[...]).astype(BF16)
        acc_ref[...] = jnp.zeros_like(acc_ref)

    hb = hb_ref[...]
    tm = hb.shape[0]
    wup = wup_ref[...]
    up = jnp.dot(hb, wup, preferred_element_type=F32)
    up_halo = jnp.dot(halo_b_ref[...], wup, preferred_element_type=F32)
    ext = jnp.concatenate([up_halo, up], axis=0)
    cw = cw_ref[...]
    hc = cb_ref[...] + cw[2:3, :] * up
    for lag in range(1, CONV_WIDTH):
        start = SUBLANES - lag
        hc = hc + cw[CONV_WIDTH - 1 - lag:CONV_WIDTH - lag, :] * ext[start:start + tm, :]
    gate = jnp.dot(hb, wgate_ref[...], preferred_element_type=F32)
    act = (_gelu(hc) * gate).astype(BF16)
    acc_ref[...] += jnp.dot(act, wdown_ref[...], preferred_element_type=F32)

    @pl.when(j == pl.num_programs(2) - 1)
    def _():
        o_ref[...] = _layer_norm(DEEPNORM_ALPHA * h_ref[...] + acc_ref[...], g_ref[...], b_ref[...])


def _ffn(h, w_up, w_gate, w_down, conv_w, conv_b, g, b, *, tm, tf):
    n_batch, seq, d = h.shape
    f = w_up.shape[1]
    halo_blocks = tm // SUBLANES
    return pl.pallas_call(
        _ffn_kernel,
        grid=(n_batch, seq // tm, f // tf),
        in_specs=[
            pl.BlockSpec((None, tm, d), lambda bb, i, j: (bb, i, 0)),
            pl.BlockSpec((None, SUBLANES, d),
                         lambda bb, i, j: (bb, jnp.maximum(i * halo_blocks - 1, 0), 0)),
            pl.BlockSpec((d, tf), lambda bb, i, j: (0, j)),
            pl.BlockSpec((d, tf), lambda bb, i, j: (0, j)),
            pl.BlockSpec((tf, d), lambda bb, i, j: (j, 0)),
            pl.BlockSpec((CONV_WIDTH, tf), lambda bb, i, j: (0, j)),
            pl.BlockSpec((1, tf), lambda bb, i, j: (0, j)),
            _const_spec((1, d)),
            _const_spec((1, d)),
        ],
        out_specs=pl.BlockSpec((None, tm, d), lambda bb, i, j: (bb, i, 0)),
        out_shape=jax.ShapeDtypeStruct((n_batch, seq, d), F32),
        scratch_shapes=[pltpu.VMEM((tm, d), BF16), pltpu.VMEM((SUBLANES, d), BF16),
                        pltpu.VMEM((tm, d), F32)],
        compiler_params=_params("arbitrary", "arbitrary", "arbitrary"),
        name="ffn_ln2",
    )(h, h, w_up, w_gate, w_down, conv_w, conv_b.reshape(1, f), g.reshape(1, d), b.reshape(1, d))


def _layer(h, w_in, a_re, a_im, log_dt, b_re, b_im, c_re, c_im, ssm_d, w_glu, b_glu, w_out,
           ln1_g, ln1_b, w_up, w_gate, conv_w, conv_b, w_down, ln2_g, ln2_b):
    n_batch, seq, d = h.shape
    assert n_batch == SUBLANES, "the time-major scan layout puts the batch on the sublanes"
    n_groups, n_state = a_re.shape
    ssm_w = n_groups * SSM_GROUP_CH
    attn_w = N_HEADS * HEAD_DIM
    kv_w = N_KV_HEADS * HEAD_DIM
    qi_w = N_IDX_HEADS * IDX_DIM
    n_slabs = ssm_w // LANES
    assert n_state == SSM_STATE and ssm_w % LANES == 0
    t_seq, t_row, t_ff, t_scan = _tiles(seq, w_up.shape[1])

    o = 0
    parts = []
    for w in (ssm_w, attn_w, kv_w, kv_w, qi_w):
        parts.append(w_in[:, o:o + w].astype(BF16))
        o += w
    tail = w_in[:, o:]
    assert tail.shape[1] == IDX_DIM + N_IDX_HEADS
    wkw = jnp.pad(tail, ((0, 0), (0, LANES - tail.shape[1]))).astype(BF16)
    u_tm, q, k, vt, qi, kw = _in_proj(h, *parts, wkw, ts=t_seq)

    abar_re, abar_im, bbar_re, bbar_im = _s5_discretize(
        a_re, a_im, log_dt, jnp.swapaxes(b_re, 1, 2), jnp.swapaxes(b_im, 1, 2))
    gl = GROUPS_PER_SLAB
    to_slabs = lambda m: m.reshape(n_slabs, gl, *m.shape[1:])
    bbd = jnp.concatenate([_block_diag_slabs(to_slabs(bbar_re)),
                           _block_diag_slabs(to_slabs(bbar_im))], axis=-1).astype(BF16)
    cbd_re = _block_diag_slabs(to_slabs(jnp.swapaxes(c_re, 1, 2)))
    cbd_im = _block_diag_slabs(to_slabs(jnp.swapaxes(c_im, 1, 2)))
    cbd = jnp.concatenate([cbd_re, -cbd_im], axis=1).astype(BF16)
    bcast = lambda m: jnp.broadcast_to(m.reshape(n_slabs, 1, gl * n_state), (n_slabs, n_batch, gl * n_state))
    y_tm = _s5_scan(u_tm, bbd, bcast(abar_re), bcast(abar_im), cbd,
                    ssm_d.reshape(n_slabs, 1, LANES), ts=t_scan, n_batch=n_batch)
    y_ssm = _glu(y_tm, w_glu.astype(BF16), b_glu, ts=t_seq, n_batch=n_batch)

    y_attn = _attention(qi, kw, q, k, vt)

    w_out_b = w_out.astype(BF16)
    h1 = _out_proj(y_ssm.reshape(n_batch * seq, ssm_w), y_attn.reshape(n_batch * seq, attn_w),
                   h.reshape(n_batch * seq, d), w_out_b[:ssm_w], w_out_b[ssm_w:], ln1_g, ln1_b, tm=t_row)
    return _ffn(h1.reshape(n_batch, seq, d), w_up.astype(BF16), w_gate.astype(BF16),
                w_down.astype(BF16), conv_w, conv_b, ln2_g, ln2_b, tm=t_row, tf=t_ff)


def kernel(x, w_in, ssm_a_re, ssm_a_im, ssm_log_dt, ssm_b_re, ssm_b_im, ssm_c_re, ssm_c_im, ssm_d,
           w_glu, b_glu, w_out, ln1_g, ln1_b, w_up, w_gate, conv_w, conv_b, w_down, ln2_g, ln2_b):
    h = x
    for l in range(w_in.shape[0]):
        h = _layer(h, w_in[l], ssm_a_re[l], ssm_a_im[l], ssm_log_dt[l], ssm_b_re[l], ssm_b_im[l],
                   ssm_c_re[l], ssm_c_im[l], ssm_d[l], w_glu[l], b_glu[l], w_out[l], ln1_g[l],
                   ln1_b[l], w_up[l], w_gate[l], conv_w[l], conv_b[l], w_down[l], ln2_g[l], ln2_b[l])
    return h
```

```python
import functools
import math

import jax
import jax.numpy as jnp
from jax import lax
from jax.experimental import pallas as pl
from jax.experimental.pallas import tpu as pltpu

F32 = jnp.float32
BF16 = jnp.bfloat16
I32 = jnp.int32

LANES = 128
SUBLANES = 8
VMEM_LIMIT_BYTES = 56 * 1024 * 1024

SSM_GROUP_CH = 16
SSM_STATE = 64
N_HEADS = 8
HEAD_DIM = 128
N_KV_HEADS = 2
GQA_GROUP = N_HEADS // N_KV_HEADS
N_IDX_HEADS = 16
IDX_DIM = 64
INDEX_TOPK = 256
CONV_WIDTH = 3
LN_EPS = 1e-5
DEPTH = 1
DEEPNORM_ALPHA = (2.0 * DEPTH) ** 0.25

GROUPS_PER_SLAB = LANES // SSM_GROUP_CH
SLAB_STATE = GROUPS_PER_SLAB * SSM_STATE
INT_MIN = -(2 ** 31)
ACC_ROWS = 4 * SUBLANES
LN_ROWS = 128
HALO_ROWS = 2 * SUBLANES


def _tiles(seq, d_ff):
    t_seq = min(256, seq)
    t_row = min(512, seq)
    t_ff = 512 if d_ff % 512 == 0 else LANES
    t_scan = min(128, seq)
    assert seq % t_seq == 0 and seq % t_row == 0 and d_ff % t_ff == 0 and seq % t_scan == 0
    return t_seq, t_row, t_ff, t_scan


def _params(*sem):
    return pltpu.CompilerParams(dimension_semantics=sem, vmem_limit_bytes=VMEM_LIMIT_BYTES)


def _const_spec(shape):
    zeros = (0,) * len(shape)
    return pl.BlockSpec(shape, lambda *_: zeros, pipeline_mode=pl.Buffered(1))


def _layer_norm(v, g, b):
    mu = jnp.mean(v, axis=-1, keepdims=True)
    vc = v - mu
    var = jnp.mean(vc * vc, axis=-1, keepdims=True)
    return vc * lax.rsqrt(var + LN_EPS) * g + b


def _gelu(v):
    return 0.5 * v * (1.0 + lax.erf(v * (1.0 / math.sqrt(2.0))))


def _in_proj_kernel(x_ref, wu_ref, wq_ref, wk_ref, wv_ref, wqi_ref, wkw_ref,
                    u_ref, q_ref, k_ref, v_ref, qi_ref, kw_ref, *, n_batch):
    b = pl.program_id(1)
    xb = x_ref[...].astype(BF16)
    ts = xb.shape[0]
    u = jnp.dot(xb, wu_ref[...], preferred_element_type=F32)
    for s in range(u.shape[1] // LANES):
        u_ref[s, pl.ds(b, ts, stride=n_batch), :] = u[:, s * LANES:(s + 1) * LANES]
    q_ref[...] = jnp.dot(xb, wq_ref[...], preferred_element_type=F32).astype(BF16)
    k_ref[...] = jnp.dot(xb, wk_ref[...], preferred_element_type=F32).astype(BF16)
    v_ref[...] = jnp.dot(xb, wv_ref[...], preferred_element_type=F32).T.astype(BF16)
    qi_ref[...] = jnp.dot(xb, wqi_ref[...], preferred_element_type=F32).astype(BF16)
    kw_ref[...] = jnp.dot(xb, wkw_ref[...], preferred_element_type=F32)


def _in_proj(x, wu, wq, wk, wv, wqi, wkw, *, ts):
    n_batch, seq, d = x.shape
    n_slabs = wu.shape[1] // LANES
    grid = (seq // ts, n_batch)
    row = lambda w: pl.BlockSpec((None, ts, w), lambda i, b: (b, i, 0))
    out_shape = (
        jax.ShapeDtypeStruct((n_slabs, seq * n_batch, LANES), F32),
        jax.ShapeDtypeStruct((n_batch, seq, wq.shape[1]), BF16),
        jax.ShapeDtypeStruct((n_batch, seq, wk.shape[1]), BF16),
        jax.ShapeDtypeStruct((n_batch, seq // ts, wv.shape[1], ts), BF16),
        jax.ShapeDtypeStruct((n_batch, seq, wqi.shape[1]), BF16),
        jax.ShapeDtypeStruct((n_batch, seq, wkw.shape[1]), F32),
    )
    return pl.pallas_call(
        functools.partial(_in_proj_kernel, n_batch=n_batch),
        grid=grid,
        in_specs=[row(d)] + [_const_spec(w.shape) for w in (wu, wq, wk, wv, wqi, wkw)],
        out_specs=(
            pl.BlockSpec((n_slabs, ts * n_batch, LANES), lambda i, b: (0, i, 0)),
            row(wq.shape[1]), row(wk.shape[1]),
            pl.BlockSpec((None, None, wv.shape[1], ts), lambda i, b: (b, i, 0, 0)),
            row(wqi.shape[1]), row(wkw.shape[1]),
        ),
        out_shape=out_shape,
        compiler_params=_params("arbitrary", "arbitrary"),
        name="in_proj",
    )(x, wu, wq, wk, wv, wqi, wkw)


def _s5_discretize_kernel(a_re_ref, a_im_ref, log_dt_ref, bt_re_ref, bt_im_ref,
                          abar_re_ref, abar_im_ref, bbar_re_ref, bbar_im_ref):
    a_re = a_re_ref[...]
    a_im = a_im_ref[...]
    dt = jnp.exp(log_dt_ref[...])
    mag = jnp.exp(dt * a_re)
    ang = dt * a_im
    abar_re = mag * jnp.cos(ang)
    abar_im = mag * jnp.sin(ang)
    num_re = abar_re - 1.0
    num_im = abar_im
    den = a_re * a_re + a_im * a_im
    f_re = (num_re * a_re + num_im * a_im) / den
    f_im = (num_im * a_re - num_re * a_im) / den
    abar_re_ref[...] = abar_re
    abar_im_ref[...] = abar_im
    bt_re = bt_re_ref[...]
    bt_im = bt_im_ref[...]
    bbar_re_ref[...] = f_re * bt_re - f_im * bt_im
    bbar_im_ref[...] = f_re * bt_im + f_im * bt_re


def _s5_discretize(a_re, a_im, log_dt, bt_re, bt_im):
    g, p = a_re.shape
    abar_re, abar_im, bbar_re, bbar_im = pl.pallas_call(
        _s5_discretize_kernel,
        out_shape=(jax.ShapeDtypeStruct((g, 1, p), F32), jax.ShapeDtypeStruct((g, 1, p), F32),
                   jax.ShapeDtypeStruct(bt_re.shape, F32), jax.ShapeDtypeStruct(bt_re.shape, F32)),
        name="s5_discretize",
    )(a_re.reshape(g, 1, p), a_im.reshape(g, 1, p), log_dt.reshape(g, 1, 1), bt_re, bt_im)
    return abar_re.reshape(g, p), abar_im.reshape(g, p), bbar_re, bbar_im


def _block_diag_slabs(m):
    n, gl, a, b = m.shape
    eye = jnp.eye(gl, dtype=m.dtype)
    full = m[:, :, :, None, :] * eye[None, :, None, :, None]
    return full.reshape(n, gl * a, gl * b)


def _s5_scan_kernel(u_next_ref, u_prev_ref, bbd_ref, ar_ref, ai_ref, cbd_ref, d_ref, y_ref,
                    buf_a, buf_b, carry_ref, *, ts, n_batch, tiles_per_slab, n_tiles):
    n = pl.program_id(0)
    last = n_tiles - 1
    cur = jnp.minimum(n, last)
    slab_next = jnp.minimum(n + 1, last) // tiles_per_slab
    slab_cur = cur // tiles_per_slab
    slab_prev = jnp.maximum(n - 1, 0) // tiles_per_slab
    half = ar_ref.shape[-1]

    @pl.when(n == 0)
    def _():
        buf_a[...] = jnp.dot(u_prev_ref[...].astype(BF16), bbd_ref[0], preferred_element_type=F32)
        buf_b[...] = jnp.zeros_like(buf_b)
        carry_ref[...] = jnp.zeros_like(carry_ref)

    def stages(cur_buf, other_buf):
        y = jnp.dot(other_buf[...].astype(BF16), cbd_ref[slab_prev], preferred_element_type=F32)
        y_ref[...] = _gelu(y + d_ref[slab_prev] * u_prev_ref[...])
        other_buf[...] = jnp.dot(u_next_ref[...].astype(BF16), bbd_ref[slab_next],
                                 preferred_element_type=F32)
        ar = ar_ref[slab_cur]
        ai = ai_ref[slab_cur]
        first = cur % tiles_per_slab == 0
        re = jnp.where(first, 0.0, carry_ref[:, :half])
        im = jnp.where(first, 0.0, carry_ref[:, half:])
        for t in range(ts):
            rows = slice(t * n_batch, (t + 1) * n_batch)
            bu = cur_buf[rows, :]
            re, im = ar * re - ai * im + bu[:, :half], ar * im + ai * re + bu[:, half:]
            cur_buf[rows, :] = jnp.concatenate([re, im], axis=-1)
        carry_ref[...] = jnp.concatenate([re, im], axis=-1)

    @pl.when(n % 2 == 0)
    def _():
        stages(buf_a, buf_b)

    @pl.when(n % 2 == 1)
    def _():
        stages(buf_b, buf_a)


def _s5_scan(u_tm, bbd, ar, ai, cbd, d, *, ts, n_batch):
    n_slabs, rows, _ = u_tm.shape
    tr = ts * n_batch
    tiles_per_slab = rows // tr
    n_tiles = n_slabs * tiles_per_slab
    n_state2 = bbd.shape[-1]
    tile = lambda shift: pl.BlockSpec(
        (tr, LANES), lambda n: (jnp.clip(n + shift, 0, n_tiles - 1), 0))
    u_flat = u_tm.reshape(n_slabs * rows, LANES)
    y = pl.pallas_call(
        functools.partial(_s5_scan_kernel, ts=ts, n_batch=n_batch,
                          tiles_per_slab=tiles_per_slab, n_tiles=n_tiles),
        grid=(n_tiles + 1,),
        in_specs=[tile(1), tile(-1), _const_spec(bbd.shape), _const_spec(ar.shape),
                  _const_spec(ai.shape), _const_spec(cbd.shape), _const_spec(d.shape)],
        out_specs=tile(-1),
        out_shape=jax.ShapeDtypeStruct(u_flat.shape, F32),
        scratch_shapes=[pltpu.VMEM((tr, n_state2), F32), pltpu.VMEM((tr, n_state2), F32),
                        pltpu.VMEM((n_batch, n_state2), F32)],
        compiler_params=_params("arbitrary"),
        name="s5_scan",
    )(u_flat, u_flat, bbd, ar, ai, cbd, d)
    return y.reshape(u_tm.shape)


def _glu_kernel(y_ref, w_ref, b_ref, o_ref, *, ts, n_batch):
    n_slabs = y_ref.shape[0]
    w = w_ref[...]
    bias = b_ref[...]
    for b in range(n_batch):
        yb = jnp.concatenate(
            [y_ref[s, pl.ds(b, ts, stride=n_batch), :] for s in range(n_slabs)], axis=-1)
        z = jnp.dot(yb.astype(BF16), w, preferred_element_type=F32) + bias
        o_ref[b] = (yb * jax.nn.sigmoid(z)).astype(BF16)


def _glu(y_tm, w_glu, b_glu, *, ts, n_batch):
    n_slabs, rows, _ = y_tm.shape
    seq = rows // n_batch
    width = n_slabs * LANES
    return pl.pallas_call(
        functools.partial(_glu_kernel, ts=ts, n_batch=n_batch),
        grid=(seq // ts,),
        in_specs=[
            pl.BlockSpec((n_slabs, ts * n_batch, LANES), lambda i: (0, i, 0)),
            _const_spec(w_glu.shape),
            _const_spec((1, width)),
        ],
        out_specs=pl.BlockSpec((n_batch, ts, width), lambda i: (0, i, 0)),
        out_shape=jax.ShapeDtypeStruct((n_batch, seq, width), BF16),
        compiler_params=_params("arbitrary"),
        name="glu",
    )(y_tm, w_glu, b_glu.reshape(1, width))


def _attn_kernel(qi_ref, kwq_ref, kwf_ref, q_ref, k_ref, vt_ref, o_ref,
                 sc_ref, lg_ref, bias_ref, q4_ref, mx_ref, m_ref, ls_ref, ot_ref,
                 *, tq, n_keep, idx_w_scale, qk_scale):
    i = pl.program_id(1)
    n_chunks = i + 1
    nt = (((1,), (1,)), ((), ()))
    fold = lambda a: a.reshape(tq // ACC_ROWS, ACC_ROWS, tq)

    key_pos = lax.broadcasted_iota(I32, (tq, tq), 0)
    qry_pos = lax.broadcasted_iota(I32, (tq, tq), 1)
    wi_t = kwq_ref[...].T[IDX_DIM:IDX_DIM + N_IDX_HEADS, :] * idx_w_scale

    def score_chunk(j):
        off = pl.multiple_of(j * tq, tq)
        kic = kwf_ref[pl.ds(off, tq), :][:, :IDX_DIM].astype(BF16)
        s = jnp.zeros((tq, tq), F32)
        for h in range(N_IDX_HEADS):
            r = lax.dot_general(kic, qi_ref[:, h * IDX_DIM:(h + 1) * IDX_DIM], nt,
                                preferred_element_type=F32)
            s = s + wi_t[h:h + 1, :] * jnp.maximum(r, 0.0)
        causal = (j - i) * tq + key_pos <= qry_pos
        sc_ref[j] = jnp.where(causal, s, -jnp.inf)

    _for_each_chunk(n_chunks, score_chunk)

    def decode(key):
        return pltpu.bitcast(jnp.where(key < 0, key ^ jnp.int32(0x7FFFFFFF), key), F32)

    def count_ge(thr, strict=False):
        thr_b = jnp.broadcast_to(thr, (ACC_ROWS, tq))

        def body(j, acc):
            s = fold(sc_ref[j])
            hit = s > thr_b if strict else s >= thr_b
            return acc + jnp.sum(jnp.where(hit, 1.0, 0.0), axis=0)

        acc = lax.fori_loop(0, n_chunks, body, jnp.zeros((ACC_ROWS, tq), F32))
        return jnp.sum(acc, axis=0, keepdims=True)

    keep = jnp.float32(n_keep)
    cnt0 = count_ge(jnp.zeros((1, tq), F32))
    tau = jnp.where(cnt0 >= keep, jnp.int32(0), jnp.int32(INT_MIN))
    cnt = jnp.where(cnt0 >= keep, cnt0, 0.0)

    def bit_body(bi, carry):
        tau, cnt = carry
        cand = tau + jnp.left_shift(jnp.int32(1), jnp.int32(30) - bi)
        c = count_ge(decode(cand))
        return jnp.where(c >= keep, cand, tau), jnp.where(c >= keep, c, cnt)

    tau, cnt = lax.fori_loop(0, 31, bit_body, (tau, cnt))
    thr = jnp.where(tau == jnp.int32(INT_MIN), jnp.finfo(F32).min, decode(tau))

    @pl.when(jnp.max(cnt) > keep)
    def _():
        need = keep - count_ge(thr, strict=True)
        lower = jnp.where(key_pos >= qry_pos, 1.0, 0.0).astype(BF16)

        def tie_chunk(j, run):
            s = sc_ref[j]
            eq = s == thr
            seen = run + jnp.dot(lower, jnp.where(eq, 1.0, 0.0).astype(BF16),
                                 preferred_element_type=F32)
            sc_ref[j] = jnp.where(eq, jnp.where(seen > need, -jnp.inf, s), s)
            return seen[tq - 1:tq, :]

        lax.fori_loop(0, n_chunks, tie_chunk, jnp.zeros((1, tq), F32))

    def mask_chunk(j, carry):
        sc_ref[j] = jnp.where(sc_ref[j] >= thr, 0.0, -1e30)
        return carry

    lax.fori_loop(0, n_chunks, mask_chunk, 0)

    slopes = [2.0 ** (-8.0 * (hd + 1) / N_HEADS) for hd in range(N_HEADS)]
    for hd in range(N_HEADS):
        c, g = divmod(hd, GQA_GROUP)
        q4_ref[c, g * tq:(g + 1) * tq, :] = q_ref[:, hd * HEAD_DIM:(hd + 1) * HEAD_DIM]
        bias_ref[hd] = slopes[hd] * key_pos.astype(F32)
    mx_ref[...] = jnp.full(mx_ref.shape, -jnp.inf, F32)
    ls_ref[...] = jnp.zeros_like(ls_ref)
    ot_ref[...] = jnp.zeros_like(ot_ref)

    def chunk_shift(j, hd):
        return ((j - i) * tq).astype(F32) * slopes[hd]

    def logits_chunk(j):
        off = pl.multiple_of(j * tq, tq)
        mask = sc_ref[j]
        for c in range(N_KV_HEADS):
            kc = k_ref[pl.ds(off, tq), :][:, c * HEAD_DIM:(c + 1) * HEAD_DIM]
            lg4 = lax.dot_general(kc, q4_ref[c], nt, preferred_element_type=F32) * qk_scale
            for g in range(GQA_GROUP):
                hd = c * GQA_GROUP + g
                lg = lg4[:, g * tq:(g + 1) * tq] + bias_ref[hd] + mask
                lg_ref[j, hd] = lg
                mx_ref[hd] = jnp.maximum(mx_ref[hd], jnp.max(fold(lg), axis=0) + chunk_shift(j, hd))

    _for_each_chunk(n_chunks, logits_chunk)
    for hd in range(N_HEADS):
        m_ref[hd] = jnp.max(mx_ref[hd], axis=0, keepdims=True)

    def pv_chunk(j):
        for c in range(N_KV_HEADS):
            vt = vt_ref[j, c * HEAD_DIM:(c + 1) * HEAD_DIM, :]
            for g in range(GQA_GROUP):
                hd = c * GQA_GROUP + g
                p = jnp.exp(lg_ref[j, hd] - (m_ref[hd] - chunk_shift(j, hd)))
                ls_ref[hd] += jnp.sum(fold(p), axis=0)
                ot_ref[hd] += jnp.dot(vt, p.astype(BF16), preferred_element_type=F32)

    _for_each_chunk(n_chunks, pv_chunk)
    for hd in range(N_HEADS):
        denom = jnp.sum(ls_ref[hd], axis=0, keepdims=True)
        o_ref[:, hd * HEAD_DIM:(hd + 1) * HEAD_DIM] = (ot_ref[hd] / denom).T.astype(BF16)


def _for_each_chunk(n_chunks, fn):
    def pair(p, carry):
        fn(2 * p)
        fn(2 * p + 1)
        return carry

    lax.fori_loop(0, n_chunks // 2, pair, 0)

    @pl.when(n_chunks % 2 == 1)
    def _():
        fn(n_chunks - 1)


def _attention(qi, kw, q, k, vt):
    n_batch, seq, _ = q.shape
    n_chunks, kv_w, tq = vt.shape[1:]
    n_keep = min(INDEX_TOPK, seq // 4)
    blk = lambda w: pl.BlockSpec((None, tq, w), lambda b, i: (b, i, 0))
    whole = lambda w: pl.BlockSpec((None, seq, w), lambda b, i: (b, 0, 0))
    return pl.pallas_call(
        functools.partial(_attn_kernel, tq=tq, n_keep=n_keep,
                          idx_w_scale=(N_IDX_HEADS ** -0.5) * (IDX_DIM ** -0.5),
                          qk_scale=HEAD_DIM ** -0.5),
        grid=(n_batch, n_chunks),
        in_specs=[blk(qi.shape[2]), blk(kw.shape[2]), whole(kw.shape[2]), blk(q.shape[2]),
                  whole(k.shape[2]),
                  pl.BlockSpec((None, n_chunks, kv_w, tq), lambda b, i: (b, 0, 0, 0))],
        out_specs=blk(q.shape[2]),
        out_shape=jax.ShapeDtypeStruct(q.shape, BF16),
        scratch_shapes=[pltpu.VMEM((n_chunks, tq, tq), F32),
                        pltpu.VMEM((n_chunks, N_HEADS, tq, tq), F32),
                        pltpu.VMEM((N_HEADS, tq, tq), F32),
                        pltpu.VMEM((N_KV_HEADS, GQA_GROUP * tq, HEAD_DIM), BF16),
                        pltpu.VMEM((N_HEADS, ACC_ROWS, tq), F32),
                        pltpu.VMEM((N_HEADS, 1, tq), F32),
                        pltpu.VMEM((N_HEADS, ACC_ROWS, tq), F32),
                        pltpu.VMEM((N_HEADS, HEAD_DIM, tq), F32)],
        compiler_params=_params("arbitrary", "arbitrary"),
        name="sparse_attn",
    )(qi, kw, kw, q, k, vt)


def _out_proj_kernel(ys_ref, ya_ref, x_ref, ws_ref, wa_ref, g_ref, b_ref, o_ref):
    for r in range(0, x_ref.shape[0], LN_ROWS):
        rows = slice(r, r + LN_ROWS)
        mix = jnp.dot(ys_ref[rows, :], ws_ref[...], preferred_element_type=F32)
        mix = mix + jnp.dot(ya_ref[rows, :], wa_ref[...], preferred_element_type=F32)
        o_ref[rows, :] = _layer_norm(DEEPNORM_ALPHA * x_ref[rows, :] + mix, g_ref[...], b_ref[...])


def _out_proj(ys, ya, x, ws, wa, g, b, *, tm):
    m, d = x.shape
    blk = lambda w: pl.BlockSpec((tm, w), lambda i: (i, 0))
    return pl.pallas_call(
        _out_proj_kernel,
        grid=(m // tm,),
        in_specs=[blk(ys.shape[1]), blk(ya.shape[1]), blk(d), _const_spec(ws.shape),
                  _const_spec(wa.shape), _const_spec((1, d)), _const_spec((1, d))],
        out_specs=blk(d),
        out_shape=jax.ShapeDtypeStruct((m, d), F32),
        compiler_params=_params("arbitrary"),
        name="out_proj_ln1",
    )(ys, ya, x, ws, wa, g.reshape(1, d), b.reshape(1, d))


def _ffn_kernel(h_ref, halo_ref, wup_ref, wgate_ref, wdown_ref, cw_ref, cb_ref, g_ref, b_ref,
                o_ref, hb_ref, acc_ref):
    i = pl.program_id(1)
    j = pl.program_id(2)
    tm = h_ref.shape[0]

    @pl.when(j == 0)
    def _():
        hb_ref[:HALO_ROWS, :] = jnp.where(i == 0, 0.0, halo_ref[...]).astype(BF16)
        hb_ref[HALO_ROWS:, :] = h_ref[...].astype(BF16)
        acc_ref[...] = jnp.zeros_like(acc_ref)

    ext = jnp.dot(hb_ref[...], wup_ref[...], preferred_element_type=F32)
    cw = cw_ref[...]
    hc = cb_ref[...]
    for lag in range(CONV_WIDTH):
        start = HALO_ROWS - lag
        hc = hc + cw[CONV_WIDTH - 1 - lag:CONV_WIDTH - lag, :] * ext[start:start + tm, :]
    gate = jnp.dot(hb_ref[HALO_ROWS:, :], wgate_ref[...], preferred_element_type=F32)
    act = (_gelu(hc) * gate).astype(BF16)
    acc_ref[...] += jnp.dot(act, wdown_ref[...], preferred_element_type=F32)

    @pl.when(j == pl.num_programs(2) - 1)
    def _():
        o_ref[...] = _layer_norm(DEEPNORM_ALPHA * h_ref[...] + acc_ref[...], g_ref[...], b_ref[...])


def _ffn(h, w_up, w_gate, w_down, conv_w, conv_b, g, b, *, tm, tf):
    n_batch, seq, d = h.shape
    f = w_up.shape[1]
    halo_blocks = tm // HALO_ROWS
    return pl.pallas_call(
        _ffn_kernel,
        grid=(n_batch, seq // tm, f // tf),
        in_specs=[
            pl.BlockSpec((None, tm, d), lambda bb, i, j: (bb, i, 0)),
            pl.BlockSpec((None, HALO_ROWS, d),
                         lambda bb, i, j: (bb, jnp.maximum(i * halo_blocks - 1, 0), 0)),
            pl.BlockSpec((d, tf), lambda bb, i, j: (0, j)),
            pl.BlockSpec((d, tf), lambda bb, i, j: (0, j)),
            pl.BlockSpec((tf, d), lambda bb, i, j: (j, 0)),
            pl.BlockSpec((CONV_WIDTH, tf), lambda bb, i, j: (0, j)),
            pl.BlockSpec((1, tf), lambda bb, i, j: (0, j)),
            _const_spec((1, d)),
            _const_spec((1, d)),
        ],
        out_specs=pl.BlockSpec((None, tm, d), lambda bb, i, j: (bb, i, 0)),
        out_shape=jax.ShapeDtypeStruct((n_batch, seq, d), F32),
        scratch_shapes=[pltpu.VMEM((HALO_ROWS + tm, d), BF16), pltpu.VMEM((tm, d), F32)],
        compiler_params=_params("arbitrary", "arbitrary", "arbitrary"),
        name="ffn_ln2",
    )(h, h, w_up, w_gate, w_down, conv_w, conv_b.reshape(1, f), g.reshape(1, d), b.reshape(1, d))


def _layer(h, w_in, a_re, a_im, log_dt, b_re, b_im, c_re, c_im, ssm_d, w_glu, b_glu, w_out,
           ln1_g, ln1_b, w_up, w_gate, conv_w, conv_b, w_down, ln2_g, ln2_b):
    n_batch, seq, d = h.shape
    assert n_batch == SUBLANES, "the time-major scan layout puts the batch on the sublanes"
    n_groups, n_state = a_re.shape
    ssm_w = n_groups * SSM_GROUP_CH
    attn_w = N_HEADS * HEAD_DIM
    kv_w = N_KV_HEADS * HEAD_DIM
    qi_w = N_IDX_HEADS * IDX_DIM
    n_slabs = ssm_w // LANES
    assert n_state == SSM_STATE and ssm_w % LANES == 0
    t_seq, t_row, t_ff, t_scan = _tiles(seq, w_up.shape[1])

    o = 0
    parts = []
    for w in (ssm_w, attn_w, kv_w, kv_w, qi_w):
        parts.append(w_in[:, o:o + w].astype(BF16))
        o += w
    tail = w_in[:, o:]
    assert tail.shape[1] == IDX_DIM + N_IDX_HEADS
    wkw = jnp.pad(tail, ((0, 0), (0, LANES - tail.shape[1]))).astype(BF16)
    u_tm, q, k, vt, qi, kw = _in_proj(h, *parts, wkw, ts=t_seq)

    abar_re, abar_im, bbar_re, bbar_im = _s5_discretize(
        a_re, a_im, log_dt, jnp.swapaxes(b_re, 1, 2), jnp.swapaxes(b_im, 1, 2))
    gl = GROUPS_PER_SLAB
    to_slabs = lambda m: m.reshape(n_slabs, gl, *m.shape[1:])
    bbd = jnp.concatenate([_block_diag_slabs(to_slabs(bbar_re)),
                           _block_diag_slabs(to_slabs(bbar_im))], axis=-1).astype(BF16)
    cbd_re = _block_diag_slabs(to_slabs(jnp.swapaxes(c_re, 1, 2)))
    cbd_im = _block_diag_slabs(to_slabs(jnp.swapaxes(c_im, 1, 2)))
    cbd = jnp.concatenate([cbd_re, -cbd_im], axis=1).astype(BF16)
    bcast = lambda m: jnp.broadcast_to(m.reshape(n_slabs, 1, gl * n_state), (n_slabs, n_batch, gl * n_state))
    y_tm = _s5_scan(u_tm, bbd, bcast(abar_re), bcast(abar_im), cbd,
                    ssm_d.reshape(n_slabs, 1, LANES), ts=t_scan, n_batch=n_batch)
    y_ssm = _glu(y_tm, w_glu.astype(BF16), b_glu, ts=t_seq, n_batch=n_batch)

    y_attn = _attention(qi, kw, q, k, vt)

    w_out_b = w_out.astype(BF16)
    h1 = _out_proj(y_ssm.reshape(n_batch * seq, ssm_w), y_attn.reshape(n_batch * seq, attn_w),
                   h.reshape(n_batch * seq, d), w_out_b[:ssm_w], w_out_b[ssm_w:], ln1_g, ln1_b, tm=t_row)
    return _ffn(h1.reshape(n_batch, seq, d), w_up.astype(BF16), w_gate.astype(BF16),
                w_down.astype(BF16), conv_w, conv_b, ln2_g, ln2_b, tm=t_row, tf=t_ff)


def kernel(x, w_in, ssm_a_re, ssm_a_im, ssm_log_dt, ssm_b_re, ssm_b_im, ssm_c_re, ssm_c_im, ssm_d,
           w_glu, b_glu, w_out, ln1_g, ln1_b, w_up, w_gate, conv_w, conv_b, w_down, ln2_g, ln2_b):
    h = x
    for l in range(w_in.shape[0]):
        h = _layer(h, w_in[l], ssm_a_re[l], ssm_a_im[l], ssm_log_dt[l], ssm_b_re[l], ssm_b_im[l],
                   ssm_c_re[l], ssm_c_im[l], ssm_d[l], w_glu[l], b_glu[l], w_out[l], ln1_g[l],
                   ln1_b[l], w_up[l], w_gate[l], conv_w[l], conv_b[l], w_down[l], ln2_g[l], ln2_b[l])
    return h
```

```python
import functools
import math

import jax
import jax.numpy as jnp
from jax import lax
from jax.experimental import pallas as pl
from jax.experimental.pallas import tpu as pltpu

F32 = jnp.float32
BF16 = jnp.bfloat16
I32 = jnp.int32

LANES = 128
SUBLANES = 8
VMEM_LIMIT_BYTES = 56 * 1024 * 1024

SSM_GROUP_CH = 16
SSM_STATE = 64
N_HEADS = 8
HEAD_DIM = 128
N_KV_HEADS = 2
GQA_GROUP = N_HEADS // N_KV_HEADS
N_IDX_HEADS = 16
IDX_DIM = 64
INDEX_TOPK = 256
CONV_WIDTH = 3
LN_EPS = 1e-5
DEPTH = 1
DEEPNORM_ALPHA = (2.0 * DEPTH) ** 0.25

GROUPS_PER_SLAB = LANES // SSM_GROUP_CH
SLAB_STATE = GROUPS_PER_SLAB * SSM_STATE
INT_MIN = -(2 ** 31)
ACC_ROWS = 4 * SUBLANES
LN_ROWS = 128
HALO_ROWS = 2 * SUBLANES


def _tiles(seq, d_ff):
    t_seq = min(256, seq)
    t_row = min(512, seq)
    t_ff = 512 if d_ff % 512 == 0 else LANES
    t_scan = min(128, seq)
    assert seq % t_seq == 0 and seq % t_row == 0 and d_ff % t_ff == 0 and seq % t_scan == 0
    return t_seq, t_row, t_ff, t_scan


def _params(*sem):
    return pltpu.CompilerParams(dimension_semantics=sem, vmem_limit_bytes=VMEM_LIMIT_BYTES)


def _const_spec(shape):
    zeros = (0,) * len(shape)
    return pl.BlockSpec(shape, lambda *_: zeros, pipeline_mode=pl.Buffered(1))


def _layer_norm(v, g, b):
    mu = jnp.mean(v, axis=-1, keepdims=True)
    vc = v - mu
    var = jnp.mean(vc * vc, axis=-1, keepdims=True)
    return vc * lax.rsqrt(var + LN_EPS) * g + b


def _gelu(v):
    return 0.5 * v * (1.0 + lax.erf(v * (1.0 / math.sqrt(2.0))))


def _in_proj_kernel(x_ref, w_ref, u_ref, q_ref, k_ref, v_ref, qi_ref, kw_ref, *, n_batch, widths):
    b = pl.program_id(1)
    xb = x_ref[...].astype(BF16)
    ts = xb.shape[0]
    starts = [sum(widths[:n]) for n in range(len(widths))]
    proj = lambda n: jnp.dot(xb, w_ref[:, starts[n]:starts[n] + widths[n]],
                             preferred_element_type=F32)
    u = proj(0)
    for s in range(u.shape[1] // LANES):
        u_ref[s, pl.ds(b, ts, stride=n_batch), :] = u[:, s * LANES:(s + 1) * LANES]
    q_ref[...] = proj(1).astype(BF16)
    k_ref[...] = proj(2).astype(BF16)
    v_ref[...] = proj(3).T.astype(BF16)
    qi_ref[...] = proj(4).astype(BF16)
    kw_ref[...] = proj(5)


def _in_proj(x, w, widths, *, ts):
    n_batch, seq, d = x.shape
    w_u, w_q, w_k, w_v, w_qi, w_kw = widths
    n_slabs = w_u // LANES
    row = lambda wd: pl.BlockSpec((None, ts, wd), lambda i, b: (b, i, 0))
    out_shape = (
        jax.ShapeDtypeStruct((n_slabs, seq * n_batch, LANES), F32),
        jax.ShapeDtypeStruct((n_batch, seq, w_q), BF16),
        jax.ShapeDtypeStruct((n_batch, seq, w_k), BF16),
        jax.ShapeDtypeStruct((n_batch, seq // ts, w_v, ts), BF16),
        jax.ShapeDtypeStruct((n_batch, seq, w_qi), BF16),
        jax.ShapeDtypeStruct((n_batch, seq, w_kw), F32),
    )
    return pl.pallas_call(
        functools.partial(_in_proj_kernel, n_batch=n_batch, widths=widths),
        grid=(seq // ts, n_batch),
        in_specs=[row(d), _const_spec(w.shape)],
        out_specs=(
            pl.BlockSpec((n_slabs, ts * n_batch, LANES), lambda i, b: (0, i, 0)),
            row(w_q), row(w_k),
            pl.BlockSpec((None, None, w_v, ts), lambda i, b: (b, i, 0, 0)),
            row(w_qi), row(w_kw),
        ),
        out_shape=out_shape,
        compiler_params=_params("arbitrary", "arbitrary"),
        name="in_proj",
    )(x, w)


def _s5_discretize_kernel(a_re_ref, a_im_ref, log_dt_ref, bt_re_ref, bt_im_ref,
                          abar_re_ref, abar_im_ref, bbar_re_ref, bbar_im_ref):
    a_re = a_re_ref[...]
    a_im = a_im_ref[...]
    dt = jnp.exp(log_dt_ref[...])
    mag = jnp.exp(dt * a_re)
    ang = dt * a_im
    abar_re = mag * jnp.cos(ang)
    abar_im = mag * jnp.sin(ang)
    num_re = abar_re - 1.0
    num_im = abar_im
    den = a_re * a_re + a_im * a_im
    f_re = (num_re * a_re + num_im * a_im) / den
    f_im = (num_im * a_re - num_re * a_im) / den
    abar_re_ref[...] = abar_re
    abar_im_ref[...] = abar_im
    bt_re = bt_re_ref[...]
    bt_im = bt_im_ref[...]
    bbar_re_ref[...] = f_re * bt_re - f_im * bt_im
    bbar_im_ref[...] = f_re * bt_im + f_im * bt_re


def _s5_discretize(a_re, a_im, log_dt, bt_re, bt_im):
    g, p = a_re.shape
    abar_re, abar_im, bbar_re, bbar_im = pl.pallas_call(
        _s5_discretize_kernel,
        out_shape=(jax.ShapeDtypeStruct((g, 1, p), F32), jax.ShapeDtypeStruct((g, 1, p), F32),
                   jax.ShapeDtypeStruct(bt_re.shape, F32), jax.ShapeDtypeStruct(bt_re.shape, F32)),
        name="s5_discretize",
    )(a_re.reshape(g, 1, p), a_im.reshape(g, 1, p), log_dt.reshape(g, 1, 1), bt_re, bt_im)
    return abar_re.reshape(g, p), abar_im.reshape(g, p), bbar_re, bbar_im


def _block_diag_slabs(m):
    n, gl, a, b = m.shape
    eye = jnp.eye(gl, dtype=m.dtype)
    full = m[:, :, :, None, :] * eye[None, :, None, :, None]
    return full.reshape(n, gl * a, gl * b)


def _s5_scan_kernel(u_next_ref, u_prev_ref, bbd_ref, ar_ref, ai_ref, cbd_ref, d_ref, y_ref,
                    buf_a, buf_b, carry_ref, *, ts, n_batch, tiles_per_slab, n_tiles):
    n = pl.program_id(0)
    last = n_tiles - 1
    cur = jnp.minimum(n, last)
    slab_next = jnp.minimum(n + 1, last) // tiles_per_slab
    slab_cur = cur // tiles_per_slab
    slab_prev = jnp.maximum(n - 1, 0) // tiles_per_slab
    half = ar_ref.shape[-1]

    @pl.when(n == 0)
    def _():
        buf_a[...] = jnp.dot(u_prev_ref[...].astype(BF16), bbd_ref[0], preferred_element_type=F32)
        buf_b[...] = jnp.zeros_like(buf_b)
        carry_ref[...] = jnp.zeros_like(carry_ref)

    def stages(cur_buf, other_buf):
        y = jnp.dot(other_buf[...].astype(BF16), cbd_ref[slab_prev], preferred_element_type=F32)
        y_ref[...] = _gelu(y + d_ref[slab_prev] * u_prev_ref[...])
        other_buf[...] = jnp.dot(u_next_ref[...].astype(BF16), bbd_ref[slab_next],
                                 preferred_element_type=F32)
        ar = ar_ref[slab_cur]
        ai = ai_ref[slab_cur]
        first = cur % tiles_per_slab == 0
        re = jnp.where(first, 0.0, carry_ref[:, :half])
        im = jnp.where(first, 0.0, carry_ref[:, half:])
        for t in range(ts):
            rows = slice(t * n_batch, (t + 1) * n_batch)
            bu = cur_buf[rows, :]
            re, im = ar * re - ai * im + bu[:, :half], ar * im + ai * re + bu[:, half:]
            cur_buf[rows, :] = jnp.concatenate([re, im], axis=-1)
        carry_ref[...] = jnp.concatenate([re, im], axis=-1)

    @pl.when(n % 2 == 0)
    def _():
        stages(buf_a, buf_b)

    @pl.when(n % 2 == 1)
    def _():
        stages(buf_b, buf_a)


def _s5_scan(u_tm, bbd, ar, ai, cbd, d, *, ts, n_batch):
    n_slabs, rows, _ = u_tm.shape
    tr = ts * n_batch
    tiles_per_slab = rows // tr
    n_tiles = n_slabs * tiles_per_slab
    n_state2 = bbd.shape[-1]
    tile = lambda shift: pl.BlockSpec(
        (tr, LANES), lambda n: (jnp.clip(n + shift, 0, n_tiles - 1), 0))
    u_flat = u_tm.reshape(n_slabs * rows, LANES)
    y = pl.pallas_call(
        functools.partial(_s5_scan_kernel, ts=ts, n_batch=n_batch,
                          tiles_per_slab=tiles_per_slab, n_tiles=n_tiles),
        grid=(n_tiles + 1,),
        in_specs=[tile(1), tile(-1), _const_spec(bbd.shape), _const_spec(ar.shape),
                  _const_spec(ai.shape), _const_spec(cbd.shape), _const_spec(d.shape)],
        out_specs=tile(-1),
        out_shape=jax.ShapeDtypeStruct(u_flat.shape, F32),
        scratch_shapes=[pltpu.VMEM((tr, n_state2), F32), pltpu.VMEM((tr, n_state2), F32),
                        pltpu.VMEM((n_batch, n_state2), F32)],
        compiler_params=_params("arbitrary"),
        name="s5_scan",
    )(u_flat, u_flat, bbd, ar, ai, cbd, d)
    return y.reshape(u_tm.shape)


def _glu_kernel(y_ref, w_ref, b_ref, o_ref, *, ts, n_batch):
    n_slabs = y_ref.shape[0]
    w = w_ref[...]
    bias = b_ref[...]
    for b in range(n_batch):
        yb = jnp.concatenate(
            [y_ref[s, pl.ds(b, ts, stride=n_batch), :] for s in range(n_slabs)], axis=-1)
        z = jnp.dot(yb.astype(BF16), w, preferred_element_type=F32) + bias
        o_ref[b] = (yb * jax.nn.sigmoid(z)).astype(BF16)


def _glu(y_tm, w_glu, b_glu, *, ts, n_batch):
    n_slabs, rows, _ = y_tm.shape
    seq = rows // n_batch
    width = n_slabs * LANES
    return pl.pallas_call(
        functools.partial(_glu_kernel, ts=ts, n_batch=n_batch),
        grid=(seq // ts,),
        in_specs=[
            pl.BlockSpec((n_slabs, ts * n_batch, LANES), lambda i: (0, i, 0)),
            _const_spec(w_glu.shape),
            _const_spec((1, width)),
        ],
        out_specs=pl.BlockSpec((n_batch, ts, width), lambda i: (0, i, 0)),
        out_shape=jax.ShapeDtypeStruct((n_batch, seq, width), BF16),
        compiler_params=_params("arbitrary"),
        name="glu",
    )(y_tm, w_glu, b_glu.reshape(1, width))


def _attn_kernel(qi_ref, kwq_ref, kwf_ref, q_ref, k_ref, vt_ref, o_ref,
                 sc_ref, lg_ref, bias_ref, q4_ref, mx_ref, m_ref, ls_ref, ot_ref,
                 *, tq, n_keep, idx_w_scale, qk_scale):
    i = pl.program_id(1)
    n_chunks = i + 1
    nt = (((1,), (1,)), ((), ()))
    fold = lambda a: a.reshape(tq // ACC_ROWS, ACC_ROWS, tq)

    key_pos = lax.broadcasted_iota(I32, (tq, tq), 0)
    qry_pos = lax.broadcasted_iota(I32, (tq, tq), 1)
    wi_t = kwq_ref[...].T[IDX_DIM:IDX_DIM + N_IDX_HEADS, :] * idx_w_scale

    def score_chunk(j):
        off = pl.multiple_of(j * tq, tq)
        kic = kwf_ref[pl.ds(off, tq), :][:, :IDX_DIM].astype(BF16)
        s = jnp.zeros((tq, tq), F32)
        for h in range(N_IDX_HEADS):
            r = lax.dot_general(kic, qi_ref[:, h * IDX_DIM:(h + 1) * IDX_DIM], nt,
                                preferred_element_type=F32)
            s = s + wi_t[h:h + 1, :] * jnp.maximum(r, 0.0)
        causal = (j - i) * tq + key_pos <= qry_pos
        sc_ref[j] = jnp.where(causal, s, -jnp.inf)

    _for_each_chunk(n_chunks, score_chunk)

    def decode(key):
        return pltpu.bitcast(jnp.where(key < 0, key ^ jnp.int32(0x7FFFFFFF), key), F32)

    def count_ge(thr, strict=False):
        thr_b = jnp.broadcast_to(thr, (ACC_ROWS, tq))

        def body(j, acc):
            s = fold(sc_ref[j])
            hit = s > thr_b if strict else s >= thr_b
            return acc + jnp.sum(jnp.where(hit, 1.0, 0.0), axis=0)

        acc = lax.fori_loop(0, n_chunks, body, jnp.zeros((ACC_ROWS, tq), F32))
        return jnp.sum(acc, axis=0, keepdims=True)

    keep = jnp.float32(n_keep)
    cnt0 = count_ge(jnp.zeros((1, tq), F32))
    tau = jnp.where(cnt0 >= keep, jnp.int32(0), jnp.int32(INT_MIN))
    cnt = jnp.where(cnt0 >= keep, cnt0, 0.0)

    def bit_body(bi, carry):
        tau, cnt = carry
        cand = tau + jnp.left_shift(jnp.int32(1), jnp.int32(30) - bi)
        c = count_ge(decode(cand))
        return jnp.where(c >= keep, cand, tau), jnp.where(c >= keep, c, cnt)

    tau, cnt = lax.fori_loop(0, 31, bit_body, (tau, cnt))
    thr = jnp.where(tau == jnp.int32(INT_MIN), jnp.finfo(F32).min, decode(tau))

    @pl.when(jnp.max(cnt) > keep)
    def _():
        need = keep - count_ge(thr, strict=True)
        lower = jnp.where(key_pos >= qry_pos, 1.0, 0.0).astype(BF16)

        def tie_chunk(j, run):
            s = sc_ref[j]
            eq = s == thr
            seen = run + jnp.dot(lower, jnp.where(eq, 1.0, 0.0).astype(BF16),
                                 preferred_element_type=F32)
            sc_ref[j] = jnp.where(eq, jnp.where(seen > need, -jnp.inf, s), s)
            return seen[tq - 1:tq, :]

        lax.fori_loop(0, n_chunks, tie_chunk, jnp.zeros((1, tq), F32))

    def mask_chunk(j, carry):
        sc_ref[j] = jnp.where(sc_ref[j] >= thr, 0.0, -1e30)
        return carry

    lax.fori_loop(0, n_chunks, mask_chunk, 0)

    slopes = [2.0 ** (-8.0 * (hd + 1) / N_HEADS) for hd in range(N_HEADS)]
    for hd in range(N_HEADS):
        c, g = divmod(hd, GQA_GROUP)
        q4_ref[c, g * tq:(g + 1) * tq, :] = q_ref[:, hd * HEAD_DIM:(hd + 1) * HEAD_DIM]
        bias_ref[hd] = slopes[hd] * key_pos.astype(F32)
    mx_ref[...] = jnp.full(mx_ref.shape, -jnp.inf, F32)
    ls_ref[...] = jnp.zeros_like(ls_ref)
    ot_ref[...] = jnp.zeros_like(ot_ref)

    def chunk_shift(j, hd):
        return ((j - i) * tq).astype(F32) * slopes[hd]

    def logits_chunk(j):
        off = pl.multiple_of(j * tq, tq)
        mask = sc_ref[j]
        for c in range(N_KV_HEADS):
            kc = k_ref[pl.ds(off, tq), :][:, c * HEAD_DIM:(c + 1) * HEAD_DIM]
            lg4 = lax.dot_general(kc, q4_ref[c], nt, preferred_element_type=F32) * qk_scale
            for g in range(GQA_GROUP):
                hd = c * GQA_GROUP + g
                lg = lg4[:, g * tq:(g + 1) * tq] + bias_ref[hd] + mask
                lg_ref[j, hd] = lg
                mx_ref[hd] = jnp.maximum(mx_ref[hd], jnp.max(fold(lg), axis=0) + chunk_shift(j, hd))

    _for_each_chunk(n_chunks, logits_chunk)
    for hd in range(N_HEADS):
        m_ref[hd] = jnp.max(mx_ref[hd], axis=0, keepdims=True)

    def pv_chunk(j):
        for c in range(N_KV_HEADS):
            vt = vt_ref[j, c * HEAD_DIM:(c + 1) * HEAD_DIM, :]
            for g in range(GQA_GROUP):
                hd = c * GQA_GROUP + g
                p = jnp.exp(lg_ref[j, hd] - (m_ref[hd] - chunk_shift(j, hd)))
                ls_ref[hd] += jnp.sum(fold(p), axis=0)
                ot_ref[hd] += jnp.dot(vt, p.astype(BF16), preferred_element_type=F32)

    _for_each_chunk(n_chunks, pv_chunk)
    for hd in range(N_HEADS):
        denom = jnp.sum(ls_ref[hd], axis=0, keepdims=True)
        o_ref[:, hd * HEAD_DIM:(hd + 1) * HEAD_DIM] = (ot_ref[hd] / denom).T.astype(BF16)


def _for_each_chunk(n_chunks, fn):
    def pair(p, carry):
        fn(2 * p)
        fn(2 * p + 1)
        return carry

    lax.fori_loop(0, n_chunks // 2, pair, 0)

    @pl.when(n_chunks % 2 == 1)
    def _():
        fn(n_chunks - 1)


def _attention(qi, kw, q, k, vt):
    n_batch, seq, _ = q.shape
    n_chunks, kv_w, tq = vt.shape[1:]
    n_keep = min(INDEX_TOPK, seq // 4)
    blk = lambda w: pl.BlockSpec((None, tq, w), lambda b, i: (b, i, 0))
    whole = lambda w: pl.BlockSpec((None, seq, w), lambda b, i: (b, 0, 0))
    return pl.pallas_call(
        functools.partial(_attn_kernel, tq=tq, n_keep=n_keep,
                          idx_w_scale=(N_IDX_HEADS ** -0.5) * (IDX_DIM ** -0.5),
                          qk_scale=HEAD_DIM ** -0.5),
        grid=(n_batch, n_chunks),
        in_specs=[blk(qi.shape[2]), blk(kw.shape[2]), whole(kw.shape[2]), blk(q.shape[2]),
                  whole(k.shape[2]),
                  pl.BlockSpec((None, n_chunks, kv_w, tq), lambda b, i: (b, 0, 0, 0))],
        out_specs=blk(q.shape[2]),
        out_shape=jax.ShapeDtypeStruct(q.shape, BF16),
        scratch_shapes=[pltpu.VMEM((n_chunks, tq, tq), F32),
                        pltpu.VMEM((n_chunks, N_HEADS, tq, tq), F32),
                        pltpu.VMEM((N_HEADS, tq, tq), F32),
                        pltpu.VMEM((N_KV_HEADS, GQA_GROUP * tq, HEAD_DIM), BF16),
                        pltpu.VMEM((N_HEADS, ACC_ROWS, tq), F32),
                        pltpu.VMEM((N_HEADS, 1, tq), F32),
                        pltpu.VMEM((N_HEADS, ACC_ROWS, tq), F32),
                        pltpu.VMEM((N_HEADS, HEAD_DIM, tq), F32)],
        compiler_params=_params("arbitrary", "arbitrary"),
        name="sparse_attn",
    )(qi, kw, kw, q, k, vt)


def _out_proj_kernel(ys_ref, ya_ref, x_ref, w_ref, g_ref, b_ref, o_ref):
    ssm_w = ys_ref.shape[1]
    for r in range(0, x_ref.shape[0], LN_ROWS):
        rows = slice(r, r + LN_ROWS)
        mix = jnp.dot(ys_ref[rows, :], w_ref[:ssm_w, :], preferred_element_type=F32)
        mix = mix + jnp.dot(ya_ref[rows, :], w_ref[ssm_w:, :], preferred_element_type=F32)
        o_ref[rows, :] = _layer_norm(DEEPNORM_ALPHA * x_ref[rows, :] + mix, g_ref[...], b_ref[...])


def _out_proj(ys, ya, x, w, g, b, *, tm):
    m, d = x.shape
    blk = lambda wd: pl.BlockSpec((tm, wd), lambda i: (i, 0))
    return pl.pallas_call(
        _out_proj_kernel,
        grid=(m // tm,),
        in_specs=[blk(ys.shape[1]), blk(ya.shape[1]), blk(d), _const_spec(w.shape),
                  _const_spec((1, d)), _const_spec((1, d))],
        out_specs=blk(d),
        out_shape=jax.ShapeDtypeStruct((m, d), F32),
        compiler_params=_params("arbitrary"),
        name="out_proj_ln1",
    )(ys, ya, x, w, g.reshape(1, d), b.reshape(1, d))


def _ffn_kernel(h_ref, halo_ref, wup_ref, wgate_ref, wdown_ref, cw_ref, cb_ref, g_ref, b_ref,
                o_ref, hb_ref, acc_ref):
    i = pl.program_id(1)
    j = pl.program_id(2)
    tm = h_ref.shape[0]

    @pl.when(j == 0)
    def _():
        hb_ref[:HALO_ROWS, :] = jnp.where(i == 0, 0.0, halo_ref[...]).astype(BF16)
        hb_ref[HALO_ROWS:, :] = h_ref[...].astype(BF16)
        acc_ref[...] = jnp.zeros_like(acc_ref)

    def chunk(last):
        ext = jnp.dot(hb_ref[...], wup_ref[...], preferred_element_type=F32)
        cw = cw_ref[...]
        hc = cb_ref[...]
        for lag in range(CONV_WIDTH):
            start = HALO_ROWS - lag
            hc = hc + cw[CONV_WIDTH - 1 - lag:CONV_WIDTH - lag, :] * ext[start:start + tm, :]
        gate = jnp.dot(hb_ref[HALO_ROWS:, :], wgate_ref[...], preferred_element_type=F32)
        act = (_gelu(hc) * gate).astype(BF16)
        if not last:
            acc_ref[...] += jnp.dot(act, wdown_ref[...], preferred_element_type=F32)
            return
        for r in range(0, tm, LN_ROWS):
            rows = slice(r, r + LN_ROWS)
            f = acc_ref[rows, :] + jnp.dot(act[rows, :], wdown_ref[...], preferred_element_type=F32)
            o_ref[rows, :] = _layer_norm(DEEPNORM_ALPHA * h_ref[rows, :] + f, g_ref[...], b_ref[...])

    is_last = j == pl.num_programs(2) - 1
    pl.when(jnp.logical_not(is_last))(functools.partial(chunk, False))
    pl.when(is_last)(functools.partial(chunk, True))


def _ffn(h, w_up, w_gate, w_down, conv_w, conv_b, g, b, *, tm, tf):
    n_batch, seq, d = h.shape
    f = w_up.shape[1]
    halo_blocks = tm // HALO_ROWS
    return pl.pallas_call(
        _ffn_kernel,
        grid=(n_batch, seq // tm, f // tf),
        in_specs=[
            pl.BlockSpec((None, tm, d), lambda bb, i, j: (bb, i, 0)),
            pl.BlockSpec((None, HALO_ROWS, d),
                         lambda bb, i, j: (bb, jnp.maximum(i * halo_blocks - 1, 0), 0)),
            pl.BlockSpec((d, tf), lambda bb, i, j: (0, j)),
            pl.BlockSpec((d, tf), lambda bb, i, j: (0, j)),
            pl.BlockSpec((tf, d), lambda bb, i, j: (j, 0)),
            pl.BlockSpec((CONV_WIDTH, tf), lambda bb, i, j: (0, j)),
            pl.BlockSpec((1, tf), lambda bb, i, j: (0, j)),
            _const_spec((1, d)),
            _const_spec((1, d)),
        ],
        out_specs=pl.BlockSpec((None, tm, d), lambda bb, i, j: (bb, i, 0)),
        out_shape=jax.ShapeDtypeStruct((n_batch, seq, d), F32),
        scratch_shapes=[pltpu.VMEM((HALO_ROWS + tm, d), BF16), pltpu.VMEM((tm, d), F32)],
        compiler_params=_params("arbitrary", "arbitrary", "arbitrary"),
        name="ffn_ln2",
    )(h, h, w_up, w_gate, w_down, conv_w, conv_b.reshape(1, f), g.reshape(1, d), b.reshape(1, d))


def _layer(h, w_in, a_re, a_im, log_dt, b_re, b_im, c_re, c_im, ssm_d, w_glu, b_glu, w_out,
           ln1_g, ln1_b, w_up, w_gate, conv_w, conv_b, w_down, ln2_g, ln2_b):
    n_batch, seq, d = h.shape
    assert n_batch == SUBLANES, "the time-major scan layout puts the batch on the sublanes"
    n_groups, n_state = a_re.shape
    ssm_w = n_groups * SSM_GROUP_CH
    attn_w = N_HEADS * HEAD_DIM
    kv_w = N_KV_HEADS * HEAD_DIM
    qi_w = N_IDX_HEADS * IDX_DIM
    n_slabs = ssm_w // LANES
    assert n_state == SSM_STATE and ssm_w % LANES == 0
    t_seq, t_row, t_ff, t_scan = _tiles(seq, w_up.shape[1])

    widths = (ssm_w, attn_w, kv_w, kv_w, qi_w, LANES)
    assert w_in.shape[1] == sum(widths[:-1]) + IDX_DIM + N_IDX_HEADS
    w_in_b = jnp.pad(w_in, ((0, 0), (0, sum(widths) - w_in.shape[1]))).astype(BF16)
    u_tm, q, k, vt, qi, kw = _in_proj(h, w_in_b, widths, ts=t_seq)

    abar_re, abar_im, bbar_re, bbar_im = _s5_discretize(
        a_re, a_im, log_dt, jnp.swapaxes(b_re, 1, 2), jnp.swapaxes(b_im, 1, 2))
    gl = GROUPS_PER_SLAB
    to_slabs = lambda m: m.reshape(n_slabs, gl, *m.shape[1:])
    bbd = jnp.concatenate([_block_diag_slabs(to_slabs(bbar_re)),
                           _block_diag_slabs(to_slabs(bbar_im))], axis=-1).astype(BF16)
    cbd_re = _block_diag_slabs(to_slabs(jnp.swapaxes(c_re, 1, 2)))
    cbd_im = _block_diag_slabs(to_slabs(jnp.swapaxes(c_im, 1, 2)))
    cbd = jnp.concatenate([cbd_re, -cbd_im], axis=1).astype(BF16)
    bcast = lambda m: jnp.broadcast_to(m.reshape(n_slabs, 1, gl * n_state), (n_slabs, n_batch, gl * n_state))
    y_tm = _s5_scan(u_tm, bbd, bcast(abar_re), bcast(abar_im), cbd,
                    ssm_d.reshape(n_slabs, 1, LANES), ts=t_scan, n_batch=n_batch)
    y_ssm = _glu(y_tm, w_glu.astype(BF16), b_glu, ts=t_seq, n_batch=n_batch)

    y_attn = _attention(qi, kw, q, k, vt)

    h1 = _out_proj(y_ssm.reshape(n_batch * seq, ssm_w), y_attn.reshape(n_batch * seq, attn_w),
                   h.reshape(n_batch * seq, d), w_out.astype(BF16), ln1_g, ln1_b, tm=t_row)
    return _ffn(h1.reshape(n_batch, seq, d), w_up.astype(BF16), w_gate.astype(BF16),
                w_down.astype(BF16), conv_w, conv_b, ln2_g, ln2_b, tm=t_row, tf=t_ff)


def kernel(x, w_in, ssm_a_re, ssm_a_im, ssm_log_dt, ssm_b_re, ssm_b_im, ssm_c_re, ssm_c_im, ssm_d,
           w_glu, b_glu, w_out, ln1_g, ln1_b, w_up, w_gate, conv_w, conv_b, w_down, ln2_g, ln2_b):
    h = x
    for l in range(w_in.shape[0]):
        h = _layer(h, w_in[l], ssm_a_re[l], ssm_a_im[l], ssm_log_dt[l], ssm_b_re[l], ssm_b_im[l],
                   ssm_c_re[l], ssm_c_im[l], ssm_d[l], w_glu[l], b_glu[l], w_out[l], ln1_g[l],
                   ln1_b[l], w_up[l], w_gate[l], conv_w[l], conv_b[l], w_down[l], ln2_g[l], ln2_b[l])
    return h
```

```python
import functools
import math

import jax
import jax.numpy as jnp
from jax import lax
from jax.experimental import pallas as pl
from jax.experimental.pallas import tpu as pltpu

F32 = jnp.float32
BF16 = jnp.bfloat16
I32 = jnp.int32

LANES = 128
SUBLANES = 8
VMEM_LIMIT_BYTES = 56 * 1024 * 1024

SSM_GROUP_CH = 16
SSM_STATE = 64
N_HEADS = 8
HEAD_DIM = 128
N_KV_HEADS = 2
GQA_GROUP = N_HEADS // N_KV_HEADS
N_IDX_HEADS = 16
IDX_DIM = 64
INDEX_TOPK = 256
CONV_WIDTH = 3
LN_EPS = 1e-5
DEPTH = 1
DEEPNORM_ALPHA = (2.0 * DEPTH) ** 0.25

GROUPS_PER_SLAB = LANES // SSM_GROUP_CH
SLAB_STATE = GROUPS_PER_SLAB * SSM_STATE
INT_MIN = -(2 ** 31)
ACC_ROWS = 4 * SUBLANES
LN_ROWS = 128
HALO_ROWS = 2 * SUBLANES


def _tiles(seq, d_ff):
    t_seq = min(256, seq)
    t_row = min(512, seq)
    t_ff = 512 if d_ff % 512 == 0 else LANES
    t_scan = min(128, seq)
    assert seq % t_seq == 0 and seq % t_row == 0 and d_ff % t_ff == 0 and seq % t_scan == 0
    return t_seq, t_row, t_ff, t_scan


def _params(*sem):
    return pltpu.CompilerParams(dimension_semantics=sem, vmem_limit_bytes=VMEM_LIMIT_BYTES)


def _const_spec(shape):
    zeros = (0,) * len(shape)
    return pl.BlockSpec(shape, lambda *_: zeros, pipeline_mode=pl.Buffered(1))


def _layer_norm(v, g, b):
    mu = jnp.mean(v, axis=-1, keepdims=True)
    vc = v - mu
    var = jnp.mean(vc * vc, axis=-1, keepdims=True)
    return vc * lax.rsqrt(var + LN_EPS) * g + b


def _gelu(v):
    return 0.5 * v * (1.0 + lax.erf(v * (1.0 / math.sqrt(2.0))))


def _in_proj_kernel(x_ref, w_ref, u_ref, q_ref, k_ref, v_ref, qi_ref, kw_ref, *, n_batch, widths):
    b = pl.program_id(1)
    xb = x_ref[...].astype(BF16)
    ts = xb.shape[0]
    starts = [sum(widths[:n]) for n in range(len(widths))]
    proj = lambda n: jnp.dot(xb, w_ref[:, starts[n]:starts[n] + widths[n]],
                             preferred_element_type=F32)
    u = proj(0)
    for s in range(u.shape[1] // LANES):
        u_ref[s, pl.ds(b, ts, stride=n_batch), :] = u[:, s * LANES:(s + 1) * LANES]
    q_ref[...] = proj(1).astype(BF16)
    k_ref[...] = proj(2).astype(BF16)
    v_ref[...] = proj(3).T.astype(BF16)
    qi_ref[...] = proj(4).astype(BF16)
    kw_ref[...] = proj(5)


def _in_proj(x, w, widths, *, ts):
    n_batch, seq, d = x.shape
    w_u, w_q, w_k, w_v, w_qi, w_kw = widths
    n_slabs = w_u // LANES
    row = lambda wd: pl.BlockSpec((None, ts, wd), lambda i, b: (b, i, 0))
    out_shape = (
        jax.ShapeDtypeStruct((n_slabs, seq * n_batch, LANES), F32),
        jax.ShapeDtypeStruct((n_batch, seq, w_q), BF16),
        jax.ShapeDtypeStruct((n_batch, seq, w_k), BF16),
        jax.ShapeDtypeStruct((n_batch, seq // ts, w_v, ts), BF16),
        jax.ShapeDtypeStruct((n_batch, seq, w_qi), BF16),
        jax.ShapeDtypeStruct((n_batch, seq, w_kw), F32),
    )
    return pl.pallas_call(
        functools.partial(_in_proj_kernel, n_batch=n_batch, widths=widths),
        grid=(seq // ts, n_batch),
        in_specs=[row(d), _const_spec(w.shape)],
        out_specs=(
            pl.BlockSpec((n_slabs, ts * n_batch, LANES), lambda i, b: (0, i, 0)),
            row(w_q), row(w_k),
            pl.BlockSpec((None, None, w_v, ts), lambda i, b: (b, i, 0, 0)),
            row(w_qi), row(w_kw),
        ),
        out_shape=out_shape,
        compiler_params=_params("arbitrary", "arbitrary"),
        name="in_proj",
    )(x, w)


def _s5_discretize_kernel(a_re_ref, a_im_ref, log_dt_ref, bt_re_ref, bt_im_ref,
                          abar_re_ref, abar_im_ref, bbar_re_ref, bbar_im_ref):
    a_re = a_re_ref[...]
    a_im = a_im_ref[...]
    dt = jnp.exp(log_dt_ref[...])
    mag = jnp.exp(dt * a_re)
    ang = dt * a_im
    abar_re = mag * jnp.cos(ang)
    abar_im = mag * jnp.sin(ang)
    num_re = abar_re - 1.0
    num_im = abar_im
    den = a_re * a_re + a_im * a_im
    f_re = (num_re * a_re + num_im * a_im) / den
    f_im = (num_im * a_re - num_re * a_im) / den
    abar_re_ref[...] = abar_re
    abar_im_ref[...] = abar_im
    bt_re = bt_re_ref[...]
    bt_im = bt_im_ref[...]
    bbar_re_ref[...] = f_re * bt_re - f_im * bt_im
    bbar_im_ref[...] = f_re * bt_im + f_im * bt_re


def _s5_discretize(a_re, a_im, log_dt, bt_re, bt_im):
    g, p = a_re.shape
    abar_re, abar_im, bbar_re, bbar_im = pl.pallas_call(
        _s5_discretize_kernel,
        out_shape=(jax.ShapeDtypeStruct((g, 1, p), F32), jax.ShapeDtypeStruct((g, 1, p), F32),
                   jax.ShapeDtypeStruct(bt_re.shape, F32), jax.ShapeDtypeStruct(bt_re.shape, F32)),
        name="s5_discretize",
    )(a_re.reshape(g, 1, p), a_im.reshape(g, 1, p), log_dt.reshape(g, 1, 1), bt_re, bt_im)
    return abar_re.reshape(g, p), abar_im.reshape(g, p), bbar_re, bbar_im


def _block_diag_slabs(m):
    n, gl, a, b = m.shape
    eye = jnp.eye(gl, dtype=m.dtype)
    full = m[:, :, :, None, :] * eye[None, :, None, :, None]
    return full.reshape(n, gl * a, gl * b)


def _s5_scan_kernel(u_next_ref, u_prev_ref, bbd_ref, ar_ref, ai_ref, cbd_ref, d_ref, y_ref,
                    buf_a, buf_b, carry_ref, *, ts, n_batch, tiles_per_slab, n_tiles):
    n = pl.program_id(0)
    last = n_tiles - 1
    cur = jnp.minimum(n, last)
    slab_next = jnp.minimum(n + 1, last) // tiles_per_slab
    slab_cur = cur // tiles_per_slab
    slab_prev = jnp.maximum(n - 1, 0) // tiles_per_slab
    half = ar_ref.shape[-1]

    @pl.when(n == 0)
    def _():
        buf_a[...] = jnp.dot(u_prev_ref[...].astype(BF16), bbd_ref[0], preferred_element_type=F32)
        buf_b[...] = jnp.zeros_like(buf_b)
        carry_ref[...] = jnp.zeros_like(carry_ref)

    def stages(cur_buf, other_buf):
        y = jnp.dot(other_buf[...].astype(BF16), cbd_ref[slab_prev], preferred_element_type=F32)
        y_ref[...] = _gelu(y + d_ref[slab_prev] * u_prev_ref[...])
        other_buf[...] = jnp.dot(u_next_ref[...].astype(BF16), bbd_ref[slab_next],
                                 preferred_element_type=F32)
        ar = ar_ref[slab_cur]
        ai = ai_ref[slab_cur]
        first = cur % tiles_per_slab == 0
        re = jnp.where(first, 0.0, carry_ref[:, :half])
        im = jnp.where(first, 0.0, carry_ref[:, half:])
        for t in range(ts):
            rows = slice(t * n_batch, (t + 1) * n_batch)
            bu = cur_buf[rows, :]
            re, im = ar * re - ai * im + bu[:, :half], ar * im + ai * re + bu[:, half:]
            cur_buf[rows, :] = jnp.concatenate([re, im], axis=-1)
        carry_ref[...] = jnp.concatenate([re, im], axis=-1)

    @pl.when(n % 2 == 0)
    def _():
        stages(buf_a, buf_b)

    @pl.when(n % 2 == 1)
    def _():
        stages(buf_b, buf_a)


def _s5_scan(u_tm, bbd, ar, ai, cbd, d, *, ts, n_batch):
    n_slabs, rows, _ = u_tm.shape
    tr = ts * n_batch
    tiles_per_slab = rows // tr
    n_tiles = n_slabs * tiles_per_slab
    n_state2 = bbd.shape[-1]
    tile = lambda shift: pl.BlockSpec(
        (tr, LANES), lambda n: (jnp.clip(n + shift, 0, n_tiles - 1), 0))
    u_flat = u_tm.reshape(n_slabs * rows, LANES)
    y = pl.pallas_call(
        functools.partial(_s5_scan_kernel, ts=ts, n_batch=n_batch,
                          tiles_per_slab=tiles_per_slab, n_tiles=n_tiles),
        grid=(n_tiles + 1,),
        in_specs=[tile(1), tile(-1), _const_spec(bbd.shape), _const_spec(ar.shape),
                  _const_spec(ai.shape), _const_spec(cbd.shape), _const_spec(d.shape)],
        out_specs=tile(-1),
        out_shape=jax.ShapeDtypeStruct(u_flat.shape, F32),
        scratch_shapes=[pltpu.VMEM((tr, n_state2), F32), pltpu.VMEM((tr, n_state2), F32),
                        pltpu.VMEM((n_batch, n_state2), F32)],
        compiler_params=_params("arbitrary"),
        name="s5_scan",
    )(u_flat, u_flat, bbd, ar, ai, cbd, d)
    return y.reshape(u_tm.shape)


def _glu_kernel(y_ref, w_ref, b_ref, o_ref, *, ts, n_batch):
    n_slabs = y_ref.shape[0]
    w = w_ref[...]
    bias = b_ref[...]
    for b in range(n_batch):
        yb = jnp.concatenate(
            [y_ref[s, pl.ds(b, ts, stride=n_batch), :] for s in range(n_slabs)], axis=-1)
        z = jnp.dot(yb.astype(BF16), w, preferred_element_type=F32) + bias
        o_ref[b] = (yb * jax.nn.sigmoid(z)).astype(BF16)


def _glu(y_tm, w_glu, b_glu, *, ts, n_batch):
    n_slabs, rows, _ = y_tm.shape
    seq = rows // n_batch
    width = n_slabs * LANES
    return pl.pallas_call(
        functools.partial(_glu_kernel, ts=ts, n_batch=n_batch),
        grid=(seq // ts,),
        in_specs=[
            pl.BlockSpec((n_slabs, ts * n_batch, LANES), lambda i: (0, i, 0)),
            _const_spec(w_glu.shape),
            _const_spec((1, width)),
        ],
        out_specs=pl.BlockSpec((n_batch, ts, width), lambda i: (0, i, 0)),
        out_shape=jax.ShapeDtypeStruct((n_batch, seq, width), BF16),
        compiler_params=_params("arbitrary"),
        name="glu",
    )(y_tm, w_glu, b_glu.reshape(1, width))


def _attn_kernel(qi_ref, kwq_ref, kwf_ref, q_ref, k_ref, vt_ref, o_ref,
                 sc_ref, lg_ref, bias_ref, q4_ref, mx_ref, m_ref, ls_ref, ot_ref,
                 *, tq, n_keep, idx_w_scale, qk_scale):
    i = pl.program_id(1)
    n_chunks = i + 1
    nt = (((1,), (1,)), ((), ()))
    fold = lambda a: a.reshape(tq // ACC_ROWS, ACC_ROWS, tq)

    key_pos = lax.broadcasted_iota(I32, (tq, tq), 0)
    qry_pos = lax.broadcasted_iota(I32, (tq, tq), 1)
    wi_t = kwq_ref[...].T[IDX_DIM:IDX_DIM + N_IDX_HEADS, :] * idx_w_scale

    def score_chunk(j):
        off = pl.multiple_of(j * tq, tq)
        kic = kwf_ref[pl.ds(off, tq), :][:, :IDX_DIM].astype(BF16)
        s = jnp.zeros((tq, tq), F32)
        for h in range(N_IDX_HEADS):
            r = lax.dot_general(kic, qi_ref[:, h * IDX_DIM:(h + 1) * IDX_DIM], nt,
                                preferred_element_type=F32)
            s = s + wi_t[h:h + 1, :] * jnp.maximum(r, 0.0)
        causal = (j - i) * tq + key_pos <= qry_pos
        sc_ref[j] = jnp.where(causal, s, -jnp.inf)

    _for_each_chunk(n_chunks, score_chunk)

    def decode(key):
        return pltpu.bitcast(jnp.where(key < 0, key ^ jnp.int32(0x7FFFFFFF), key), F32)

    def count_ge(thr, strict=False):
        thr_b = jnp.broadcast_to(thr, (ACC_ROWS, tq))

        def body(j, acc):
            s = fold(sc_ref[j])
            hit = s > thr_b if strict else s >= thr_b
            return acc + jnp.sum(jnp.where(hit, 1.0, 0.0), axis=0)

        acc = lax.fori_loop(0, n_chunks, body, jnp.zeros((ACC_ROWS, tq), F32))
        return jnp.sum(acc, axis=0, keepdims=True)

    keep = jnp.float32(n_keep)
    cnt0 = count_ge(jnp.zeros((1, tq), F32))
    tau = jnp.where(cnt0 >= keep, jnp.int32(0), jnp.int32(INT_MIN))
    cnt = jnp.where(cnt0 >= keep, cnt0, 0.0)

    def bit_body(bi, carry):
        tau, cnt = carry
        cand = tau + jnp.left_shift(jnp.int32(1), jnp.int32(30) - bi)
        c = count_ge(decode(cand))
        return jnp.where(c >= keep, cand, tau), jnp.where(c >= keep, c, cnt)

    tau, cnt = lax.fori_loop(0, 31, bit_body, (tau, cnt))
    thr = jnp.where(tau == jnp.int32(INT_MIN), jnp.finfo(F32).min, decode(tau))

    @pl.when(jnp.max(cnt) > keep)
    def _():
        need = keep - count_ge(thr, strict=True)
        lower = jnp.where(key_pos >= qry_pos, 1.0, 0.0).astype(BF16)

        def tie_chunk(j, run):
            s = sc_ref[j]
            eq = s == thr
            seen = run + jnp.dot(lower, jnp.where(eq, 1.0, 0.0).astype(BF16),
                                 preferred_element_type=F32)
            sc_ref[j] = jnp.where(eq, jnp.where(seen > need, -jnp.inf, s), s)
            return seen[tq - 1:tq, :]

        lax.fori_loop(0, n_chunks, tie_chunk, jnp.zeros((1, tq), F32))

    def mask_chunk(j, carry):
        sc_ref[j] = jnp.where(sc_ref[j] >= thr, 0.0, -1e30)
        return carry

    lax.fori_loop(0, n_chunks, mask_chunk, 0)

    slopes = [2.0 ** (-8.0 * (hd + 1) / N_HEADS) for hd in range(N_HEADS)]
    for hd in range(N_HEADS):
        c, g = divmod(hd, GQA_GROUP)
        q4_ref[c, g * tq:(g + 1) * tq, :] = q_ref[:, hd * HEAD_DIM:(hd + 1) * HEAD_DIM]
        bias_ref[hd] = slopes[hd] * key_pos.astype(F32)
    mx_ref[...] = jnp.full(mx_ref.shape, -jnp.inf, F32)
    ls_ref[...] = jnp.zeros_like(ls_ref)
    ot_ref[...] = jnp.zeros_like(ot_ref)

    def chunk_shift(j, hd):
        return ((j - i) * tq).astype(F32) * slopes[hd]

    def logits_chunk(j):
        off = pl.multiple_of(j * tq, tq)
        mask = sc_ref[j]
        for c in range(N_KV_HEADS):
            kc = k_ref[pl.ds(off, tq), :][:, c * HEAD_DIM:(c + 1) * HEAD_DIM]
            lg4 = lax.dot_general(kc, q4_ref[c], nt, preferred_element_type=F32) * qk_scale
            for g in range(GQA_GROUP):
                hd = c * GQA_GROUP + g
                lg = lg4[:, g * tq:(g + 1) * tq] + bias_ref[hd] + mask
                lg_ref[j, hd] = lg
                mx_ref[hd] = jnp.maximum(mx_ref[hd], jnp.max(fold(lg), axis=0) + chunk_shift(j, hd))

    _for_each_chunk(n_chunks, logits_chunk)
    for hd in range(N_HEADS):
        m_ref[hd] = jnp.max(mx_ref[hd], axis=0, keepdims=True)

    def pv_chunk(j):
        for c in range(N_KV_HEADS):
            vt = vt_ref[j, c * HEAD_DIM:(c + 1) * HEAD_DIM, :]
            for g in range(GQA_GROUP):
                hd = c * GQA_GROUP + g
                p = jnp.exp(lg_ref[j, hd] - (m_ref[hd] - chunk_shift(j, hd)))
                ls_ref[hd] += jnp.sum(fold(p), axis=0)
                ot_ref[hd] += jnp.dot(vt, p.astype(BF16), preferred_element_type=F32)

    _for_each_chunk(n_chunks, pv_chunk)
    for hd in range(N_HEADS):
        denom = jnp.sum(ls_ref[hd], axis=0, keepdims=True)
        o_ref[:, hd * HEAD_DIM:(hd + 1) * HEAD_DIM] = (ot_ref[hd] / denom).T.astype(BF16)


def _for_each_chunk(n_chunks, fn):
    def pair(p, carry):
        fn(2 * p)
        fn(2 * p + 1)
        return carry

    lax.fori_loop(0, n_chunks // 2, pair, 0)

    @pl.when(n_chunks % 2 == 1)
    def _():
        fn(n_chunks - 1)


def _attention(qi, kw, q, k, vt):
    n_batch, seq, _ = q.shape
    n_chunks, kv_w, tq = vt.shape[1:]
    n_keep = min(INDEX_TOPK, seq // 4)
    blk = lambda w: pl.BlockSpec((None, tq, w), lambda b, i: (b, i, 0))
    whole = lambda w: pl.BlockSpec((None, seq, w), lambda b, i: (b, 0, 0))
    return pl.pallas_call(
        functools.partial(_attn_kernel, tq=tq, n_keep=n_keep,
                          idx_w_scale=(N_IDX_HEADS ** -0.5) * (IDX_DIM ** -0.5),
                          qk_scale=HEAD_DIM ** -0.5),
        grid=(n_batch, n_chunks),
        in_specs=[blk(qi.shape[2]), blk(kw.shape[2]), whole(kw.shape[2]), blk(q.shape[2]),
                  whole(k.shape[2]),
                  pl.BlockSpec((None, n_chunks, kv_w, tq), lambda b, i: (b, 0, 0, 0))],
        out_specs=blk(q.shape[2]),
        out_shape=jax.ShapeDtypeStruct(q.shape, BF16),
        scratch_shapes=[pltpu.VMEM((n_chunks, tq, tq), F32),
                        pltpu.VMEM((n_chunks, N_HEADS, tq, tq), F32),
                        pltpu.VMEM((N_HEADS, tq, tq), F32),
                        pltpu.VMEM((N_KV_HEADS, GQA_GROUP * tq, HEAD_DIM), BF16),
                        pltpu.VMEM((N_HEADS, ACC_ROWS, tq), F32),
                        pltpu.VMEM((N_HEADS, 1, tq), F32),
                        pltpu.VMEM((N_HEADS, ACC_ROWS, tq), F32),
                        pltpu.VMEM((N_HEADS, HEAD_DIM, tq), F32)],
        compiler_params=_params("arbitrary", "arbitrary"),
        name="sparse_attn",
    )(qi, kw, kw, q, k, vt)


def _out_proj_kernel(ys_ref, ya_ref, x_ref, w_ref, g_ref, b_ref, o_ref):
    ssm_w = ys_ref.shape[1]
    for r in range(0, x_ref.shape[0], LN_ROWS):
        rows = slice(r, r + LN_ROWS)
        mix = jnp.dot(ys_ref[rows, :], w_ref[:ssm_w, :], preferred_element_type=F32)
        mix = mix + jnp.dot(ya_ref[rows, :], w_ref[ssm_w:, :], preferred_element_type=F32)
        o_ref[rows, :] = _layer_norm(DEEPNORM_ALPHA * x_ref[rows, :] + mix, g_ref[...], b_ref[...])


def _out_proj(ys, ya, x, w, g, b, *, tm):
    m, d = x.shape
    blk = lambda wd: pl.BlockSpec((tm, wd), lambda i: (i, 0))
    return pl.pallas_call(
        _out_proj_kernel,
        grid=(m // tm,),
        in_specs=[blk(ys.shape[1]), blk(ya.shape[1]), blk(d), _const_spec(w.shape),
                  _const_spec((1, d)), _const_spec((1, d))],
        out_specs=blk(d),
        out_shape=jax.ShapeDtypeStruct((m, d), F32),
        compiler_params=_params("arbitrary"),
        name="out_proj_ln1",
    )(ys, ya, x, w, g.reshape(1, d), b.reshape(1, d))


def _ffn_kernel(h_ref, halo_ref, wup_ref, wgate_ref, wdown_ref, cw_ref, cb_ref, g_ref, b_ref,
                o_ref, hb_ref, acc_ref):
    i = pl.program_id(1)
    j = pl.program_id(2)
    tm = h_ref.shape[0]

    @pl.when(j == 0)
    def _():
        hb_ref[:HALO_ROWS, :] = jnp.where(i == 0, 0.0, halo_ref[...]).astype(BF16)
        hb_ref[HALO_ROWS:, :] = h_ref[...].astype(BF16)
        acc_ref[...] = jnp.zeros_like(acc_ref)

    ext = jnp.dot(hb_ref[...], wup_ref[...], preferred_element_type=F32)
    cw = cw_ref[...]
    hc = cb_ref[...]
    for lag in range(CONV_WIDTH):
        start = HALO_ROWS - lag
        hc = hc + cw[CONV_WIDTH - 1 - lag:CONV_WIDTH - lag, :] * ext[start:start + tm, :]
    gate = jnp.dot(hb_ref[HALO_ROWS:, :], wgate_ref[...], preferred_element_type=F32)
    act = (_gelu(hc) * gate).astype(BF16)
    acc_ref[...] += jnp.dot(act, wdown_ref[...], preferred_element_type=F32)

    @pl.when(j == pl.num_programs(2) - 1)
    def _():
        o_ref[...] = _layer_norm(DEEPNORM_ALPHA * h_ref[...] + acc_ref[...], g_ref[...], b_ref[...])


def _ffn(h, w_up, w_gate, w_down, conv_w, conv_b, g, b, *, tm, tf):
    n_batch, seq, d = h.shape
    f = w_up.shape[1]
    halo_blocks = tm // HALO_ROWS
    return pl.pallas_call(
        _ffn_kernel,
        grid=(n_batch, seq // tm, f // tf),
        in_specs=[
            pl.BlockSpec((None, tm, d), lambda bb, i, j: (bb, i, 0)),
            pl.BlockSpec((None, HALO_ROWS, d),
                         lambda bb, i, j: (bb, jnp.maximum(i * halo_blocks - 1, 0), 0)),
            pl.BlockSpec((d, tf), lambda bb, i, j: (0, j)),
            pl.BlockSpec((d, tf), lambda bb, i, j: (0, j)),
            pl.BlockSpec((tf, d), lambda bb, i, j: (j, 0)),
            pl.BlockSpec((CONV_WIDTH, tf), lambda bb, i, j: (0, j)),
            pl.BlockSpec((1, tf), lambda bb, i, j: (0, j)),
            _const_spec((1, d)),
            _const_spec((1, d)),
        ],
        out_specs=pl.BlockSpec((None, tm, d), lambda bb, i, j: (bb, i, 0)),
        out_shape=jax.ShapeDtypeStruct((n_batch, seq, d), F32),
        scratch_shapes=[pltpu.VMEM((HALO_ROWS + tm, d), BF16), pltpu.VMEM((tm, d), F32)],
        compiler_params=_params("arbitrary", "arbitrary", "arbitrary"),
        name="ffn_ln2",
    )(h, h, w_up, w_gate, w_down, conv_w, conv_b.reshape(1, f), g.reshape(1, d), b.reshape(1, d))


def _layer(h, w_in, a_re, a_im, log_dt, b_re, b_im, c_re, c_im, ssm_d, w_glu, b_glu, w_out,
           ln1_g, ln1_b, w_up, w_gate, conv_w, conv_b, w_down, ln2_g, ln2_b):
    n_batch, seq, d = h.shape
    assert n_batch == SUBLANES, "the time-major scan layout puts the batch on the sublanes"
    n_groups, n_state = a_re.shape
    ssm_w = n_groups * SSM_GROUP_CH
    attn_w = N_HEADS * HEAD_DIM
    kv_w = N_KV_HEADS * HEAD_DIM
    qi_w = N_IDX_HEADS * IDX_DIM
    n_slabs = ssm_w // LANES
    assert n_state == SSM_STATE and ssm_w % LANES == 0
    t_seq, t_row, t_ff, t_scan = _tiles(seq, w_up.shape[1])

    widths = (ssm_w, attn_w, kv_w, kv_w, qi_w, LANES)
    assert w_in.shape[1] == sum(widths[:-1]) + IDX_DIM + N_IDX_HEADS
    w_in_b = jnp.pad(w_in, ((0, 0), (0, sum(widths) - w_in.shape[1]))).astype(BF16)
    u_tm, q, k, vt, qi, kw = _in_proj(h, w_in_b, widths, ts=t_seq)

    abar_re, abar_im, bbar_re, bbar_im = _s5_discretize(
        a_re, a_im, log_dt, jnp.swapaxes(b_re, 1, 2), jnp.swapaxes(b_im, 1, 2))
    gl = GROUPS_PER_SLAB
    to_slabs = lambda m: m.reshape(n_slabs, gl, *m.shape[1:])
    bbd = jnp.concatenate([_block_diag_slabs(to_slabs(bbar_re)),
                           _block_diag_slabs(to_slabs(bbar_im))], axis=-1).astype(BF16)
    cbd_re = _block_diag_slabs(to_slabs(jnp.swapaxes(c_re, 1, 2)))
    cbd_im = _block_diag_slabs(to_slabs(jnp.swapaxes(c_im, 1, 2)))
    cbd = jnp.concatenate([cbd_re, -cbd_im], axis=1).astype(BF16)
    bcast = lambda m: jnp.broadcast_to(m.reshape(n_slabs, 1, gl * n_state), (n_slabs, n_batch, gl * n_state))
    y_tm = _s5_scan(u_tm, bbd, bcast(abar_re), bcast(abar_im), cbd,
                    ssm_d.reshape(n_slabs, 1, LANES), ts=t_scan, n_batch=n_batch)
    y_ssm = _glu(y_tm, w_glu.astype(BF16), b_glu, ts=t_seq, n_batch=n_batch)

    y_attn = _attention(qi, kw, q, k, vt)

    h1 = _out_proj(y_ssm.reshape(n_batch * seq, ssm_w), y_attn.reshape(n_batch * seq, attn_w),
                   h.reshape(n_batch * seq, d), w_out.astype(BF16), ln1_g, ln1_b, tm=t_row)
    return _ffn(h1.reshape(n_batch, seq, d), w_up.astype(BF16), w_gate.astype(BF16),
                w_down.astype(BF16), conv_w, conv_b, ln2_g, ln2_b, tm=t_row, tf=t_ff)


def kernel(x, w_in, ssm_a_re, ssm_a_im, ssm_log_dt, ssm_b_re, ssm_b_im, ssm_c_re, ssm_c_im, ssm_d,
           w_glu, b_glu, w_out, ln1_g, ln1_b, w_up, w_gate, conv_w, conv_b, w_down, ln2_g, ln2_b):
    h = x
    for l in range(w_in.shape[0]):
        h = _layer(h, w_in[l], ssm_a_re[l], ssm_a_im[l], ssm_log_dt[l], ssm_b_re[l], ssm_b_im[l],
                   ssm_c_re[l], ssm_c_im[l], ssm_d[l], w_glu[l], b_glu[l], w_out[l], ln1_g[l],
                   ln1_b[l], w_up[l], w_gate[l], conv_w[l], conv_b[l], w_down[l], ln2_g[l], ln2_b[l])
    return h
```

```python
import functools
import math

import jax
import jax.numpy as jnp
from jax import lax
from jax.experimental import pallas as pl
from jax.experimental.pallas import tpu as pltpu

F32 = jnp.float32
BF16 = jnp.bfloat16
I32 = jnp.int32

LANES = 128
SUBLANES = 8
VMEM_LIMIT_BYTES = 56 * 1024 * 1024

SSM_GROUP_CH = 16
SSM_STATE = 64
N_HEADS = 8
HEAD_DIM = 128
N_KV_HEADS = 2
GQA_GROUP = N_HEADS // N_KV_HEADS
N_IDX_HEADS = 16
IDX_DIM = 64
INDEX_TOPK = 256
CONV_WIDTH = 3
LN_EPS = 1e-5
DEPTH = 1
DEEPNORM_ALPHA = (2.0 * DEPTH) ** 0.25

GROUPS_PER_SLAB = LANES // SSM_GROUP_CH
SLAB_STATE = GROUPS_PER_SLAB * SSM_STATE
INT_MIN = -(2 ** 31)
ACC_ROWS = 4 * SUBLANES
LN_ROWS = 128
HALO_ROWS = 2 * SUBLANES


def _tiles(seq, d_ff):
    t_seq = min(256, seq)
    t_row = min(512, seq)
    t_ff = min(1024, d_ff)
    t_scan = min(128, seq)
    assert seq % t_seq == 0 and seq % t_row == 0 and d_ff % LANES == 0 and seq % t_scan == 0
    return t_seq, t_row, t_ff, t_scan


def _params(*sem):
    return pltpu.CompilerParams(dimension_semantics=sem, vmem_limit_bytes=VMEM_LIMIT_BYTES)


def _const_spec(shape):
    zeros = (0,) * len(shape)
    return pl.BlockSpec(shape, lambda *_: zeros, pipeline_mode=pl.Buffered(1))


def _layer_norm(v, g, b):
    mu = jnp.mean(v, axis=-1, keepdims=True)
    vc = v - mu
    var = jnp.mean(vc * vc, axis=-1, keepdims=True)
    return vc * lax.rsqrt(var + LN_EPS) * g + b


def _gelu(v):
    return 0.5 * v * (1.0 + lax.erf(v * (1.0 / math.sqrt(2.0))))


def _in_proj_kernel(x_ref, w_ref, u_ref, q_ref, k_ref, v_ref, qi_ref, kw_ref, *, n_batch, widths):
    b = pl.program_id(1)
    xb = x_ref[...].astype(BF16)
    ts = xb.shape[0]
    starts = [sum(widths[:n]) for n in range(len(widths))]
    proj = lambda n: jnp.dot(xb, w_ref[:, starts[n]:starts[n] + widths[n]],
                             preferred_element_type=F32)
    u = proj(0)
    for s in range(u.shape[1] // LANES):
        u_ref[s, pl.ds(b, ts, stride=n_batch), :] = u[:, s * LANES:(s + 1) * LANES]
    q_ref[...] = proj(1).astype(BF16)
    k_ref[...] = proj(2).astype(BF16)
    v_ref[...] = proj(3).T.astype(BF16)
    qi_ref[...] = proj(4).astype(BF16)
    kw_ref[...] = proj(5)


def _in_proj(x, w, widths, *, ts):
    n_batch, seq, d = x.shape
    w_u, w_q, w_k, w_v, w_qi, w_kw = widths
    n_slabs = w_u // LANES
    row = lambda wd: pl.BlockSpec((None, ts, wd), lambda i, b: (b, i, 0))
    out_shape = (
        jax.ShapeDtypeStruct((n_slabs, seq * n_batch, LANES), F32),
        jax.ShapeDtypeStruct((n_batch, seq, w_q), BF16),
        jax.ShapeDtypeStruct((n_batch, seq, w_k), BF16),
        jax.ShapeDtypeStruct((n_batch, seq // ts, w_v, ts), BF16),
        jax.ShapeDtypeStruct((n_batch, seq, w_qi), BF16),
        jax.ShapeDtypeStruct((n_batch, seq, w_kw), F32),
    )
    return pl.pallas_call(
        functools.partial(_in_proj_kernel, n_batch=n_batch, widths=widths),
        grid=(seq // ts, n_batch),
        in_specs=[row(d), _const_spec(w.shape)],
        out_specs=(
            pl.BlockSpec((n_slabs, ts * n_batch, LANES), lambda i, b: (0, i, 0)),
            row(w_q), row(w_k),
            pl.BlockSpec((None, None, w_v, ts), lambda i, b: (b, i, 0, 0)),
            row(w_qi), row(w_kw),
        ),
        out_shape=out_shape,
        compiler_params=_params("arbitrary", "arbitrary"),
        name="in_proj",
    )(x, w)


def _s5_discretize_kernel(a_re_ref, a_im_ref, log_dt_ref, bt_re_ref, bt_im_ref,
                          abar_re_ref, abar_im_ref, bbar_re_ref, bbar_im_ref):
    a_re = a_re_ref[...]
    a_im = a_im_ref[...]
    dt = jnp.exp(log_dt_ref[...])
    mag = jnp.exp(dt * a_re)
    ang = dt * a_im
    abar_re = mag * jnp.cos(ang)
    abar_im = mag * jnp.sin(ang)
    num_re = abar_re - 1.0
    num_im = abar_im
    den = a_re * a_re + a_im * a_im
    f_re = (num_re * a_re + num_im * a_im) / den
    f_im = (num_im * a_re - num_re * a_im) / den
    abar_re_ref[...] = abar_re
    abar_im_ref[...] = abar_im
    bt_re = bt_re_ref[...]
    bt_im = bt_im_ref[...]
    bbar_re_ref[...] = f_re * bt_re - f_im * bt_im
    bbar_im_ref[...] = f_re * bt_im + f_im * bt_re


def _s5_discretize(a_re, a_im, log_dt, bt_re, bt_im):
    g, p = a_re.shape
    abar_re, abar_im, bbar_re, bbar_im = pl.pallas_call(
        _s5_discretize_kernel,
        out_shape=(jax.ShapeDtypeStruct((g, 1, p), F32), jax.ShapeDtypeStruct((g, 1, p), F32),
                   jax.ShapeDtypeStruct(bt_re.shape, F32), jax.ShapeDtypeStruct(bt_re.shape, F32)),
        name="s5_discretize",
    )(a_re.reshape(g, 1, p), a_im.reshape(g, 1, p), log_dt.reshape(g, 1, 1), bt_re, bt_im)
    return abar_re.reshape(g, p), abar_im.reshape(g, p), bbar_re, bbar_im


def _block_diag_slabs(m):
    n, gl, a, b = m.shape
    eye = jnp.eye(gl, dtype=m.dtype)
    full = m[:, :, :, None, :] * eye[None, :, None, :, None]
    return full.reshape(n, gl * a, gl * b)


def _s5_scan_kernel(u_next_ref, u_prev_ref, bbd_ref, ar_ref, ai_ref, cbd_ref, d_ref, y_ref,
                    buf_a, buf_b, carry_ref, *, ts, n_batch, tiles_per_slab, n_tiles):
    n = pl.program_id(0)
    last = n_tiles - 1
    cur = jnp.minimum(n, last)
    slab_next = jnp.minimum(n + 1, last) // tiles_per_slab
    slab_cur = cur // tiles_per_slab
    slab_prev = jnp.maximum(n - 1, 0) // tiles_per_slab
    half = ar_ref.shape[-1]

    @pl.when(n == 0)
    def _():
        buf_a[...] = jnp.dot(u_prev_ref[...].astype(BF16), bbd_ref[0], preferred_element_type=F32)
        buf_b[...] = jnp.zeros_like(buf_b)
        carry_ref[...] = jnp.zeros_like(carry_ref)

    def stages(cur_buf, other_buf):
        y = jnp.dot(other_buf[...].astype(BF16), cbd_ref[slab_prev], preferred_element_type=F32)
        y_ref[...] = _gelu(y + d_ref[slab_prev] * u_prev_ref[...])
        other_buf[...] = jnp.dot(u_next_ref[...].astype(BF16), bbd_ref[slab_next],
                                 preferred_element_type=F32)
        ar = ar_ref[slab_cur]
        ai = ai_ref[slab_cur]
        first = cur % tiles_per_slab == 0
        re = jnp.where(first, 0.0, carry_ref[:, :half])
        im = jnp.where(first, 0.0, carry_ref[:, half:])
        for t in range(ts):
            rows = slice(t * n_batch, (t + 1) * n_batch)
            bu = cur_buf[rows, :]
            re, im = ar * re - ai * im + bu[:, :half], ar * im + ai * re + bu[:, half:]
            cur_buf[rows, :] = jnp.concatenate([re, im], axis=-1)
        carry_ref[...] = jnp.concatenate([re, im], axis=-1)

    @pl.when(n % 2 == 0)
    def _():
        stages(buf_a, buf_b)

    @pl.when(n % 2 == 1)
    def _():
        stages(buf_b, buf_a)


def _s5_scan(u_tm, bbd, ar, ai, cbd, d, *, ts, n_batch):
    n_slabs, rows, _ = u_tm.shape
    tr = ts * n_batch
    tiles_per_slab = rows // tr
    n_tiles = n_slabs * tiles_per_slab
    n_state2 = bbd.shape[-1]
    tile = lambda shift: pl.BlockSpec(
        (tr, LANES), lambda n: (jnp.clip(n + shift, 0, n_tiles - 1), 0))
    u_flat = u_tm.reshape(n_slabs * rows, LANES)
    y = pl.pallas_call(
        functools.partial(_s5_scan_kernel, ts=ts, n_batch=n_batch,
                          tiles_per_slab=tiles_per_slab, n_tiles=n_tiles),
        grid=(n_tiles + 1,),
        in_specs=[tile(1), tile(-1), _const_spec(bbd.shape), _const_spec(ar.shape),
                  _const_spec(ai.shape), _const_spec(cbd.shape), _const_spec(d.shape)],
        out_specs=tile(-1),
        out_shape=jax.ShapeDtypeStruct(u_flat.shape, F32),
        scratch_shapes=[pltpu.VMEM((tr, n_state2), F32), pltpu.VMEM((tr, n_state2), F32),
                        pltpu.VMEM((n_batch, n_state2), F32)],
        compiler_params=_params("arbitrary"),
        name="s5_scan",
    )(u_flat, u_flat, bbd, ar, ai, cbd, d)
    return y.reshape(u_tm.shape)


def _glu_kernel(y_ref, w_ref, b_ref, o_ref, *, ts, n_batch):
    n_slabs = y_ref.shape[0]
    w = w_ref[...]
    bias = b_ref[...]
    for b in range(n_batch):
        yb = jnp.concatenate(
            [y_ref[s, pl.ds(b, ts, stride=n_batch), :] for s in range(n_slabs)], axis=-1)
        z = jnp.dot(yb.astype(BF16), w, preferred_element_type=F32) + bias
        o_ref[b] = (yb * jax.nn.sigmoid(z)).astype(BF16)


def _glu(y_tm, w_glu, b_glu, *, ts, n_batch):
    n_slabs, rows, _ = y_tm.shape
    seq = rows // n_batch
    width = n_slabs * LANES
    return pl.pallas_call(
        functools.partial(_glu_kernel, ts=ts, n_batch=n_batch),
        grid=(seq // ts,),
        in_specs=[
            pl.BlockSpec((n_slabs, ts * n_batch, LANES), lambda i: (0, i, 0)),
            _const_spec(w_glu.shape),
            _const_spec((1, width)),
        ],
        out_specs=pl.BlockSpec((n_batch, ts, width), lambda i: (0, i, 0)),
        out_shape=jax.ShapeDtypeStruct((n_batch, seq, width), BF16),
        compiler_params=_params("arbitrary"),
        name="glu",
    )(y_tm, w_glu, b_glu.reshape(1, width))


def _attn_kernel(qi_ref, kwq_ref, kwf_ref, q_ref, k_ref, vt_ref, o_ref,
                 sc_ref, lg_ref, bias_ref, q4_ref, mx_ref, m_ref, ls_ref, ot_ref,
                 *, tq, n_keep, idx_w_scale, qk_scale):
    i = pl.program_id(1)
    n_chunks = i + 1
    nt = (((1,), (1,)), ((), ()))
    fold = lambda a: a.reshape(tq // ACC_ROWS, ACC_ROWS, tq)

    key_pos = lax.broadcasted_iota(I32, (tq, tq), 0)
    qry_pos = lax.broadcasted_iota(I32, (tq, tq), 1)
    wi_t = kwq_ref[...].T[IDX_DIM:IDX_DIM + N_IDX_HEADS, :] * idx_w_scale

    def score_chunk(j):
        off = pl.multiple_of(j * tq, tq)
        kic = kwf_ref[pl.ds(off, tq), :][:, :IDX_DIM].astype(BF16)
        s = jnp.zeros((tq, tq), F32)
        for h in range(N_IDX_HEADS):
            r = lax.dot_general(kic, qi_ref[:, h * IDX_DIM:(h + 1) * IDX_DIM], nt,
                                preferred_element_type=F32)
            s = s + wi_t[h:h + 1, :] * jnp.maximum(r, 0.0)
        causal = (j - i) * tq + key_pos <= qry_pos
        sc_ref[j] = jnp.where(causal, s, -jnp.inf)

    _for_each_chunk(n_chunks, score_chunk)

    def decode(key):
        return pltpu.bitcast(jnp.where(key < 0, key ^ jnp.int32(0x7FFFFFFF), key), F32)

    def count_ge(thr, strict=False):
        thr_b = jnp.broadcast_to(thr, (ACC_ROWS, tq))

        def body(j, acc):
            s = fold(sc_ref[j])
            hit = s > thr_b if strict else s >= thr_b
            return acc + jnp.sum(jnp.where(hit, 1.0, 0.0), axis=0)

        acc = lax.fori_loop(0, n_chunks, body, jnp.zeros((ACC_ROWS, tq), F32))
        return jnp.sum(acc, axis=0, keepdims=True)

    keep = jnp.float32(n_keep)
    cnt0 = count_ge(jnp.zeros((1, tq), F32))
    tau = jnp.where(cnt0 >= keep, jnp.int32(0), jnp.int32(INT_MIN))
    cnt = jnp.where(cnt0 >= keep, cnt0, 0.0)

    def bit_body(bi, carry):
        tau, cnt = carry
        cand = tau + jnp.left_shift(jnp.int32(1), jnp.int32(30) - bi)
        c = count_ge(decode(cand))
        return jnp.where(c >= keep, cand, tau), jnp.where(c >= keep, c, cnt)

    tau, cnt = lax.fori_loop(0, 31, bit_body, (tau, cnt))
    thr = jnp.where(tau == jnp.int32(INT_MIN), jnp.finfo(F32).min, decode(tau))

    @pl.when(jnp.max(cnt) > keep)
    def _():
        need = keep - count_ge(thr, strict=True)
        lower = jnp.where(key_pos >= qry_pos, 1.0, 0.0).astype(BF16)

        def tie_chunk(j, run):
            s = sc_ref[j]
            eq = s == thr
            seen = run + jnp.dot(lower, jnp.where(eq, 1.0, 0.0).astype(BF16),
                                 preferred_element_type=F32)
            sc_ref[j] = jnp.where(eq, jnp.where(seen > need, -jnp.inf, s), s)
            return seen[tq - 1:tq, :]

        lax.fori_loop(0, n_chunks, tie_chunk, jnp.zeros((1, tq), F32))

    def mask_chunk(j, carry):
        sc_ref[j] = jnp.where(sc_ref[j] >= thr, 0.0, -1e30)
        return carry

    lax.fori_loop(0, n_chunks, mask_chunk, 0)

    slopes = [2.0 ** (-8.0 * (hd + 1) / N_HEADS) for hd in range(N_HEADS)]
    for hd in range(N_HEADS):
        c, g = divmod(hd, GQA_GROUP)
        q4_ref[c, g * tq:(g + 1) * tq, :] = q_ref[:, hd * HEAD_DIM:(hd + 1) * HEAD_DIM]
        bias_ref[hd] = slopes[hd] * key_pos.astype(F32)
    mx_ref[...] = jnp.full(mx_ref.shape, -jnp.inf, F32)
    ls_ref[...] = jnp.zeros_like(ls_ref)
    ot_ref[...] = jnp.zeros_like(ot_ref)

    def chunk_shift(j, hd):
        return ((j - i) * tq).astype(F32) * slopes[hd]

    def logits_chunk(j):
        off = pl.multiple_of(j * tq, tq)
        mask = sc_ref[j]
        for c in range(N_KV_HEADS):
            kc = k_ref[pl.ds(off, tq), :][:, c * HEAD_DIM:(c + 1) * HEAD_DIM]
            lg4 = lax.dot_general(kc, q4_ref[c], nt, preferred_element_type=F32) * qk_scale
            for g in range(GQA_GROUP):
                hd = c * GQA_GROUP + g
                lg = lg4[:, g * tq:(g + 1) * tq] + bias_ref[hd] + mask
                lg_ref[j, hd] = lg
                mx_ref[hd] = jnp.maximum(mx_ref[hd], jnp.max(fold(lg), axis=0) + chunk_shift(j, hd))

    _for_each_chunk(n_chunks, logits_chunk)
    for hd in range(N_HEADS):
        m_ref[hd] = jnp.max(mx_ref[hd], axis=0, keepdims=True)

    def pv_chunk(j):
        for c in range(N_KV_HEADS):
            vt = vt_ref[j, c * HEAD_DIM:(c + 1) * HEAD_DIM, :]
            for g in range(GQA_GROUP):
                hd = c * GQA_GROUP + g
                p = jnp.exp(lg_ref[j, hd] - (m_ref[hd] - chunk_shift(j, hd)))
                ls_ref[hd] += jnp.sum(fold(p), axis=0)
                ot_ref[hd] += jnp.dot(vt, p.astype(BF16), preferred_element_type=F32)

    _for_each_chunk(n_chunks, pv_chunk)
    for hd in range(N_HEADS):
        denom = jnp.sum(ls_ref[hd], axis=0, keepdims=True)
        o_ref[:, hd * HEAD_DIM:(hd + 1) * HEAD_DIM] = (ot_ref[hd] / denom).T.astype(BF16)


def _for_each_chunk(n_chunks, fn):
    def pair(p, carry):
        fn(2 * p)
        fn(2 * p + 1)
        return carry

    lax.fori_loop(0, n_chunks // 2, pair, 0)

    @pl.when(n_chunks % 2 == 1)
    def _():
        fn(n_chunks - 1)


def _attention(qi, kw, q, k, vt):
    n_batch, seq, _ = q.shape
    n_chunks, kv_w, tq = vt.shape[1:]
    n_keep = min(INDEX_TOPK, seq // 4)
    blk = lambda w: pl.BlockSpec((None, tq, w), lambda b, i: (b, i, 0))
    whole = lambda w: pl.BlockSpec((None, seq, w), lambda b, i: (b, 0, 0))
    return pl.pallas_call(
        functools.partial(_attn_kernel, tq=tq, n_keep=n_keep,
                          idx_w_scale=(N_IDX_HEADS ** -0.5) * (IDX_DIM ** -0.5),
                          qk_scale=HEAD_DIM ** -0.5),
        grid=(n_batch, n_chunks),
        in_specs=[blk(qi.shape[2]), blk(kw.shape[2]), whole(kw.shape[2]), blk(q.shape[2]),
                  whole(k.shape[2]),
                  pl.BlockSpec((None, n_chunks, kv_w, tq), lambda b, i: (b, 0, 0, 0))],
        out_specs=blk(q.shape[2]),
        out_shape=jax.ShapeDtypeStruct(q.shape, BF16),
        scratch_shapes=[pltpu.VMEM((n_chunks, tq, tq), F32),
                        pltpu.VMEM((n_chunks, N_HEADS, tq, tq), F32),
                        pltpu.VMEM((N_HEADS, tq, tq), F32),
                        pltpu.VMEM((N_KV_HEADS, GQA_GROUP * tq, HEAD_DIM), BF16),
                        pltpu.VMEM((N_HEADS, ACC_ROWS, tq), F32),
                        pltpu.VMEM((N_HEADS, 1, tq), F32),
                        pltpu.VMEM((N_HEADS, ACC_ROWS, tq), F32),
                        pltpu.VMEM((N_HEADS, HEAD_DIM, tq), F32)],
        compiler_params=_params("arbitrary", "arbitrary"),
        name="sparse_attn",
    )(qi, kw, kw, q, k, vt)


def _out_proj_kernel(ys_ref, ya_ref, x_ref, w_ref, g_ref, b_ref, o_ref):
    ssm_w = ys_ref.shape[1]
    for r in range(0, x_ref.shape[0], LN_ROWS):
        rows = slice(r, r + LN_ROWS)
        mix = jnp.dot(ys_ref[rows, :], w_ref[:ssm_w, :], preferred_element_type=F32)
        mix = mix + jnp.dot(ya_ref[rows, :], w_ref[ssm_w:, :], preferred_element_type=F32)
        o_ref[rows, :] = _layer_norm(DEEPNORM_ALPHA * x_ref[rows, :] + mix, g_ref[...], b_ref[...])


def _out_proj(ys, ya, x, w, g, b, *, tm):
    m, d = x.shape
    blk = lambda wd: pl.BlockSpec((tm, wd), lambda i: (i, 0))
    return pl.pallas_call(
        _out_proj_kernel,
        grid=(m // tm,),
        in_specs=[blk(ys.shape[1]), blk(ya.shape[1]), blk(d), _const_spec(w.shape),
                  _const_spec((1, d)), _const_spec((1, d))],
        out_specs=blk(d),
        out_shape=jax.ShapeDtypeStruct((m, d), F32),
        compiler_params=_params("arbitrary"),
        name="out_proj_ln1",
    )(ys, ya, x, w, g.reshape(1, d), b.reshape(1, d))


def _ffn_kernel(h_ref, halo_ref, wup_ref, wgate_ref, wdown_ref, cw_ref, cb_ref, g_ref, b_ref,
                o_ref, hb_ref, acc_ref, *, last_width):
    i = pl.program_id(1)
    j = pl.program_id(2)
    tm = h_ref.shape[0]
    tf = wup_ref.shape[1]

    @pl.when(j == 0)
    def _():
        hb_ref[:HALO_ROWS, :] = jnp.where(i == 0, 0.0, halo_ref[...]).astype(BF16)
        hb_ref[HALO_ROWS:, :] = h_ref[...].astype(BF16)
        acc_ref[...] = jnp.zeros_like(acc_ref)

    def chunk(width):
        ext = jnp.dot(hb_ref[...], wup_ref[:, :width], preferred_element_type=F32)
        cw = cw_ref[:, :width]
        hc = cb_ref[:, :width]
        for lag in range(CONV_WIDTH):
            start = HALO_ROWS - lag
            hc = hc + cw[CONV_WIDTH - 1 - lag:CONV_WIDTH - lag, :] * ext[start:start + tm, :]
        gate = jnp.dot(hb_ref[HALO_ROWS:, :], wgate_ref[:, :width], preferred_element_type=F32)
        act = (_gelu(hc) * gate).astype(BF16)
        acc_ref[...] += jnp.dot(act, wdown_ref[:width, :], preferred_element_type=F32)

    is_last = j == pl.num_programs(2) - 1
    if last_width == tf:
        chunk(tf)
    else:
        pl.when(jnp.logical_not(is_last))(functools.partial(chunk, tf))
        pl.when(is_last)(functools.partial(chunk, last_width))

    @pl.when(is_last)
    def _():
        o_ref[...] = _layer_norm(DEEPNORM_ALPHA * h_ref[...] + acc_ref[...], g_ref[...], b_ref[...])


def _ffn(h, w_up, w_gate, w_down, conv_w, conv_b, g, b, *, tm, tf):
    n_batch, seq, d = h.shape
    f = w_up.shape[1]
    n_chunks = -(-f // tf)
    pad = n_chunks * tf - f
    w_up, w_gate, conv_w = (jnp.pad(a, ((0, 0), (0, pad))) for a in (w_up, w_gate, conv_w))
    conv_b = jnp.pad(conv_b, (0, pad))
    w_down = jnp.pad(w_down, ((0, pad), (0, 0)))
    f = n_chunks * tf
    halo_blocks = tm // HALO_ROWS
    return pl.pallas_call(
        functools.partial(_ffn_kernel, last_width=tf - pad),
        grid=(n_batch, seq // tm, n_chunks),
        in_specs=[
            pl.BlockSpec((None, tm, d), lambda bb, i, j: (bb, i, 0)),
            pl.BlockSpec((None, HALO_ROWS, d),
                         lambda bb, i, j: (bb, jnp.maximum(i * halo_blocks - 1, 0), 0)),
            pl.BlockSpec((d, tf), lambda bb, i, j: (0, j)),
            pl.BlockSpec((d, tf), lambda bb, i, j: (0, j)),
            pl.BlockSpec((tf, d), lambda bb, i, j: (j, 0)),
            pl.BlockSpec((CONV_WIDTH, tf), lambda bb, i, j: (0, j)),
            pl.BlockSpec((1, tf), lambda bb, i, j: (0, j)),
            _const_spec((1, d)),
            _const_spec((1, d)),
        ],
        out_specs=pl.BlockSpec((None, tm, d), lambda bb, i, j: (bb, i, 0)),
        out_shape=jax.ShapeDtypeStruct((n_batch, seq, d), F32),
        scratch_shapes=[pltpu.VMEM((HALO_ROWS + tm, d), BF16), pltpu.VMEM((tm, d), F32)],
        compiler_params=_params("arbitrary", "arbitrary", "arbitrary"),
        name="ffn_ln2",
    )(h, h, w_up, w_gate, w_down, conv_w, conv_b.reshape(1, f), g.reshape(1, d), b.reshape(1, d))


def _layer(h, w_in, a_re, a_im, log_dt, b_re, b_im, c_re, c_im, ssm_d, w_glu, b_glu, w_out,
           ln1_g, ln1_b, w_up, w_gate, conv_w, conv_b, w_down, ln2_g, ln2_b):
    n_batch, seq, d = h.shape
    assert n_batch == SUBLANES, "the time-major scan layout puts the batch on the sublanes"
    n_groups, n_state = a_re.shape
    ssm_w = n_groups * SSM_GROUP_CH
    attn_w = N_HEADS * HEAD_DIM
    kv_w = N_KV_HEADS * HEAD_DIM
    qi_w = N_IDX_HEADS * IDX_DIM
    n_slabs = ssm_w // LANES
    assert n_state == SSM_STATE and ssm_w % LANES == 0
    t_seq, t_row, t_ff, t_scan = _tiles(seq, w_up.shape[1])

    widths = (ssm_w, attn_w, kv_w, kv_w, qi_w, LANES)
    assert w_in.shape[1] == sum(widths[:-1]) + IDX_DIM + N_IDX_HEADS
    w_in_b = jnp.pad(w_in, ((0, 0), (0, sum(widths) - w_in.shape[1]))).astype(BF16)
    u_tm, q, k, vt, qi, kw = _in_proj(h, w_in_b, widths, ts=t_seq)

    abar_re, abar_im, bbar_re, bbar_im = _s5_discretize(
        a_re, a_im, log_dt, jnp.swapaxes(b_re, 1, 2), jnp.swapaxes(b_im, 1, 2))
    gl = GROUPS_PER_SLAB
    to_slabs = lambda m: m.reshape(n_slabs, gl, *m.shape[1:])
    bbd = jnp.concatenate([_block_diag_slabs(to_slabs(bbar_re)),
                           _block_diag_slabs(to_slabs(bbar_im))], axis=-1).astype(BF16)
    cbd_re = _block_diag_slabs(to_slabs(jnp.swapaxes(c_re, 1, 2)))
    cbd_im = _block_diag_slabs(to_slabs(jnp.swapaxes(c_im, 1, 2)))
    cbd = jnp.concatenate([cbd_re, -cbd_im], axis=1).astype(BF16)
    bcast = lambda m: jnp.broadcast_to(m.reshape(n_slabs, 1, gl * n_state), (n_slabs, n_batch, gl * n_state))
    y_tm = _s5_scan(u_tm, bbd, bcast(abar_re), bcast(abar_im), cbd,
                    ssm_d.reshape(n_slabs, 1, LANES), ts=t_scan, n_batch=n_batch)
    y_ssm = _glu(y_tm, w_glu.astype(BF16), b_glu, ts=t_seq, n_batch=n_batch)

    y_attn = _attention(qi, kw, q, k, vt)

    h1 = _out_proj(y_ssm.reshape(n_batch * seq, ssm_w), y_attn.reshape(n_batch * seq, attn_w),
                   h.reshape(n_batch * seq, d), w_out.astype(BF16), ln1_g, ln1_b, tm=t_row)
    return _ffn(h1.reshape(n_batch, seq, d), w_up.astype(BF16), w_gate.astype(BF16),
                w_down.astype(BF16), conv_w, conv_b, ln2_g, ln2_b, tm=t_row, tf=t_ff)


def kernel(x, w_in, ssm_a_re, ssm_a_im, ssm_log_dt, ssm_b_re, ssm_b_im, ssm_c_re, ssm_c_im, ssm_d,
           w_glu, b_glu, w_out, ln1_g, ln1_b, w_up, w_gate, conv_w, conv_b, w_down, ln2_g, ln2_b):
    h = x
    for l in range(w_in.shape[0]):
        h = _layer(h, w_in[l], ssm_a_re[l], ssm_a_im[l], ssm_log_dt[l], ssm_b_re[l], ssm_b_im[l],
                   ssm_c_re[l], ssm_c_im[l], ssm_d[l], w_glu[l], b_glu[l], w_out[l], ln1_g[l],
                   ln1_b[l], w_up[l], w_gate[l], conv_w[l], conv_b[l], w_down[l], ln2_g[l], ln2_b[l])
    return h
```

```python
import functools
import math

import jax
import jax.numpy as jnp
from jax import lax
from jax.experimental import pallas as pl
from jax.experimental.pallas import tpu as pltpu

F32 = jnp.float32
BF16 = jnp.bfloat16
I32 = jnp.int32

LANES = 128
SUBLANES = 8
VMEM_LIMIT_BYTES = 56 * 1024 * 1024

SSM_GROUP_CH = 16
SSM_STATE = 64
N_HEADS = 8
HEAD_DIM = 128
N_KV_HEADS = 2
GQA_GROUP = N_HEADS // N_KV_HEADS
N_IDX_HEADS = 16
IDX_DIM = 64
INDEX_TOPK = 256
CONV_WIDTH = 3
LN_EPS = 1e-5
DEPTH = 1
DEEPNORM_ALPHA = (2.0 * DEPTH) ** 0.25

GROUPS_PER_SLAB = LANES // SSM_GROUP_CH
SLAB_STATE = GROUPS_PER_SLAB * SSM_STATE
INT_MIN = -(2 ** 31)
ACC_ROWS = 4 * SUBLANES
LN_ROWS = 128
HALO_ROWS = 2 * SUBLANES


def _tiles(seq, d_ff):
    t_seq = min(256, seq)
    t_row = min(512, seq)
    t_ff = min(1024, d_ff)
    t_scan = min(128, seq)
    assert seq % t_seq == 0 and seq % t_row == 0 and d_ff % LANES == 0 and seq % t_scan == 0
    return t_seq, t_row, t_ff, t_scan


def _params(*sem):
    return pltpu.CompilerParams(dimension_semantics=sem, vmem_limit_bytes=VMEM_LIMIT_BYTES)


def _const_spec(shape):
    zeros = (0,) * len(shape)
    return pl.BlockSpec(shape, lambda *_: zeros, pipeline_mode=pl.Buffered(1))


def _layer_norm(v, g, b):
    mu = jnp.mean(v, axis=-1, keepdims=True)
    vc = v - mu
    var = jnp.mean(vc * vc, axis=-1, keepdims=True)
    return vc * lax.rsqrt(var + LN_EPS) * g + b


def _gelu(v):
    return 0.5 * v * (1.0 + lax.erf(v * (1.0 / math.sqrt(2.0))))


def _in_proj_kernel(x_ref, w_ref, u_ref, q_ref, k_ref, v_ref, qi_ref, kw_ref, *, n_batch, widths):
    b = pl.program_id(1)
    xb = x_ref[...].astype(BF16)
    ts = xb.shape[0]
    starts = [sum(widths[:n]) for n in range(len(widths))]
    proj = lambda n: jnp.dot(xb, w_ref[:, starts[n]:starts[n] + widths[n]],
                             preferred_element_type=F32)
    u = proj(0)
    for s in range(u.shape[1] // LANES):
        u_ref[s, pl.ds(b, ts, stride=n_batch), :] = u[:, s * LANES:(s + 1) * LANES]
    q_ref[...] = proj(1).astype(BF16)
    k_ref[...] = proj(2).astype(BF16)
    v_ref[...] = proj(3).T.astype(BF16)
    qi_ref[...] = proj(4).astype(BF16)
    kw_ref[...] = proj(5)


def _in_proj(x, w, widths, *, ts):
    n_batch, seq, d = x.shape
    w_u, w_q, w_k, w_v, w_qi, w_kw = widths
    n_slabs = w_u // LANES
    row = lambda wd: pl.BlockSpec((None, ts, wd), lambda i, b: (b, i, 0))
    out_shape = (
        jax.ShapeDtypeStruct((n_slabs, seq * n_batch, LANES), F32),
        jax.ShapeDtypeStruct((n_batch, seq, w_q), BF16),
        jax.ShapeDtypeStruct((n_batch, seq, w_k), BF16),
        jax.ShapeDtypeStruct((n_batch, seq // ts, w_v, ts), BF16),
        jax.ShapeDtypeStruct((n_batch, seq, w_qi), BF16),
        jax.ShapeDtypeStruct((n_batch, seq, w_kw), F32),
    )
    return pl.pallas_call(
        functools.partial(_in_proj_kernel, n_batch=n_batch, widths=widths),
        grid=(seq // ts, n_batch),
        in_specs=[row(d), _const_spec(w.shape)],
        out_specs=(
            pl.BlockSpec((n_slabs, ts * n_batch, LANES), lambda i, b: (0, i, 0)),
            row(w_q), row(w_k),
            pl.BlockSpec((None, None, w_v, ts), lambda i, b: (b, i, 0, 0)),
            row(w_qi), row(w_kw),
        ),
        out_shape=out_shape,
        compiler_params=_params("arbitrary", "arbitrary"),
        name="in_proj",
    )(x, w)


def _s5_discretize_kernel(a_re_ref, a_im_ref, log_dt_ref, bt_re_ref, bt_im_ref,
                          abar_re_ref, abar_im_ref, bbar_re_ref, bbar_im_ref):
    a_re = a_re_ref[...]
    a_im = a_im_ref[...]
    dt = jnp.exp(log_dt_ref[...])
    mag = jnp.exp(dt * a_re)
    ang = dt * a_im
    abar_re = mag * jnp.cos(ang)
    abar_im = mag * jnp.sin(ang)
    num_re = abar_re - 1.0
    num_im = abar_im
    den = a_re * a_re + a_im * a_im
    f_re = (num_re * a_re + num_im * a_im) / den
    f_im = (num_im * a_re - num_re * a_im) / den
    abar_re_ref[...] = abar_re
    abar_im_ref[...] = abar_im
    bt_re = bt_re_ref[...]
    bt_im = bt_im_ref[...]
    bbar_re_ref[...] = f_re * bt_re - f_im * bt_im
    bbar_im_ref[...] = f_re * bt_im + f_im * bt_re


def _s5_discretize(a_re, a_im, log_dt, bt_re, bt_im):
    g, p = a_re.shape
    abar_re, abar_im, bbar_re, bbar_im = pl.pallas_call(
        _s5_discretize_kernel,
        out_shape=(jax.ShapeDtypeStruct((g, 1, p), F32), jax.ShapeDtypeStruct((g, 1, p), F32),
                   jax.ShapeDtypeStruct(bt_re.shape, F32), jax.ShapeDtypeStruct(bt_re.shape, F32)),
        name="s5_discretize",
    )(a_re.reshape(g, 1, p), a_im.reshape(g, 1, p), log_dt.reshape(g, 1, 1), bt_re, bt_im)
    return abar_re.reshape(g, p), abar_im.reshape(g, p), bbar_re, bbar_im


def _block_diag_slabs(m):
    n, gl, a, b = m.shape
    eye = jnp.eye(gl, dtype=m.dtype)
    full = m[:, :, :, None, :] * eye[None, :, None, :, None]
    return full.reshape(n, gl * a, gl * b)


def _s5_scan_kernel(u_next_ref, u_prev_ref, bbd_ref, ar_ref, ai_ref, cbd_ref, d_ref, y_ref,
                    buf_a, buf_b, carry_ref, *, ts, n_batch, tiles_per_slab, n_tiles):
    n = pl.program_id(0)
    last = n_tiles - 1
    cur = jnp.minimum(n, last)
    slab_next = jnp.minimum(n + 1, last) // tiles_per_slab
    slab_cur = cur // tiles_per_slab
    slab_prev = jnp.maximum(n - 1, 0) // tiles_per_slab
    half = ar_ref.shape[-1]

    @pl.when(n == 0)
    def _():
        buf_a[...] = jnp.dot(u_prev_ref[...].astype(BF16), bbd_ref[0], preferred_element_type=F32)
        buf_b[...] = jnp.zeros_like(buf_b)
        carry_ref[...] = jnp.zeros_like(carry_ref)

    def stages(cur_buf, other_buf):
        y = jnp.dot(other_buf[...].astype(BF16), cbd_ref[slab_prev], preferred_element_type=F32)
        y_ref[...] = _gelu(y + d_ref[slab_prev] * u_prev_ref[...])
        other_buf[...] = jnp.dot(u_next_ref[...].astype(BF16), bbd_ref[slab_next],
                                 preferred_element_type=F32)
        ar = ar_ref[slab_cur]
        ai = ai_ref[slab_cur]
        first = cur % tiles_per_slab == 0
        re = jnp.where(first, 0.0, carry_ref[:, :half])
        im = jnp.where(first, 0.0, carry_ref[:, half:])
        for t in range(ts):
            rows = slice(t * n_batch, (t + 1) * n_batch)
            bu = cur_buf[rows, :]
            re, im = ar * re - ai * im + bu[:, :half], ar * im + ai * re + bu[:, half:]
            cur_buf[rows, :] = jnp.concatenate([re, im], axis=-1)
        carry_ref[...] = jnp.concatenate([re, im], axis=-1)

    @pl.when(n % 2 == 0)
    def _():
        stages(buf_a, buf_b)

    @pl.when(n % 2 == 1)
    def _():
        stages(buf_b, buf_a)


def _s5_scan(u_tm, bbd, ar, ai, cbd, d, *, ts, n_batch):
    n_slabs, rows, _ = u_tm.shape
    tr = ts * n_batch
    tiles_per_slab = rows // tr
    n_tiles = n_slabs * tiles_per_slab
    n_state2 = bbd.shape[-1]
    tile = lambda shift: pl.BlockSpec(
        (tr, LANES), lambda n: (jnp.clip(n + shift, 0, n_tiles - 1), 0))
    u_flat = u_tm.reshape(n_slabs * rows, LANES)
    y = pl.pallas_call(
        functools.partial(_s5_scan_kernel, ts=ts, n_batch=n_batch,
                          tiles_per_slab=tiles_per_slab, n_tiles=n_tiles),
        grid=(n_tiles + 1,),
        in_specs=[tile(1), tile(-1), _const_spec(bbd.shape), _const_spec(ar.shape),
                  _const_spec(ai.shape), _const_spec(cbd.shape), _const_spec(d.shape)],
        out_specs=tile(-1),
        out_shape=jax.ShapeDtypeStruct(u_flat.shape, F32),
        scratch_shapes=[pltpu.VMEM((tr, n_state2), F32), pltpu.VMEM((tr, n_state2), F32),
                        pltpu.VMEM((n_batch, n_state2), F32)],
        compiler_params=_params("arbitrary"),
        name="s5_scan",
    )(u_flat, u_flat, bbd, ar, ai, cbd, d)
    return y.reshape(u_tm.shape)


def _glu_kernel(y_ref, w_ref, b_ref, o_ref, *, ts, n_batch):
    n_slabs = y_ref.shape[0]
    w = w_ref[...]
    bias = b_ref[...]
    for b in range(n_batch):
        yb = jnp.concatenate(
            [y_ref[s, pl.ds(b, ts, stride=n_batch), :] for s in range(n_slabs)], axis=-1)
        z = jnp.dot(yb.astype(BF16), w, preferred_element_type=F32) + bias
        o_ref[b] = (yb * jax.nn.sigmoid(z)).astype(BF16)


def _glu(y_tm, w_glu, b_glu, *, ts, n_batch):
    n_slabs, rows, _ = y_tm.shape
    seq = rows // n_batch
    width = n_slabs * LANES
    return pl.pallas_call(
        functools.partial(_glu_kernel, ts=ts, n_batch=n_batch),
        grid=(seq // ts,),
        in_specs=[
            pl.BlockSpec((n_slabs, ts * n_batch, LANES), lambda i: (0, i, 0)),
            _const_spec(w_glu.shape),
            _const_spec((1, width)),
        ],
        out_specs=pl.BlockSpec((n_batch, ts, width), lambda i: (0, i, 0)),
        out_shape=jax.ShapeDtypeStruct((n_batch, seq, width), BF16),
        compiler_params=_params("arbitrary"),
        name="glu",
    )(y_tm, w_glu, b_glu.reshape(1, width))


def _attn_kernel(qi_ref, kwq_ref, kwf_ref, q_ref, k_ref, vt_ref, o_ref,
                 sc_ref, lg_ref, bias_ref, q4_ref, mx_ref, m_ref, ls_ref, ot_ref,
                 *, tq, n_keep, idx_w_scale, qk_scale):
    i = pl.program_id(1)
    n_chunks = i + 1
    nt = (((1,), (1,)), ((), ()))
    fold = lambda a: a.reshape(tq // ACC_ROWS, ACC_ROWS, tq)

    key_pos = lax.broadcasted_iota(I32, (tq, tq), 0)
    qry_pos = lax.broadcasted_iota(I32, (tq, tq), 1)
    wi_t = kwq_ref[...].T[IDX_DIM:IDX_DIM + N_IDX_HEADS, :] * idx_w_scale

    def score_chunk(j):
        off = pl.multiple_of(j * tq, tq)
        kic = kwf_ref[pl.ds(off, tq), :][:, :IDX_DIM].astype(BF16)
        s = jnp.zeros((tq, tq), F32)
        for h in range(N_IDX_HEADS):
            r = lax.dot_general(kic, qi_ref[:, h * IDX_DIM:(h + 1) * IDX_DIM], nt,
                                preferred_element_type=F32)
            s = s + wi_t[h:h + 1, :] * jnp.maximum(r, 0.0)
        causal = (j - i) * tq + key_pos <= qry_pos
        sc_ref[j] = jnp.where(causal, s, -jnp.inf)

    _for_each_chunk(n_chunks, score_chunk)

    def decode(key):
        return pltpu.bitcast(jnp.where(key < 0, key ^ jnp.int32(0x7FFFFFFF), key), F32)

    def count_ge(thr, strict=False):
        thr_b = jnp.broadcast_to(thr, (ACC_ROWS, tq))

        def body(j, acc):
            s = fold(sc_ref[j])
            hit = s > thr_b if strict else s >= thr_b
            return acc + jnp.sum(jnp.where(hit, 1.0, 0.0), axis=0)

        acc = lax.fori_loop(0, n_chunks, body, jnp.zeros((ACC_ROWS, tq), F32))
        return jnp.sum(acc, axis=0, keepdims=True)

    keep = jnp.float32(n_keep)
    cnt0 = count_ge(jnp.zeros((1, tq), F32))
    tau = jnp.where(cnt0 >= keep, jnp.int32(0), jnp.int32(INT_MIN))
    cnt = jnp.where(cnt0 >= keep, cnt0, 0.0)

    def bit_body(bi, carry):
        tau, cnt = carry
        cand = tau + jnp.left_shift(jnp.int32(1), jnp.int32(30) - bi)
        c = count_ge(decode(cand))
        return jnp.where(c >= keep, cand, tau), jnp.where(c >= keep, c, cnt)

    tau, cnt = lax.fori_loop(0, 31, bit_body, (tau, cnt))
    thr = jnp.where(tau == jnp.int32(INT_MIN), jnp.finfo(F32).min, decode(tau))

    @pl.when(jnp.max(cnt) > keep)
    def _():
        need = keep - count_ge(thr, strict=True)
        lower = jnp.where(key_pos >= qry_pos, 1.0, 0.0).astype(BF16)

        def tie_chunk(j, run):
            s = sc_ref[j]
            eq = s == thr
            seen = run + jnp.dot(lower, jnp.where(eq, 1.0, 0.0).astype(BF16),
                                 preferred_element_type=F32)
            sc_ref[j] = jnp.where(eq, jnp.where(seen > need, -jnp.inf, s), s)
            return seen[tq - 1:tq, :]

        lax.fori_loop(0, n_chunks, tie_chunk, jnp.zeros((1, tq), F32))

    def mask_chunk(j, carry):
        sc_ref[j] = jnp.where(sc_ref[j] >= thr, 0.0, -1e30)
        return carry

    lax.fori_loop(0, n_chunks, mask_chunk, 0)

    slopes = [2.0 ** (-8.0 * (hd + 1) / N_HEADS) for hd in range(N_HEADS)]
    for hd in range(N_HEADS):
        c, g = divmod(hd, GQA_GROUP)
        q4_ref[c, g * tq:(g + 1) * tq, :] = q_ref[:, hd * HEAD_DIM:(hd + 1) * HEAD_DIM]
        bias_ref[hd] = slopes[hd] * key_pos.astype(F32)
    mx_ref[...] = jnp.full(mx_ref.shape, -jnp.inf, F32)
    ls_ref[...] = jnp.zeros_like(ls_ref)
    ot_ref[...] = jnp.zeros_like(ot_ref)

    def chunk_shift(j, hd):
        return ((j - i) * tq).astype(F32) * slopes[hd]

    def logits_chunk(j):
        off = pl.multiple_of(j * tq, tq)
        mask = sc_ref[j]
        for c in range(N_KV_HEADS):
            kc = k_ref[pl.ds(off, tq), :][:, c * HEAD_DIM:(c + 1) * HEAD_DIM]
            lg4 = lax.dot_general(kc, q4_ref[c], nt, preferred_element_type=F32) * qk_scale
            for g in range(GQA_GROUP):
                hd = c * GQA_GROUP + g
                lg = lg4[:, g * tq:(g + 1) * tq] + bias_ref[hd] + mask
                lg_ref[j, hd] = lg
                mx_ref[hd] = jnp.maximum(mx_ref[hd], jnp.max(fold(lg), axis=0) + chunk_shift(j, hd))

    _for_each_chunk(n_chunks, logits_chunk)
    for hd in range(N_HEADS):
        m_ref[hd] = jnp.max(mx_ref[hd], axis=0, keepdims=True)

    def pv_chunk(j):
        for c in range(N_KV_HEADS):
            vt = vt_ref[j, c * HEAD_DIM:(c + 1) * HEAD_DIM, :]
            for g in range(GQA_GROUP):
                hd = c * GQA_GROUP + g
                p = jnp.exp(lg_ref[j, hd] - (m_ref[hd] - chunk_shift(j, hd)))
                ls_ref[hd] += jnp.sum(fold(p), axis=0)
                ot_ref[hd] += jnp.dot(vt, p.astype(BF16), preferred_element_type=F32)

    _for_each_chunk(n_chunks, pv_chunk)
    for hd in range(N_HEADS):
        denom = jnp.sum(ls_ref[hd], axis=0, keepdims=True)
        o_ref[:, hd * HEAD_DIM:(hd + 1) * HEAD_DIM] = (ot_ref[hd] / denom).T.astype(BF16)


def _for_each_chunk(n_chunks, fn):
    def pair(p, carry):
        fn(2 * p)
        fn(2 * p + 1)
        return carry

    lax.fori_loop(0, n_chunks // 2, pair, 0)

    @pl.when(n_chunks % 2 == 1)
    def _():
        fn(n_chunks - 1)


def _attention(qi, kw, q, k, vt):
    n_batch, seq, _ = q.shape
    n_chunks, kv_w, tq = vt.shape[1:]
    n_keep = min(INDEX_TOPK, seq // 4)
    blk = lambda w: pl.BlockSpec((None, tq, w), lambda b, i: (b, i, 0))
    whole = lambda w: pl.BlockSpec((None, seq, w), lambda b, i: (b, 0, 0))
    return pl.pallas_call(
        functools.partial(_attn_kernel, tq=tq, n_keep=n_keep,
                          idx_w_scale=(N_IDX_HEADS ** -0.5) * (IDX_DIM ** -0.5),
                          qk_scale=HEAD_DIM ** -0.5),
        grid=(n_batch, n_chunks),
        in_specs=[blk(qi.shape[2]), blk(kw.shape[2]), whole(kw.shape[2]), blk(q.shape[2]),
                  whole(k.shape[2]),
                  pl.BlockSpec((None, n_chunks, kv_w, tq), lambda b, i: (b, 0, 0, 0))],
        out_specs=blk(q.shape[2]),
        out_shape=jax.ShapeDtypeStruct(q.shape, BF16),
        scratch_shapes=[pltpu.VMEM((n_chunks, tq, tq), F32),
                        pltpu.VMEM((n_chunks, N_HEADS, tq, tq), F32),
                        pltpu.VMEM((N_HEADS, tq, tq), F32),
                        pltpu.VMEM((N_KV_HEADS, GQA_GROUP * tq, HEAD_DIM), BF16),
                        pltpu.VMEM((N_HEADS, ACC_ROWS, tq), F32),
                        pltpu.VMEM((N_HEADS, 1, tq), F32),
                        pltpu.VMEM((N_HEADS, ACC_ROWS, tq), F32),
                        pltpu.VMEM((N_HEADS, HEAD_DIM, tq), F32)],
        compiler_params=_params("arbitrary", "arbitrary"),
        name="sparse_attn",
    )(qi, kw, kw, q, k, vt)


def _out_proj_kernel(ys_ref, ya_ref, x_ref, w_ref, g_ref, b_ref, o_ref):
    ssm_w = ys_ref.shape[1]
    for r in range(0, x_ref.shape[0], LN_ROWS):
        rows = slice(r, r + LN_ROWS)
        mix = jnp.dot(ys_ref[rows, :], w_ref[:ssm_w, :], preferred_element_type=F32)
        mix = mix + jnp.dot(ya_ref[rows, :], w_ref[ssm_w:, :], preferred_element_type=F32)
        o_ref[rows, :] = _layer_norm(DEEPNORM_ALPHA * x_ref[rows, :] + mix, g_ref[...], b_ref[...])


def _out_proj(ys, ya, x, w, g, b, *, tm):
    m, d = x.shape
    blk = lambda wd: pl.BlockSpec((tm, wd), lambda i: (i, 0))
    return pl.pallas_call(
        _out_proj_kernel,
        grid=(m // tm,),
        in_specs=[blk(ys.shape[1]), blk(ya.shape[1]), blk(d), _const_spec(w.shape),
                  _const_spec((1, d)), _const_spec((1, d))],
        out_specs=blk(d),
        out_shape=jax.ShapeDtypeStruct((m, d), F32),
        compiler_params=_params("arbitrary"),
        name="out_proj_ln1",
    )(ys, ya, x, w, g.reshape(1, d), b.reshape(1, d))


def _ffn_kernel(h_ref, halo_ref, wup_ref, wgate_ref, wdown_ref, cw_ref, cb_ref, g_ref, b_ref,
                o_ref, hb_ref, acc_ref, *, last_width):
    i = pl.program_id(1)
    j = pl.program_id(2)
    tm = h_ref.shape[0]
    tf = wup_ref.shape[1]

    @pl.when(j == 0)
    def _():
        hb_ref[:HALO_ROWS, :] = jnp.where(i == 0, 0.0, halo_ref[...]).astype(BF16)
        hb_ref[HALO_ROWS:, :] = h_ref[...].astype(BF16)
        acc_ref[...] = jnp.zeros_like(acc_ref)

    def chunk(width):
        ext = jnp.dot(hb_ref[...], wup_ref[:, :width], preferred_element_type=F32)
        cw = cw_ref[:, :width]
        hc = cb_ref[:, :width]
        for lag in range(CONV_WIDTH):
            start = HALO_ROWS - lag
            hc = hc + cw[CONV_WIDTH - 1 - lag:CONV_WIDTH - lag, :] * ext[start:start + tm, :]
        gate = jnp.dot(hb_ref[HALO_ROWS:, :], wgate_ref[:, :width], preferred_element_type=F32)
        act = (_gelu(hc) * gate).astype(BF16)
        acc_ref[...] += jnp.dot(act, wdown_ref[:width, :], preferred_element_type=F32)

    is_last = j == pl.num_programs(2) - 1
    if last_width == tf:
        chunk(tf)
    else:
        pl.when(jnp.logical_not(is_last))(functools.partial(chunk, tf))
        pl.when(is_last)(functools.partial(chunk, last_width))

    @pl.when(is_last)
    def _():
        o_ref[...] = _layer_norm(DEEPNORM_ALPHA * h_ref[...] + acc_ref[...], g_ref[...], b_ref[...])


def _ffn(h, w_up, w_gate, w_down, conv_w, conv_b, g, b, *, tm, tf):
    n_batch, seq, d = h.shape
    f = w_up.shape[1]
    n_chunks = -(-f // tf)
    halo_blocks = tm // HALO_ROWS
    return pl.pallas_call(
        functools.partial(_ffn_kernel, last_width=f - (n_chunks - 1) * tf),
        grid=(n_batch, seq // tm, n_chunks),
        in_specs=[
            pl.BlockSpec((None, tm, d), lambda bb, i, j: (bb, i, 0)),
            pl.BlockSpec((None, HALO_ROWS, d),
                         lambda bb, i, j: (bb, jnp.maximum(i * halo_blocks - 1, 0), 0)),
            pl.BlockSpec((d, tf), lambda bb, i, j: (0, j)),
            pl.BlockSpec((d, tf), lambda bb, i, j: (0, j)),
            pl.BlockSpec((tf, d), lambda bb, i, j: (j, 0)),
            pl.BlockSpec((CONV_WIDTH, tf), lambda bb, i, j: (0, j)),
            pl.BlockSpec((1, tf), lambda bb, i, j: (0, j)),
            _const_spec((1, d)),
            _const_spec((1, d)),
        ],
        out_specs=pl.BlockSpec((None, tm, d), lambda bb, i, j: (bb, i, 0)),
        out_shape=jax.ShapeDtypeStruct((n_batch, seq, d), F32),
        scratch_shapes=[pltpu.VMEM((HALO_ROWS + tm, d), BF16), pltpu.VMEM((tm, d), F32)],
        compiler_params=_params("arbitrary", "arbitrary", "arbitrary"),
        name="ffn_ln2",
    )(h, h, w_up, w_gate, w_down, conv_w, conv_b.reshape(1, f), g.reshape(1, d), b.reshape(1, d))


def _layer(h, w_in, a_re, a_im, log_dt, b_re, b_im, c_re, c_im, ssm_d, w_glu, b_glu, w_out,
           ln1_g, ln1_b, w_up, w_gate, conv_w, conv_b, w_down, ln2_g, ln2_b):
    n_batch, seq, d = h.shape
    assert n_batch == SUBLANES, "the time-major scan layout puts the batch on the sublanes"
    n_groups, n_state = a_re.shape
    ssm_w = n_groups * SSM_GROUP_CH
    attn_w = N_HEADS * HEAD_DIM
    kv_w = N_KV_HEADS * HEAD_DIM
    qi_w = N_IDX_HEADS * IDX_DIM
    n_slabs = ssm_w // LANES
    assert n_state == SSM_STATE and ssm_w % LANES == 0
    t_seq, t_row, t_ff, t_scan = _tiles(seq, w_up.shape[1])

    widths = (ssm_w, attn_w, kv_w, kv_w, qi_w, LANES)
    assert w_in.shape[1] == sum(widths[:-1]) + IDX_DIM + N_IDX_HEADS
    w_in_b = jnp.pad(w_in, ((0, 0), (0, sum(widths) - w_in.shape[1]))).astype(BF16)
    u_tm, q, k, vt, qi, kw = _in_proj(h, w_in_b, widths, ts=t_seq)

    abar_re, abar_im, bbar_re, bbar_im = _s5_discretize(
        a_re, a_im, log_dt, jnp.swapaxes(b_re, 1, 2), jnp.swapaxes(b_im, 1, 2))
    gl = GROUPS_PER_SLAB
    to_slabs = lambda m: m.reshape(n_slabs, gl, *m.shape[1:])
    bbd = jnp.concatenate([_block_diag_slabs(to_slabs(bbar_re)),
                           _block_diag_slabs(to_slabs(bbar_im))], axis=-1).astype(BF16)
    cbd_re = _block_diag_slabs(to_slabs(jnp.swapaxes(c_re, 1, 2)))
    cbd_im = _block_diag_slabs(to_slabs(jnp.swapaxes(c_im, 1, 2)))
    cbd = jnp.concatenate([cbd_re, -cbd_im], axis=1).astype(BF16)
    bcast = lambda m: jnp.broadcast_to(m.reshape(n_slabs, 1, gl * n_state), (n_slabs, n_batch, gl * n_state))
    y_tm = _s5_scan(u_tm, bbd, bcast(abar_re), bcast(abar_im), cbd,
                    ssm_d.reshape(n_slabs, 1, LANES), ts=t_scan, n_batch=n_batch)
    y_ssm = _glu(y_tm, w_glu.astype(BF16), b_glu, ts=t_seq, n_batch=n_batch)

    y_attn = _attention(qi, kw, q, k, vt)

    h1 = _out_proj(y_ssm.reshape(n_batch * seq, ssm_w), y_attn.reshape(n_batch * seq, attn_w),
                   h.reshape(n_batch * seq, d), w_out.astype(BF16), ln1_g, ln1_b, tm=t_row)
    return _ffn(h1.reshape(n_batch, seq, d), w_up.astype(BF16), w_gate.astype(BF16),
                w_down.astype(BF16), conv_w, conv_b, ln2_g, ln2_b, tm=t_row, tf=t_ff)


def kernel(x, w_in, ssm_a_re, ssm_a_im, ssm_log_dt, ssm_b_re, ssm_b_im, ssm_c_re, ssm_c_im, ssm_d,
           w_glu, b_glu, w_out, ln1_g, ln1_b, w_up, w_gate, conv_w, conv_b, w_down, ln2_g, ln2_b):
    h = x
    for l in range(w_in.shape[0]):
        h = _layer(h, w_in[l], ssm_a_re[l], ssm_a_im[l], ssm_log_dt[l], ssm_b_re[l], ssm_b_im[l],
                   ssm_c_re[l], ssm_c_im[l], ssm_d[l], w_glu[l], b_glu[l], w_out[l], ln1_g[l],
                   ln1_b[l], w_up[l], w_gate[l], conv_w[l], conv_b[l], w_down[l], ln2_g[l], ln2_b[l])
    return h
```

```python
import functools
import math

import jax
import jax.numpy as jnp
from jax import lax
from jax.experimental import pallas as pl
from jax.experimental.pallas import tpu as pltpu

F32 = jnp.float32
BF16 = jnp.bfloat16
I32 = jnp.int32

LANES = 128
SUBLANES = 8
VMEM_LIMIT_BYTES = 56 * 1024 * 1024

SSM_GROUP_CH = 16
SSM_STATE = 64
N_HEADS = 8
HEAD_DIM = 128
N_KV_HEADS = 2
GQA_GROUP = N_HEADS // N_KV_HEADS
N_IDX_HEADS = 16
IDX_DIM = 64
INDEX_TOPK = 256
CONV_WIDTH = 3
LN_EPS = 1e-5
DEPTH = 1
DEEPNORM_ALPHA = (2.0 * DEPTH) ** 0.25

GROUPS_PER_SLAB = LANES // SSM_GROUP_CH
SLAB_STATE = GROUPS_PER_SLAB * SSM_STATE
INT_MIN = -(2 ** 31)
ACC_ROWS = 4 * SUBLANES
LN_ROWS = 128
HALO_ROWS = 2 * SUBLANES


def _tiles(seq, d_ff):
    t_seq = min(256, seq)
    t_row = min(512, seq)
    t_ff = min(1024, d_ff)
    t_scan = min(128, seq)
    assert seq % t_seq == 0 and seq % t_row == 0 and d_ff % LANES == 0 and seq % t_scan == 0
    return t_seq, t_row, t_ff, t_scan


def _params(*sem):
    return pltpu.CompilerParams(dimension_semantics=sem, vmem_limit_bytes=VMEM_LIMIT_BYTES)


def _const_spec(shape):
    zeros = (0,) * len(shape)
    return pl.BlockSpec(shape, lambda *_: zeros, pipeline_mode=pl.Buffered(1))


def _layer_norm(v, g, b):
    mu = jnp.mean(v, axis=-1, keepdims=True)
    vc = v - mu
    var = jnp.mean(vc * vc, axis=-1, keepdims=True)
    return vc * lax.rsqrt(var + LN_EPS) * g + b


def _gelu(v):
    return 0.5 * v * (1.0 + lax.erf(v * (1.0 / math.sqrt(2.0))))


def _in_proj_kernel(x_ref, w_ref, u_ref, q_ref, k_ref, v_ref, qi_ref, kw_ref, *, n_batch, widths):
    b = pl.program_id(1)
    xb = x_ref[...].astype(BF16)
    ts = xb.shape[0]
    starts = [sum(widths[:n]) for n in range(len(widths))]
    proj = lambda n: jnp.dot(xb, w_ref[:, starts[n]:starts[n] + widths[n]],
                             preferred_element_type=F32)
    u = proj(0)
    for s in range(u.shape[1] // LANES):
        u_ref[s, pl.ds(b, ts, stride=n_batch), :] = u[:, s * LANES:(s + 1) * LANES]
    q_ref[...] = proj(1).astype(BF16)
    k_ref[...] = proj(2).astype(BF16)
    v_ref[...] = proj(3).T.astype(BF16)
    qi_ref[...] = proj(4).astype(BF16)
    kw_ref[...] = proj(5)


def _in_proj(x, w, widths, *, ts):
    n_batch, seq, d = x.shape
    w_u, w_q, w_k, w_v, w_qi, w_kw = widths
    n_slabs = w_u // LANES
    row = lambda wd: pl.BlockSpec((None, ts, wd), lambda i, b: (b, i, 0))
    out_shape = (
        jax.ShapeDtypeStruct((n_slabs, seq * n_batch, LANES), F32),
        jax.ShapeDtypeStruct((n_batch, seq, w_q), BF16),
        jax.ShapeDtypeStruct((n_batch, seq, w_k), BF16),
        jax.ShapeDtypeStruct((n_batch, seq // ts, w_v, ts), BF16),
        jax.ShapeDtypeStruct((n_batch, seq, w_qi), BF16),
        jax.ShapeDtypeStruct((n_batch, seq, w_kw), F32),
    )
    return pl.pallas_call(
        functools.partial(_in_proj_kernel, n_batch=n_batch, widths=widths),
        grid=(seq // ts, n_batch),
        in_specs=[row(d), _const_spec(w.shape)],
        out_specs=(
            pl.BlockSpec((n_slabs, ts * n_batch, LANES), lambda i, b: (0, i, 0)),
            row(w_q), row(w_k),
            pl.BlockSpec((None, None, w_v, ts), lambda i, b: (b, i, 0, 0)),
            row(w_qi), row(w_kw),
        ),
        out_shape=out_shape,
        compiler_params=_params("arbitrary", "arbitrary"),
        name="in_proj",
    )(x, w)


def _s5_discretize_kernel(a_re_ref, a_im_ref, log_dt_ref, bt_re_ref, bt_im_ref,
                          abar_re_ref, abar_im_ref, bbar_re_ref, bbar_im_ref):
    a_re = a_re_ref[...]
    a_im = a_im_ref[...]
    dt = jnp.exp(log_dt_ref[...])
    mag = jnp.exp(dt * a_re)
    ang = dt * a_im
    abar_re = mag * jnp.cos(ang)
    abar_im = mag * jnp.sin(ang)
    num_re = abar_re - 1.0
    num_im = abar_im
    den = a_re * a_re + a_im * a_im
    f_re = (num_re * a_re + num_im * a_im) / den
    f_im = (num_im * a_re - num_re * a_im) / den
    abar_re_ref[...] = abar_re
    abar_im_ref[...] = abar_im
    bt_re = bt_re_ref[...]
    bt_im = bt_im_ref[...]
    bbar_re_ref[...] = f_re * bt_re - f_im * bt_im
    bbar_im_ref[...] = f_re * bt_im + f_im * bt_re


def _s5_discretize(a_re, a_im, log_dt, bt_re, bt_im):
    g, p = a_re.shape
    abar_re, abar_im, bbar_re, bbar_im = pl.pallas_call(
        _s5_discretize_kernel,
        out_shape=(jax.ShapeDtypeStruct((g, 1, p), F32), jax.ShapeDtypeStruct((g, 1, p), F32),
                   jax.ShapeDtypeStruct(bt_re.shape, F32), jax.ShapeDtypeStruct(bt_re.shape, F32)),
        name="s5_discretize",
    )(a_re.reshape(g, 1, p), a_im.reshape(g, 1, p), log_dt.reshape(g, 1, 1), bt_re, bt_im)
    return abar_re.reshape(g, p), abar_im.reshape(g, p), bbar_re, bbar_im


def _block_diag_slabs(m):
    n, gl, a, b = m.shape
    eye = jnp.eye(gl, dtype=m.dtype)
    full = m[:, :, :, None, :] * eye[None, :, None, :, None]
    return full.reshape(n, gl * a, gl * b)


def _s5_scan_kernel(u_next_ref, u_prev_ref, bbd_ref, ar_ref, ai_ref, cbd_ref, d_ref, y_ref,
                    buf_a, buf_b, carry_ref, *, ts, n_batch, tiles_per_slab, n_tiles):
    n = pl.program_id(0)
    last = n_tiles - 1
    cur = jnp.minimum(n, last)
    slab_next = jnp.minimum(n + 1, last) // tiles_per_slab
    slab_cur = cur // tiles_per_slab
    slab_prev = jnp.maximum(n - 1, 0) // tiles_per_slab
    half = ar_ref.shape[-1]

    @pl.when(n == 0)
    def _():
        buf_a[...] = jnp.dot(u_prev_ref[...].astype(BF16), bbd_ref[0], preferred_element_type=F32)
        buf_b[...] = jnp.zeros_like(buf_b)
        carry_ref[...] = jnp.zeros_like(carry_ref)

    def stages(cur_buf, other_buf):
        y = jnp.dot(other_buf[...].astype(BF16), cbd_ref[slab_prev], preferred_element_type=F32)
        y_ref[...] = _gelu(y + d_ref[slab_prev] * u_prev_ref[...])
        other_buf[...] = jnp.dot(u_next_ref[...].astype(BF16), bbd_ref[slab_next],
                                 preferred_element_type=F32)
        ar = ar_ref[slab_cur]
        ai = ai_ref[slab_cur]
        first = cur % tiles_per_slab == 0
        re = jnp.where(first, 0.0, carry_ref[:, :half])
        im = jnp.where(first, 0.0, carry_ref[:, half:])
        for t in range(ts):
            rows = slice(t * n_batch, (t + 1) * n_batch)
            bu = cur_buf[rows, :]
            re, im = ar * re - ai * im + bu[:, :half], ar * im + ai * re + bu[:, half:]
            cur_buf[rows, :] = jnp.concatenate([re, im], axis=-1)
        carry_ref[...] = jnp.concatenate([re, im], axis=-1)

    @pl.when(n % 2 == 0)
    def _():
        stages(buf_a, buf_b)

    @pl.when(n % 2 == 1)
    def _():
        stages(buf_b, buf_a)


def _s5_scan(u_tm, bbd, ar, ai, cbd, d, *, ts, n_batch):
    n_slabs, rows, _ = u_tm.shape
    tr = ts * n_batch
    tiles_per_slab = rows // tr
    n_tiles = n_slabs * tiles_per_slab
    n_state2 = bbd.shape[-1]
    tile = lambda shift: pl.BlockSpec(
        (tr, LANES), lambda n: (jnp.clip(n + shift, 0, n_tiles - 1), 0))
    u_flat = u_tm.reshape(n_slabs * rows, LANES)
    y = pl.pallas_call(
        functools.partial(_s5_scan_kernel, ts=ts, n_batch=n_batch,
                          tiles_per_slab=tiles_per_slab, n_tiles=n_tiles),
        grid=(n_tiles + 1,),
        in_specs=[tile(1), tile(-1), _const_spec(bbd.shape), _const_spec(ar.shape),
                  _const_spec(ai.shape), _const_spec(cbd.shape), _const_spec(d.shape)],
        out_specs=tile(-1),
        out_shape=jax.ShapeDtypeStruct(u_flat.shape, F32),
        scratch_shapes=[pltpu.VMEM((tr, n_state2), F32), pltpu.VMEM((tr, n_state2), F32),
                        pltpu.VMEM((n_batch, n_state2), F32)],
        compiler_params=_params("arbitrary"),
        name="s5_scan",
    )(u_flat, u_flat, bbd, ar, ai, cbd, d)
    return y.reshape(u_tm.shape)


def _glu_kernel(y_ref, w_ref, b_ref, o_ref, *, ts, n_batch):
    n_slabs = y_ref.shape[0]
    w = w_ref[...]
    bias = b_ref[...]
    for b in range(n_batch):
        yb = jnp.concatenate(
            [y_ref[s, pl.ds(b, ts, stride=n_batch), :] for s in range(n_slabs)], axis=-1)
        z = jnp.dot(yb.astype(BF16), w, preferred_element_type=F32) + bias
        o_ref[b] = (yb * jax.nn.sigmoid(z)).astype(BF16)


def _glu(y_tm, w_glu, b_glu, *, ts, n_batch):
    n_slabs, rows, _ = y_tm.shape
    seq = rows // n_batch
    width = n_slabs * LANES
    return pl.pallas_call(
        functools.partial(_glu_kernel, ts=ts, n_batch=n_batch),
        grid=(seq // ts,),
        in_specs=[
            pl.BlockSpec((n_slabs, ts * n_batch, LANES), lambda i: (0, i, 0)),
            _const_spec(w_glu.shape),
            _const_spec((1, width)),
        ],
        out_specs=pl.BlockSpec((n_batch, ts, width), lambda i: (0, i, 0)),
        out_shape=jax.ShapeDtypeStruct((n_batch, seq, width), BF16),
        compiler_params=_params("arbitrary"),
        name="glu",
    )(y_tm, w_glu, b_glu.reshape(1, width))


def _attn_kernel(qi_ref, kwq_ref, kwf_ref, q_ref, k_ref, vt_ref, *rest,
                 n_cast, tq, n_keep, idx_w_scale, qk_scale):
    cast_src, o_ref, cast_dst = rest[:n_cast], rest[n_cast], rest[n_cast + 1:2 * n_cast + 1]
    sc_ref, lg_ref, bias_ref, q4_ref, mx_ref, m_ref, ls_ref, ot_ref = rest[2 * n_cast + 1:]
    for src, dst in zip(cast_src, cast_dst):
        dst[...] = src[...].astype(BF16)

    i = pl.program_id(1)
    n_chunks = i + 1
    nt = (((1,), (1,)), ((), ()))
    fold = lambda a: a.reshape(tq // ACC_ROWS, ACC_ROWS, tq)

    key_pos = lax.broadcasted_iota(I32, (tq, tq), 0)
    qry_pos = lax.broadcasted_iota(I32, (tq, tq), 1)
    wi_t = kwq_ref[...].T[IDX_DIM:IDX_DIM + N_IDX_HEADS, :] * idx_w_scale

    def score_chunk(j):
        off = pl.multiple_of(j * tq, tq)
        kic = kwf_ref[pl.ds(off, tq), :][:, :IDX_DIM].astype(BF16)
        s = jnp.zeros((tq, tq), F32)
        for h in range(N_IDX_HEADS):
            r = lax.dot_general(kic, qi_ref[:, h * IDX_DIM:(h + 1) * IDX_DIM], nt,
                                preferred_element_type=F32)
            s = s + wi_t[h:h + 1, :] * jnp.maximum(r, 0.0)
        causal = (j - i) * tq + key_pos <= qry_pos
        sc_ref[j] = jnp.where(causal, s, -jnp.inf)

    _for_each_chunk(n_chunks, score_chunk)

    def decode(key):
        return pltpu.bitcast(jnp.where(key < 0, key ^ jnp.int32(0x7FFFFFFF), key), F32)

    def count_ge(thr, strict=False):
        thr_b = jnp.broadcast_to(thr, (ACC_ROWS, tq))

        def body(j, acc):
            s = fold(sc_ref[j])
            hit = s > thr_b if strict else s >= thr_b
            return acc + jnp.sum(jnp.where(hit, 1.0, 0.0), axis=0)

        acc = lax.fori_loop(0, n_chunks, body, jnp.zeros((ACC_ROWS, tq), F32))
        return jnp.sum(acc, axis=0, keepdims=True)

    keep = jnp.float32(n_keep)
    cnt0 = count_ge(jnp.zeros((1, tq), F32))
    tau = jnp.where(cnt0 >= keep, jnp.int32(0), jnp.int32(INT_MIN))
    cnt = jnp.where(cnt0 >= keep, cnt0, 0.0)

    def bit_body(bi, carry):
        tau, cnt = carry
        cand = tau + jnp.left_shift(jnp.int32(1), jnp.int32(30) - bi)
        c = count_ge(decode(cand))
        return jnp.where(c >= keep, cand, tau), jnp.where(c >= keep, c, cnt)

    tau, cnt = lax.fori_loop(0, 31, bit_body, (tau, cnt))
    thr = jnp.where(tau == jnp.int32(INT_MIN), jnp.finfo(F32).min, decode(tau))

    @pl.when(jnp.max(cnt) > keep)
    def _():
        need = keep - count_ge(thr, strict=True)
        lower = jnp.where(key_pos >= qry_pos, 1.0, 0.0).astype(BF16)

        def tie_chunk(j, run):
            s = sc_ref[j]
            eq = s == thr
            seen = run + jnp.dot(lower, jnp.where(eq, 1.0, 0.0).astype(BF16),
                                 preferred_element_type=F32)
            sc_ref[j] = jnp.where(eq, jnp.where(seen > need, -jnp.inf, s), s)
            return seen[tq - 1:tq, :]

        lax.fori_loop(0, n_chunks, tie_chunk, jnp.zeros((1, tq), F32))

    def mask_chunk(j, carry):
        sc_ref[j] = jnp.where(sc_ref[j] >= thr, 0.0, -1e30)
        return carry

    lax.fori_loop(0, n_chunks, mask_chunk, 0)

    slopes = [2.0 ** (-8.0 * (hd + 1) / N_HEADS) for hd in range(N_HEADS)]
    for hd in range(N_HEADS):
        c, g = divmod(hd, GQA_GROUP)
        q4_ref[c, g * tq:(g + 1) * tq, :] = q_ref[:, hd * HEAD_DIM:(hd + 1) * HEAD_DIM]
        bias_ref[hd] = slopes[hd] * key_pos.astype(F32)
    mx_ref[...] = jnp.full(mx_ref.shape, -jnp.inf, F32)
    ls_ref[...] = jnp.zeros_like(ls_ref)
    ot_ref[...] = jnp.zeros_like(ot_ref)

    def chunk_shift(j, hd):
        return ((j - i) * tq).astype(F32) * slopes[hd]

    def logits_chunk(j):
        off = pl.multiple_of(j * tq, tq)
        mask = sc_ref[j]
        for c in range(N_KV_HEADS):
            kc = k_ref[pl.ds(off, tq), :][:, c * HEAD_DIM:(c + 1) * HEAD_DIM]
            lg4 = lax.dot_general(kc, q4_ref[c], nt, preferred_element_type=F32) * qk_scale
            for g in range(GQA_GROUP):
                hd = c * GQA_GROUP + g
                lg = lg4[:, g * tq:(g + 1) * tq] + bias_ref[hd] + mask
                lg_ref[j, hd] = lg
                mx_ref[hd] = jnp.maximum(mx_ref[hd], jnp.max(fold(lg), axis=0) + chunk_shift(j, hd))

    _for_each_chunk(n_chunks, logits_chunk)
    for hd in range(N_HEADS):
        m_ref[hd] = jnp.max(mx_ref[hd], axis=0, keepdims=True)

    def pv_chunk(j):
        for c in range(N_KV_HEADS):
            vt = vt_ref[j, c * HEAD_DIM:(c + 1) * HEAD_DIM, :]
            for g in range(GQA_GROUP):
                hd = c * GQA_GROUP + g
                p = jnp.exp(lg_ref[j, hd] - (m_ref[hd] - chunk_shift(j, hd)))
                ls_ref[hd] += jnp.sum(fold(p), axis=0)
                ot_ref[hd] += jnp.dot(vt, p.astype(BF16), preferred_element_type=F32)

    _for_each_chunk(n_chunks, pv_chunk)
    for hd in range(N_HEADS):
        denom = jnp.sum(ls_ref[hd], axis=0, keepdims=True)
        o_ref[:, hd * HEAD_DIM:(hd + 1) * HEAD_DIM] = (ot_ref[hd] / denom).T.astype(BF16)


def _for_each_chunk(n_chunks, fn):
    def pair(p, carry):
        fn(2 * p)
        fn(2 * p + 1)
        return carry

    lax.fori_loop(0, n_chunks // 2, pair, 0)

    @pl.when(n_chunks % 2 == 1)
    def _():
        fn(n_chunks - 1)


def _cast_block_rows(n_rows, n_steps):
    bf16_rows = 2 * SUBLANES
    for rows in range(bf16_rows, n_rows + 1, bf16_rows):
        if n_rows % rows == 0 and n_rows // rows <= n_steps:
            return rows
    raise ValueError(f"no row block for {n_rows} rows in {n_steps} steps")


def _attention(qi, kw, q, k, vt, cast_weights):
    n_batch, seq, _ = q.shape
    n_chunks, kv_w, tq = vt.shape[1:]
    n_keep = min(INDEX_TOPK, seq // 4)
    blk = lambda w: pl.BlockSpec((None, tq, w), lambda b, i: (b, i, 0))
    whole = lambda w: pl.BlockSpec((None, seq, w), lambda b, i: (b, 0, 0))

    def cast_spec(a):
        rows = _cast_block_rows(a.shape[0], n_batch * n_chunks)
        last = a.shape[0] // rows - 1
        return pl.BlockSpec((rows, a.shape[1]),
                            lambda b, i: (jnp.minimum(b * n_chunks + i, last), 0))

    cast_specs = [cast_spec(a) for a in cast_weights]
    out = pl.pallas_call(
        functools.partial(_attn_kernel, n_cast=len(cast_weights), tq=tq, n_keep=n_keep,
                          idx_w_scale=(N_IDX_HEADS ** -0.5) * (IDX_DIM ** -0.5),
                          qk_scale=HEAD_DIM ** -0.5),
        grid=(n_batch, n_chunks),
        in_specs=[blk(qi.shape[2]), blk(kw.shape[2]), whole(kw.shape[2]), blk(q.shape[2]),
                  whole(k.shape[2]),
                  pl.BlockSpec((None, n_chunks, kv_w, tq), lambda b, i: (b, 0, 0, 0))] + cast_specs,
        out_specs=[blk(q.shape[2])] + cast_specs,
        out_shape=[jax.ShapeDtypeStruct(q.shape, BF16)]
        + [jax.ShapeDtypeStruct(a.shape, BF16) for a in cast_weights],
        scratch_shapes=[pltpu.VMEM((n_chunks, tq, tq), F32),
                        pltpu.VMEM((n_chunks, N_HEADS, tq, tq), F32),
                        pltpu.VMEM((N_HEADS, tq, tq), F32),
                        pltpu.VMEM((N_KV_HEADS, GQA_GROUP * tq, HEAD_DIM), BF16),
                        pltpu.VMEM((N_HEADS, ACC_ROWS, tq), F32),
                        pltpu.VMEM((N_HEADS, 1, tq), F32),
                        pltpu.VMEM((N_HEADS, ACC_ROWS, tq), F32),
                        pltpu.VMEM((N_HEADS, HEAD_DIM, tq), F32)],
        compiler_params=_params("arbitrary", "arbitrary"),
        name="sparse_attn",
    )(qi, kw, kw, q, k, vt, *cast_weights)
    return out[0], out[1:]


def _out_proj_kernel(ys_ref, ya_ref, x_ref, w_ref, g_ref, b_ref, o_ref):
    ssm_w = ys_ref.shape[1]
    for r in range(0, x_ref.shape[0], LN_ROWS):
        rows = slice(r, r + LN_ROWS)
        mix = jnp.dot(ys_ref[rows, :], w_ref[:ssm_w, :], preferred_element_type=F32)
        mix = mix + jnp.dot(ya_ref[rows, :], w_ref[ssm_w:, :], preferred_element_type=F32)
        o_ref[rows, :] = _layer_norm(DEEPNORM_ALPHA * x_ref[rows, :] + mix, g_ref[...], b_ref[...])


def _out_proj(ys, ya, x, w, g, b, *, tm):
    m, d = x.shape
    blk = lambda wd: pl.BlockSpec((tm, wd), lambda i: (i, 0))
    return pl.pallas_call(
        _out_proj_kernel,
        grid=(m // tm,),
        in_specs=[blk(ys.shape[1]), blk(ya.shape[1]), blk(d), _const_spec(w.shape),
                  _const_spec((1, d)), _const_spec((1, d))],
        out_specs=blk(d),
        out_shape=jax.ShapeDtypeStruct((m, d), F32),
        compiler_params=_params("arbitrary"),
        name="out_proj_ln1",
    )(ys, ya, x, w, g.reshape(1, d), b.reshape(1, d))


def _ffn_kernel(h_ref, halo_ref, wup_ref, wgate_ref, wdown_ref, cw_ref, cb_ref, g_ref, b_ref,
                o_ref, hb_ref, acc_ref, *, last_width):
    i = pl.program_id(1)
    j = pl.program_id(2)
    tm = h_ref.shape[0]
    tf = wup_ref.shape[1]

    @pl.when(j == 0)
    def _():
        hb_ref[:HALO_ROWS, :] = jnp.where(i == 0, 0.0, halo_ref[...]).astype(BF16)
        hb_ref[HALO_ROWS:, :] = h_ref[...].astype(BF16)
        acc_ref[...] = jnp.zeros_like(acc_ref)

    def chunk(width):
        ext = jnp.dot(hb_ref[...], wup_ref[:, :width], preferred_element_type=F32)
        cw = cw_ref[:, :width]
        hc = cb_ref[:, :width]
        for lag in range(CONV_WIDTH):
            start = HALO_ROWS - lag
            hc = hc + cw[CONV_WIDTH - 1 - lag:CONV_WIDTH - lag, :] * ext[start:start + tm, :]
        gate = jnp.dot(hb_ref[HALO_ROWS:, :], wgate_ref[:, :width], preferred_element_type=F32)
        act = (_gelu(hc) * gate).astype(BF16)
        acc_ref[...] += jnp.dot(act, wdown_ref[:width, :], preferred_element_type=F32)

    is_last = j == pl.num_programs(2) - 1
    if last_width == tf:
        chunk(tf)
    else:
        pl.when(jnp.logical_not(is_last))(functools.partial(chunk, tf))
        pl.when(is_last)(functools.partial(chunk, last_width))

    @pl.when(is_last)
    def _():
        o_ref[...] = _layer_norm(DEEPNORM_ALPHA * h_ref[...] + acc_ref[...], g_ref[...], b_ref[...])


def _ffn(h, w_up, w_gate, w_down, conv_w, conv_b, g, b, *, tm, tf):
    n_batch, seq, d = h.shape
    f = w_up.shape[1]
    n_chunks = -(-f // tf)
    halo_blocks = tm // HALO_ROWS
    return pl.pallas_call(
        functools.partial(_ffn_kernel, last_width=f - (n_chunks - 1) * tf),
        grid=(n_batch, seq // tm, n_chunks),
        in_specs=[
            pl.BlockSpec((None, tm, d), lambda bb, i, j: (bb, i, 0)),
            pl.BlockSpec((None, HALO_ROWS, d),
                         lambda bb, i, j: (bb, jnp.maximum(i * halo_blocks - 1, 0), 0)),
            pl.BlockSpec((d, tf), lambda bb, i, j: (0, j)),
            pl.BlockSpec((d, tf), lambda bb, i, j: (0, j)),
            pl.BlockSpec((tf, d), lambda bb, i, j: (j, 0)),
            pl.BlockSpec((CONV_WIDTH, tf), lambda bb, i, j: (0, j)),
            pl.BlockSpec((1, tf), lambda bb, i, j: (0, j)),
            _const_spec((1, d)),
            _const_spec((1, d)),
        ],
        out_specs=pl.BlockSpec((None, tm, d), lambda bb, i, j: (bb, i, 0)),
        out_shape=jax.ShapeDtypeStruct((n_batch, seq, d), F32),
        scratch_shapes=[pltpu.VMEM((HALO_ROWS + tm, d), BF16), pltpu.VMEM((tm, d), F32)],
        compiler_params=_params("arbitrary", "arbitrary", "arbitrary"),
        name="ffn_ln2",
    )(h, h, w_up, w_gate, w_down, conv_w, conv_b.reshape(1, f), g.reshape(1, d), b.reshape(1, d))


def _layer(h, w_in, a_re, a_im, log_dt, b_re, b_im, c_re, c_im, ssm_d, w_glu, b_glu, w_out,
           ln1_g, ln1_b, w_up, w_gate, conv_w, conv_b, w_down, ln2_g, ln2_b):
    n_batch, seq, d = h.shape
    assert n_batch == SUBLANES, "the time-major scan layout puts the batch on the sublanes"
    n_groups, n_state = a_re.shape
    ssm_w = n_groups * SSM_GROUP_CH
    attn_w = N_HEADS * HEAD_DIM
    kv_w = N_KV_HEADS * HEAD_DIM
    qi_w = N_IDX_HEADS * IDX_DIM
    n_slabs = ssm_w // LANES
    assert n_state == SSM_STATE and ssm_w % LANES == 0
    t_seq, t_row, t_ff, t_scan = _tiles(seq, w_up.shape[1])

    widths = (ssm_w, attn_w, kv_w, kv_w, qi_w, LANES)
    assert w_in.shape[1] == sum(widths[:-1]) + IDX_DIM + N_IDX_HEADS
    w_in_b = jnp.pad(w_in, ((0, 0), (0, sum(widths) - w_in.shape[1]))).astype(BF16)
    u_tm, q, k, vt, qi, kw = _in_proj(h, w_in_b, widths, ts=t_seq)

    abar_re, abar_im, bbar_re, bbar_im = _s5_discretize(
        a_re, a_im, log_dt, jnp.swapaxes(b_re, 1, 2), jnp.swapaxes(b_im, 1, 2))
    gl = GROUPS_PER_SLAB
    to_slabs = lambda m: m.reshape(n_slabs, gl, *m.shape[1:])
    bbd = jnp.concatenate([_block_diag_slabs(to_slabs(bbar_re)),
                           _block_diag_slabs(to_slabs(bbar_im))], axis=-1).astype(BF16)
    cbd_re = _block_diag_slabs(to_slabs(jnp.swapaxes(c_re, 1, 2)))
    cbd_im = _block_diag_slabs(to_slabs(jnp.swapaxes(c_im, 1, 2)))
    cbd = jnp.concatenate([cbd_re, -cbd_im], axis=1).astype(BF16)
    bcast = lambda m: jnp.broadcast_to(m.reshape(n_slabs, 1, gl * n_state), (n_slabs, n_batch, gl * n_state))
    y_tm = _s5_scan(u_tm, bbd, bcast(abar_re), bcast(abar_im), cbd,
                    ssm_d.reshape(n_slabs, 1, LANES), ts=t_scan, n_batch=n_batch)
    y_ssm = _glu(y_tm, w_glu.astype(BF16), b_glu, ts=t_seq, n_batch=n_batch)

    y_attn, (w_out_b, w_up_b, w_gate_b, w_down_b) = _attention(
        qi, kw, q, k, vt, (w_out, w_up, w_gate, w_down))

    h1 = _out_proj(y_ssm.reshape(n_batch * seq, ssm_w), y_attn.reshape(n_batch * seq, attn_w),
                   h.reshape(n_batch * seq, d), w_out_b, ln1_g, ln1_b, tm=t_row)
    return _ffn(h1.reshape(n_batch, seq, d), w_up_b, w_gate_b, w_down_b, conv_w, conv_b,
                ln2_g, ln2_b, tm=t_row, tf=t_ff)


def kernel(x, w_in, ssm_a_re, ssm_a_im, ssm_log_dt, ssm_b_re, ssm_b_im, ssm_c_re, ssm_c_im, ssm_d,
           w_glu, b_glu, w_out, ln1_g, ln1_b, w_up, w_gate, conv_w, conv_b, w_down, ln2_g, ln2_b):
    h = x
    for l in range(w_in.shape[0]):
        h = _layer(h, w_in[l], ssm_a_re[l], ssm_a_im[l], ssm_log_dt[l], ssm_b_re[l], ssm_b_im[l],
                   ssm_c_re[l], ssm_c_im[l], ssm_d[l], w_glu[l], b_glu[l], w_out[l], ln1_g[l],
                   ln1_b[l], w_up[l], w_gate[l], conv_w[l], conv_b[l], w_down[l], ln2_g[l], ln2_b[l])
    return h
```

```python
import functools
import math

import jax
import jax.numpy as jnp
from jax import lax
from jax.experimental import pallas as pl
from jax.experimental.pallas import tpu as pltpu

F32 = jnp.float32
BF16 = jnp.bfloat16
I32 = jnp.int32

LANES = 128
SUBLANES = 8
VMEM_LIMIT_BYTES = 56 * 1024 * 1024

SSM_GROUP_CH = 16
SSM_STATE = 64
N_HEADS = 8
HEAD_DIM = 128
N_KV_HEADS = 2
GQA_GROUP = N_HEADS // N_KV_HEADS
N_IDX_HEADS = 16
IDX_DIM = 64
INDEX_TOPK = 256
CONV_WIDTH = 3
LN_EPS = 1e-5
DEPTH = 1
DEEPNORM_ALPHA = (2.0 * DEPTH) ** 0.25

GROUPS_PER_SLAB = LANES // SSM_GROUP_CH
SLAB_STATE = GROUPS_PER_SLAB * SSM_STATE
INT_MIN = -(2 ** 31)
ACC_ROWS = 4 * SUBLANES
LN_ROWS = 128
HALO_ROWS = 2 * SUBLANES


def _tiles(seq, d_ff):
    t_seq = min(256, seq)
    t_row = min(512, seq)
    t_ff = min(1024, d_ff)
    t_scan = min(128, seq)
    assert seq % t_seq == 0 and seq % t_row == 0 and d_ff % LANES == 0 and seq % t_scan == 0
    return t_seq, t_row, t_ff, t_scan


def _params(*sem):
    return pltpu.CompilerParams(dimension_semantics=sem, vmem_limit_bytes=VMEM_LIMIT_BYTES)


def _const_spec(shape):
    zeros = (0,) * len(shape)
    return pl.BlockSpec(shape, lambda *_: zeros, pipeline_mode=pl.Buffered(1))


def _layer_norm(v, g, b):
    mu = jnp.mean(v, axis=-1, keepdims=True)
    vc = v - mu
    var = jnp.mean(vc * vc, axis=-1, keepdims=True)
    return vc * lax.rsqrt(var + LN_EPS) * g + b


def _gelu(v):
    return 0.5 * v * (1.0 + lax.erf(v * (1.0 / math.sqrt(2.0))))


def _in_proj_kernel(x_ref, w_ref, u_ref, q_ref, k_ref, v_ref, qi_ref, kw_ref, *, n_batch, widths):
    b = pl.program_id(1)
    xb = x_ref[...].astype(BF16)
    ts = xb.shape[0]
    starts = [sum(widths[:n]) for n in range(len(widths))]
    proj = lambda n: jnp.dot(xb, w_ref[:, starts[n]:starts[n] + widths[n]],
                             preferred_element_type=F32)
    u = proj(0)
    for s in range(u.shape[1] // LANES):
        u_ref[s, pl.ds(b, ts, stride=n_batch), :] = u[:, s * LANES:(s + 1) * LANES]
    q_ref[...] = proj(1).astype(BF16)
    k_ref[...] = proj(2).astype(BF16)
    v_ref[...] = proj(3).T.astype(BF16)
    qi_ref[...] = proj(4).astype(BF16)
    kw_ref[...] = proj(5)


def _in_proj(x, w, widths, *, ts):
    n_batch, seq, d = x.shape
    w_u, w_q, w_k, w_v, w_qi, w_kw = widths
    n_slabs = w_u // LANES
    row = lambda wd: pl.BlockSpec((None, ts, wd), lambda i, b: (b, i, 0))
    out_shape = (
        jax.ShapeDtypeStruct((n_slabs, seq * n_batch, LANES), F32),
        jax.ShapeDtypeStruct((n_batch, seq, w_q), BF16),
        jax.ShapeDtypeStruct((n_batch, seq, w_k), BF16),
        jax.ShapeDtypeStruct((n_batch, seq // ts, w_v, ts), BF16),
        jax.ShapeDtypeStruct((n_batch, seq, w_qi), BF16),
        jax.ShapeDtypeStruct((n_batch, seq, w_kw), F32),
    )
    return pl.pallas_call(
        functools.partial(_in_proj_kernel, n_batch=n_batch, widths=widths),
        grid=(seq // ts, n_batch),
        in_specs=[row(d), _const_spec(w.shape)],
        out_specs=(
            pl.BlockSpec((n_slabs, ts * n_batch, LANES), lambda i, b: (0, i, 0)),
            row(w_q), row(w_k),
            pl.BlockSpec((None, None, w_v, ts), lambda i, b: (b, i, 0, 0)),
            row(w_qi), row(w_kw),
        ),
        out_shape=out_shape,
        compiler_params=_params("arbitrary", "arbitrary"),
        name="in_proj",
    )(x, w)


def _s5_discretize_kernel(a_re_ref, a_im_ref, log_dt_ref, bt_re_ref, bt_im_ref,
                          abar_re_ref, abar_im_ref, bbar_re_ref, bbar_im_ref):
    a_re = a_re_ref[...]
    a_im = a_im_ref[...]
    dt = jnp.exp(log_dt_ref[...])
    mag = jnp.exp(dt * a_re)
    ang = dt * a_im
    abar_re = mag * jnp.cos(ang)
    abar_im = mag * jnp.sin(ang)
    num_re = abar_re - 1.0
    num_im = abar_im
    den = a_re * a_re + a_im * a_im
    f_re = (num_re * a_re + num_im * a_im) / den
    f_im = (num_im * a_re - num_re * a_im) / den
    abar_re_ref[...] = abar_re
    abar_im_ref[...] = abar_im
    bt_re = bt_re_ref[...]
    bt_im = bt_im_ref[...]
    bbar_re_ref[...] = f_re * bt_re - f_im * bt_im
    bbar_im_ref[...] = f_re * bt_im + f_im * bt_re


def _s5_discretize(a_re, a_im, log_dt, bt_re, bt_im):
    g, p = a_re.shape
    abar_re, abar_im, bbar_re, bbar_im = pl.pallas_call(
        _s5_discretize_kernel,
        out_shape=(jax.ShapeDtypeStruct((g, 1, p), F32), jax.ShapeDtypeStruct((g, 1, p), F32),
                   jax.ShapeDtypeStruct(bt_re.shape, F32), jax.ShapeDtypeStruct(bt_re.shape, F32)),
        name="s5_discretize",
    )(a_re.reshape(g, 1, p), a_im.reshape(g, 1, p), log_dt.reshape(g, 1, 1), bt_re, bt_im)
    return abar_re.reshape(g, p), abar_im.reshape(g, p), bbar_re, bbar_im


def _block_diag_slabs(m):
    n, gl, a, b = m.shape
    eye = jnp.eye(gl, dtype=m.dtype)
    full = m[:, :, :, None, :] * eye[None, :, None, :, None]
    return full.reshape(n, gl * a, gl * b)


def _s5_scan_kernel(u_next_ref, u_prev_ref, bbd_ref, ar_ref, ai_ref, cbd_ref, d_ref, y_ref,
                    buf_a, buf_b, carry_ref, *, ts, n_batch, tiles_per_slab, n_tiles):
    n = pl.program_id(0)
    last = n_tiles - 1
    cur = jnp.minimum(n, last)
    slab_next = jnp.minimum(n + 1, last) // tiles_per_slab
    slab_cur = cur // tiles_per_slab
    slab_prev = jnp.maximum(n - 1, 0) // tiles_per_slab
    half = ar_ref.shape[-1]

    @pl.when(n == 0)
    def _():
        buf_a[...] = jnp.dot(u_prev_ref[...].astype(BF16), bbd_ref[0], preferred_element_type=F32)
        buf_b[...] = jnp.zeros_like(buf_b)
        carry_ref[...] = jnp.zeros_like(carry_ref)

    def stages(cur_buf, other_buf):
        y = jnp.dot(other_buf[...].astype(BF16), cbd_ref[slab_prev], preferred_element_type=F32)
        y_ref[...] = _gelu(y + d_ref[slab_prev] * u_prev_ref[...])
        other_buf[...] = jnp.dot(u_next_ref[...].astype(BF16), bbd_ref[slab_next],
                                 preferred_element_type=F32)
        ar = ar_ref[slab_cur]
        ai = ai_ref[slab_cur]
        first = cur % tiles_per_slab == 0
        re = jnp.where(first, 0.0, carry_ref[:, :half])
        im = jnp.where(first, 0.0, carry_ref[:, half:])
        for t in range(ts):
            rows = slice(t * n_batch, (t + 1) * n_batch)
            bu = cur_buf[rows, :]
            re, im = ar * re - ai * im + bu[:, :half], ar * im + ai * re + bu[:, half:]
            cur_buf[rows, :] = jnp.concatenate([re, im], axis=-1)
        carry_ref[...] = jnp.concatenate([re, im], axis=-1)

    @pl.when(n % 2 == 0)
    def _():
        stages(buf_a, buf_b)

    @pl.when(n % 2 == 1)
    def _():
        stages(buf_b, buf_a)


def _s5_scan(u_tm, bbd, ar, ai, cbd, d, *, ts, n_batch):
    n_slabs, rows, _ = u_tm.shape
    tr = ts * n_batch
    tiles_per_slab = rows // tr
    n_tiles = n_slabs * tiles_per_slab
    n_state2 = bbd.shape[-1]
    tile = lambda shift: pl.BlockSpec(
        (tr, LANES), lambda n: (jnp.clip(n + shift, 0, n_tiles - 1), 0))
    u_flat = u_tm.reshape(n_slabs * rows, LANES)
    y = pl.pallas_call(
        functools.partial(_s5_scan_kernel, ts=ts, n_batch=n_batch,
                          tiles_per_slab=tiles_per_slab, n_tiles=n_tiles),
        grid=(n_tiles + 1,),
        in_specs=[tile(1), tile(-1), _const_spec(bbd.shape), _const_spec(ar.shape),
                  _const_spec(ai.shape), _const_spec(cbd.shape), _const_spec(d.shape)],
        out_specs=tile(-1),
        out_shape=jax.ShapeDtypeStruct(u_flat.shape, F32),
        scratch_shapes=[pltpu.VMEM((tr, n_state2), F32), pltpu.VMEM((tr, n_state2), F32),
                        pltpu.VMEM((n_batch, n_state2), F32)],
        compiler_params=_params("arbitrary"),
        name="s5_scan",
    )(u_flat, u_flat, bbd, ar, ai, cbd, d)
    return y.reshape(u_tm.shape)


def _glu_kernel(y_ref, w_ref, b_ref, o_ref, *, ts, n_batch):
    n_slabs = y_ref.shape[0]
    w = w_ref[...]
    bias = b_ref[...]
    for b in range(n_batch):
        yb = jnp.concatenate(
            [y_ref[s, pl.ds(b, ts, stride=n_batch), :] for s in range(n_slabs)], axis=-1)
        z = jnp.dot(yb.astype(BF16), w, preferred_element_type=F32) + bias
        o_ref[b] = (yb * jax.nn.sigmoid(z)).astype(BF16)


def _glu(y_tm, w_glu, b_glu, *, ts, n_batch):
    n_slabs, rows, _ = y_tm.shape
    seq = rows // n_batch
    width = n_slabs * LANES
    return pl.pallas_call(
        functools.partial(_glu_kernel, ts=ts, n_batch=n_batch),
        grid=(seq // ts,),
        in_specs=[
            pl.BlockSpec((n_slabs, ts * n_batch, LANES), lambda i: (0, i, 0)),
            _const_spec(w_glu.shape),
            _const_spec((1, width)),
        ],
        out_specs=pl.BlockSpec((n_batch, ts, width), lambda i: (0, i, 0)),
        out_shape=jax.ShapeDtypeStruct((n_batch, seq, width), BF16),
        compiler_params=_params("arbitrary"),
        name="glu",
    )(y_tm, w_glu, b_glu.reshape(1, width))


def _attn_kernel(qi_ref, kwq_ref, kwf_ref, q_ref, k_ref, vt_ref, *rest,
                 n_cast, tq, n_keep, idx_w_scale, qk_scale):
    cast_src, o_ref, cast_dst = rest[:n_cast], rest[n_cast], rest[n_cast + 1:2 * n_cast + 1]
    sc_ref, lg_ref, bias_ref, q4_ref, mx_ref, m_ref, ls_ref, ot_ref = rest[2 * n_cast + 1:]
    for src, dst in zip(cast_src, cast_dst):
        dst[...] = src[...].astype(BF16)

    i = pl.program_id(1)
    n_chunks = i + 1
    nt = (((1,), (1,)), ((), ()))
    fold = lambda a: a.reshape(tq // ACC_ROWS, ACC_ROWS, tq)

    key_pos = lax.broadcasted_iota(I32, (tq, tq), 0)
    qry_pos = lax.broadcasted_iota(I32, (tq, tq), 1)
    wi_t = kwq_ref[...].T[IDX_DIM:IDX_DIM + N_IDX_HEADS, :] * idx_w_scale

    def score_chunk(j):
        off = pl.multiple_of(j * tq, tq)
        kic = kwf_ref[pl.ds(off, tq), :][:, :IDX_DIM].astype(BF16)
        s = jnp.zeros((tq, tq), F32)
        for h in range(N_IDX_HEADS):
            r = lax.dot_general(kic, qi_ref[:, h * IDX_DIM:(h + 1) * IDX_DIM], nt,
                                preferred_element_type=F32)
            s = s + wi_t[h:h + 1, :] * jnp.maximum(r, 0.0)
        causal = (j - i) * tq + key_pos <= qry_pos
        sc_ref[j] = jnp.where(causal, s, -jnp.inf)

    _for_each_chunk(n_chunks, score_chunk)

    def decode(key):
        return pltpu.bitcast(jnp.where(key < 0, key ^ jnp.int32(0x7FFFFFFF), key), F32)

    def count_ge(thr, strict=False):
        thr_b = jnp.broadcast_to(thr, (ACC_ROWS, tq))

        def body(j, acc):
            s = fold(sc_ref[j])
            hit = s > thr_b if strict else s >= thr_b
            return acc + jnp.sum(jnp.where(hit, 1.0, 0.0), axis=0)

        acc = lax.fori_loop(0, n_chunks, body, jnp.zeros((ACC_ROWS, tq), F32))
        return jnp.sum(acc, axis=0, keepdims=True)

    keep = jnp.float32(n_keep)
    cnt0 = count_ge(jnp.zeros((1, tq), F32))
    tau = jnp.where(cnt0 >= keep, jnp.int32(0), jnp.int32(INT_MIN))
    cnt = jnp.where(cnt0 >= keep, cnt0, 0.0)

    def bit_body(bi, carry):
        tau, cnt = carry
        cand = tau + jnp.left_shift(jnp.int32(1), jnp.int32(30) - bi)
        c = count_ge(decode(cand))
        return jnp.where(c >= keep, cand, tau), jnp.where(c >= keep, c, cnt)

    tau, cnt = lax.fori_loop(0, 31, bit_body, (tau, cnt))
    thr = jnp.where(tau == jnp.int32(INT_MIN), jnp.finfo(F32).min, decode(tau))

    @pl.when(jnp.max(cnt) > keep)
    def _():
        need = keep - count_ge(thr, strict=True)
        lower = jnp.where(key_pos >= qry_pos, 1.0, 0.0).astype(BF16)

        def tie_chunk(j, run):
            s = sc_ref[j]
            eq = s == thr
            seen = run + jnp.dot(lower, jnp.where(eq, 1.0, 0.0).astype(BF16),
                                 preferred_element_type=F32)
            sc_ref[j] = jnp.where(eq, jnp.where(seen > need, -jnp.inf, s), s)
            return seen[tq - 1:tq, :]

        lax.fori_loop(0, n_chunks, tie_chunk, jnp.zeros((1, tq), F32))

    def mask_chunk(j, carry):
        sc_ref[j] = jnp.where(sc_ref[j] >= thr, 0.0, -1e30)
        return carry

    lax.fori_loop(0, n_chunks, mask_chunk, 0)

    slopes = [2.0 ** (-8.0 * (hd + 1) / N_HEADS) for hd in range(N_HEADS)]
    for hd in range(N_HEADS):
        c, g = divmod(hd, GQA_GROUP)
        q4_ref[c, g * tq:(g + 1) * tq, :] = q_ref[:, hd * HEAD_DIM:(hd + 1) * HEAD_DIM]
        bias_ref[hd] = slopes[hd] * key_pos.astype(F32)
    mx_ref[...] = jnp.full(mx_ref.shape, -jnp.inf, F32)
    ls_ref[...] = jnp.zeros_like(ls_ref)
    ot_ref[...] = jnp.zeros_like(ot_ref)

    def chunk_shift(j, hd):
        return ((j - i) * tq).astype(F32) * slopes[hd]

    def logits_chunk(j):
        off = pl.multiple_of(j * tq, tq)
        mask = sc_ref[j]
        for c in range(N_KV_HEADS):
            kc = k_ref[pl.ds(off, tq), :][:, c * HEAD_DIM:(c + 1) * HEAD_DIM]
            lg4 = lax.dot_general(kc, q4_ref[c], nt, preferred_element_type=F32) * qk_scale
            for g in range(GQA_GROUP):
                hd = c * GQA_GROUP + g
                lg = lg4[:, g * tq:(g + 1) * tq] + bias_ref[hd] + mask
                lg_ref[j, hd] = lg
                mx_ref[hd] = jnp.maximum(mx_ref[hd], jnp.max(fold(lg), axis=0) + chunk_shift(j, hd))

    _for_each_chunk(n_chunks, logits_chunk)
    for hd in range(N_HEADS):
        m_ref[hd] = jnp.max(mx_ref[hd], axis=0, keepdims=True)

    def pv_chunk(j):
        for c in range(N_KV_HEADS):
            vt = vt_ref[j, c * HEAD_DIM:(c + 1) * HEAD_DIM, :]
            for g in range(GQA_GROUP):
                hd = c * GQA_GROUP + g
                p = jnp.exp(lg_ref[j, hd] - (m_ref[hd] - chunk_shift(j, hd)))
                ls_ref[hd] += jnp.sum(fold(p), axis=0)
                ot_ref[hd] += jnp.dot(vt, p.astype(BF16), preferred_element_type=F32)

    _for_each_chunk(n_chunks, pv_chunk)
    for hd in range(N_HEADS):
        denom = jnp.sum(ls_ref[hd], axis=0, keepdims=True)
        o_ref[:, hd * HEAD_DIM:(hd + 1) * HEAD_DIM] = (ot_ref[hd] / denom).T.astype(BF16)


def _for_each_chunk(n_chunks, fn):
    def pair(p, carry):
        fn(2 * p)
        fn(2 * p + 1)
        return carry

    lax.fori_loop(0, n_chunks // 2, pair, 0)

    @pl.when(n_chunks % 2 == 1)
    def _():
        fn(n_chunks - 1)


def _cast_block_rows(n_rows, n_steps):
    bf16_rows = 2 * SUBLANES
    for rows in range(bf16_rows, n_rows + 1, bf16_rows):
        if n_rows % rows == 0 and n_rows // rows <= n_steps:
            return rows
    raise ValueError(f"no row block for {n_rows} rows in {n_steps} steps")


def _attention(qi, kw, q, k, vt, cast_weights):
    n_batch, seq, _ = q.shape
    n_chunks, kv_w, tq = vt.shape[1:]
    n_keep = min(INDEX_TOPK, seq // 4)
    blk = lambda w: pl.BlockSpec((None, tq, w), lambda b, i: (b, i, 0))
    whole = lambda w: pl.BlockSpec((None, seq, w), lambda b, i: (b, 0, 0))

    def cast_spec(a):
        rows = _cast_block_rows(a.shape[0], n_batch * n_chunks)
        last = a.shape[0] // rows - 1
        return pl.BlockSpec((rows, a.shape[1]),
                            lambda b, i: (jnp.minimum(b * n_chunks + i, last), 0))

    cast_specs = [cast_spec(a) for a in cast_weights]
    out = pl.pallas_call(
        functools.partial(_attn_kernel, n_cast=len(cast_weights), tq=tq, n_keep=n_keep,
                          idx_w_scale=(N_IDX_HEADS ** -0.5) * (IDX_DIM ** -0.5),
                          qk_scale=HEAD_DIM ** -0.5),
        grid=(n_batch, n_chunks),
        in_specs=[blk(qi.shape[2]), blk(kw.shape[2]), whole(kw.shape[2]), blk(q.shape[2]),
                  whole(k.shape[2]),
                  pl.BlockSpec((None, n_chunks, kv_w, tq), lambda b, i: (b, 0, 0, 0))] + cast_specs,
        out_specs=[blk(q.shape[2])] + cast_specs,
        out_shape=[jax.ShapeDtypeStruct(q.shape, BF16)]
        + [jax.ShapeDtypeStruct(a.shape, BF16) for a in cast_weights],
        scratch_shapes=[pltpu.VMEM((n_chunks, tq, tq), F32),
                        pltpu.VMEM((n_chunks, N_HEADS, tq, tq), F32),
                        pltpu.VMEM((N_HEADS, tq, tq), F32),
                        pltpu.VMEM((N_KV_HEADS, GQA_GROUP * tq, HEAD_DIM), BF16),
                        pltpu.VMEM((N_HEADS, ACC_ROWS, tq), F32),
                        pltpu.VMEM((N_HEADS, 1, tq), F32),
                        pltpu.VMEM((N_HEADS, ACC_ROWS, tq), F32),
                        pltpu.VMEM((N_HEADS, HEAD_DIM, tq), F32)],
        compiler_params=_params("arbitrary", "arbitrary"),
        name="sparse_attn",
    )(qi, kw, kw, q, k, vt, *cast_weights)
    return out[0], out[1:]


def _out_proj_kernel(ys_ref, ya_ref, x_ref, w_ref, g_ref, b_ref, o_ref):
    ssm_w = ys_ref.shape[1]
    for r in range(0, x_ref.shape[0], LN_ROWS):
        rows = slice(r, r + LN_ROWS)
        mix = jnp.dot(ys_ref[rows, :], w_ref[:ssm_w, :], preferred_element_type=F32)
        mix = mix + jnp.dot(ya_ref[rows, :], w_ref[ssm_w:, :], preferred_element_type=F32)
        o_ref[rows, :] = _layer_norm(DEEPNORM_ALPHA * x_ref[rows, :] + mix, g_ref[...], b_ref[...])


def _out_proj(ys, ya, x, w, g, b, *, tm):
    m, d = x.shape
    blk = lambda wd: pl.BlockSpec((tm, wd), lambda i: (i, 0))
    return pl.pallas_call(
        _out_proj_kernel,
        grid=(m // tm,),
        in_specs=[blk(ys.shape[1]), blk(ya.shape[1]), blk(d), _const_spec(w.shape),
                  _const_spec((1, d)), _const_spec((1, d))],
        out_specs=blk(d),
        out_shape=jax.ShapeDtypeStruct((m, d), F32),
        compiler_params=_params("arbitrary"),
        name="out_proj_ln1",
    )(ys, ya, x, w, g.reshape(1, d), b.reshape(1, d))


def _ffn_kernel(h_ref, halo_ref, wup_ref, wgate_ref, wdown_ref, cw_ref, cb_ref, g_ref, b_ref,
                o_ref, hb_ref, acc_ref, fin_ref, *, last_width, n_blocks, blocks_per_seq):
    n = pl.program_id(0)
    j = pl.program_id(1)
    tm = h_ref.shape[0]
    tf = wup_ref.shape[1]
    is_last = j == pl.num_programs(1) - 1

    real = n < n_blocks

    @pl.when(jnp.logical_and(n == 0, j == 0))
    def _():
        fin_ref[...] = jnp.zeros_like(fin_ref)

    def chunk(width, first, last):
        if first:
            seq_start = n % blocks_per_seq == 0
            hb_ref[:HALO_ROWS, :] = jnp.where(seq_start, 0.0, halo_ref[...]).astype(BF16)
            hb_ref[HALO_ROWS:, :] = h_ref[...].astype(BF16)
        ext = jnp.dot(hb_ref[...], wup_ref[:, :width], preferred_element_type=F32)
        cw = cw_ref[:, :width]
        hc = cb_ref[:, :width]
        for lag in range(CONV_WIDTH):
            start = HALO_ROWS - lag
            hc = hc + cw[CONV_WIDTH - 1 - lag:CONV_WIDTH - lag, :] * ext[start:start + tm, :]
        gate = jnp.dot(hb_ref[HALO_ROWS:, :], wgate_ref[:, :width], preferred_element_type=F32)
        act = (_gelu(hc) * gate).astype(BF16)
        if first:
            o_ref[...] = _layer_norm(fin_ref[...], g_ref[...], b_ref[...])
        down = jnp.dot(act, wdown_ref[:width, :], preferred_element_type=F32)
        base = DEEPNORM_ALPHA * h_ref[...] if first else acc_ref[...]
        dst = fin_ref if last else acc_ref
        dst[...] = base + down

    middle = jnp.logical_and(j > 0, jnp.logical_not(is_last))
    pl.when(jnp.logical_and(real, j == 0))(functools.partial(chunk, tf, True, False))
    pl.when(jnp.logical_and(real, middle))(functools.partial(chunk, tf, False, False))
    pl.when(jnp.logical_and(real, is_last))(functools.partial(chunk, last_width, False, True))

    @pl.when(jnp.logical_and(jnp.logical_not(real), j == 0))
    def _():
        o_ref[...] = _layer_norm(fin_ref[...], g_ref[...], b_ref[...])


def _ffn(h, w_up, w_gate, w_down, conv_w, conv_b, g, b, *, tm, tf):
    n_batch, seq, d = h.shape
    f = w_up.shape[1]
    n_chunks = -(-f // tf)
    assert n_chunks > 1
    blocks_per_seq = seq // tm
    n_blocks = n_batch * blocks_per_seq
    halo_blocks = tm // HALO_ROWS
    h2 = h.reshape(n_batch * seq, d)
    row_block = lambda n: jnp.minimum(n, n_blocks - 1)
    chunk_block = lambda n, j: jnp.where(n == n_blocks, n_chunks - 1, j)
    out = pl.pallas_call(
        functools.partial(_ffn_kernel, last_width=f - (n_chunks - 1) * tf, n_blocks=n_blocks,
                          blocks_per_seq=blocks_per_seq),
        grid=(n_blocks + 1, n_chunks),
        in_specs=[
            pl.BlockSpec((tm, d), lambda n, j: (row_block(n), 0)),
            pl.BlockSpec((HALO_ROWS, d),
                         lambda n, j: (jnp.maximum(row_block(n) * halo_blocks - 1, 0), 0)),
            pl.BlockSpec((d, tf), lambda n, j: (0, chunk_block(n, j))),
            pl.BlockSpec((d, tf), lambda n, j: (0, chunk_block(n, j))),
            pl.BlockSpec((tf, d), lambda n, j: (chunk_block(n, j), 0)),
            pl.BlockSpec((CONV_WIDTH, tf), lambda n, j: (0, chunk_block(n, j))),
            pl.BlockSpec((1, tf), lambda n, j: (0, chunk_block(n, j))),
            _const_spec((1, d)),
            _const_spec((1, d)),
        ],
        out_specs=pl.BlockSpec((tm, d), lambda n, j: (jnp.maximum(n - 1, 0), 0)),
        out_shape=jax.ShapeDtypeStruct((n_batch * seq, d), F32),
        scratch_shapes=[pltpu.VMEM((HALO_ROWS + tm, d), BF16), pltpu.VMEM((tm, d), F32),
                        pltpu.VMEM((tm, d), F32)],
        compiler_params=_params("arbitrary", "arbitrary"),
        name="ffn_ln2",
    )(h2, h2, w_up, w_gate, w_down, conv_w, conv_b.reshape(1, f), g.reshape(1, d), b.reshape(1, d))
    return out.reshape(n_batch, seq, d)


def _layer(h, w_in, a_re, a_im, log_dt, b_re, b_im, c_re, c_im, ssm_d, w_glu, b_glu, w_out,
           ln1_g, ln1_b, w_up, w_gate, conv_w, conv_b, w_down, ln2_g, ln2_b):
    n_batch, seq, d = h.shape
    assert n_batch == SUBLANES, "the time-major scan layout puts the batch on the sublanes"
    n_groups, n_state = a_re.shape
    ssm_w = n_groups * SSM_GROUP_CH
    attn_w = N_HEADS * HEAD_DIM
    kv_w = N_KV_HEADS * HEAD_DIM
    qi_w = N_IDX_HEADS * IDX_DIM
    n_slabs = ssm_w // LANES
    assert n_state == SSM_STATE and ssm_w % LANES == 0
    t_seq, t_row, t_ff, t_scan = _tiles(seq, w_up.shape[1])

    widths = (ssm_w, attn_w, kv_w, kv_w, qi_w, LANES)
    assert w_in.shape[1] == sum(widths[:-1]) + IDX_DIM + N_IDX_HEADS
    w_in_b = jnp.pad(w_in, ((0, 0), (0, sum(widths) - w_in.shape[1]))).astype(BF16)
    u_tm, q, k, vt, qi, kw = _in_proj(h, w_in_b, widths, ts=t_seq)

    abar_re, abar_im, bbar_re, bbar_im = _s5_discretize(
        a_re, a_im, log_dt, jnp.swapaxes(b_re, 1, 2), jnp.swapaxes(b_im, 1, 2))
    gl = GROUPS_PER_SLAB
    to_slabs = lambda m: m.reshape(n_slabs, gl, *m.shape[1:])
    bbd = jnp.concatenate([_block_diag_slabs(to_slabs(bbar_re)),
                           _block_diag_slabs(to_slabs(bbar_im))], axis=-1).astype(BF16)
    cbd_re = _block_diag_slabs(to_slabs(jnp.swapaxes(c_re, 1, 2)))
    cbd_im = _block_diag_slabs(to_slabs(jnp.swapaxes(c_im, 1, 2)))
    cbd = jnp.concatenate([cbd_re, -cbd_im], axis=1).astype(BF16)
    bcast = lambda m: jnp.broadcast_to(m.reshape(n_slabs, 1, gl * n_state), (n_slabs, n_batch, gl * n_state))
    y_tm = _s5_scan(u_tm, bbd, bcast(abar_re), bcast(abar_im), cbd,
                    ssm_d.reshape(n_slabs, 1, LANES), ts=t_scan, n_batch=n_batch)
    y_ssm = _glu(y_tm, w_glu.astype(BF16), b_glu, ts=t_seq, n_batch=n_batch)

    y_attn, (w_out_b, w_up_b, w_gate_b, w_down_b) = _attention(
        qi, kw, q, k, vt, (w_out, w_up, w_gate, w_down))

    h1 = _out_proj(y_ssm.reshape(n_batch * seq, ssm_w), y_attn.reshape(n_batch * seq, attn_w),
                   h.reshape(n_batch * seq, d), w_out_b, ln1_g, ln1_b, tm=t_row)
    return _ffn(h1.reshape(n_batch, seq, d), w_up_b, w_gate_b, w_down_b, conv_w, conv_b,
                ln2_g, ln2_b, tm=t_row, tf=t_ff)


def kernel(x, w_in, ssm_a_re, ssm_a_im, ssm_log_dt, ssm_b_re, ssm_b_im, ssm_c_re, ssm_c_im, ssm_d,
           w_glu, b_glu, w_out, ln1_g, ln1_b, w_up, w_gate, conv_w, conv_b, w_down, ln2_g, ln2_b):
    h = x
    for l in range(w_in.shape[0]):
        h = _layer(h, w_in[l], ssm_a_re[l], ssm_a_im[l], ssm_log_dt[l], ssm_b_re[l], ssm_b_im[l],
                   ssm_c_re[l], ssm_c_im[l], ssm_d[l], w_glu[l], b_glu[l], w_out[l], ln1_g[l],
                   ln1_b[l], w_up[l], w_gate[l], conv_w[l], conv_b[l], w_down[l], ln2_g[l], ln2_b[l])
    return h
```

```python
import functools
import math

import jax
import jax.numpy as jnp
from jax import lax
from jax.experimental import pallas as pl
from jax.experimental.pallas import tpu as pltpu

F32 = jnp.float32
BF16 = jnp.bfloat16
I32 = jnp.int32

LANES = 128
SUBLANES = 8
VMEM_LIMIT_BYTES = 56 * 1024 * 1024

SSM_GROUP_CH = 16
SSM_STATE = 64
N_HEADS = 8
HEAD_DIM = 128
N_KV_HEADS = 2
GQA_GROUP = N_HEADS // N_KV_HEADS
N_IDX_HEADS = 16
IDX_DIM = 64
INDEX_TOPK = 256
CONV_WIDTH = 3
LN_EPS = 1e-5
DEPTH = 1
DEEPNORM_ALPHA = (2.0 * DEPTH) ** 0.25

GROUPS_PER_SLAB = LANES // SSM_GROUP_CH
SLAB_STATE = GROUPS_PER_SLAB * SSM_STATE
INT_MIN = -(2 ** 31)
ACC_ROWS = 4 * SUBLANES
LN_ROWS = 128
HALO_ROWS = 2 * SUBLANES


def _tiles(seq, d_ff):
    t_seq = min(256, seq)
    t_row = min(512, seq)
    t_ff = min(1024, d_ff)
    t_scan = min(256, seq)
    assert seq % t_seq == 0 and seq % t_row == 0 and d_ff % LANES == 0 and seq % t_scan == 0
    return t_seq, t_row, t_ff, t_scan


def _params(*sem):
    return pltpu.CompilerParams(dimension_semantics=sem, vmem_limit_bytes=VMEM_LIMIT_BYTES)


def _const_spec(shape):
    zeros = (0,) * len(shape)
    return pl.BlockSpec(shape, lambda *_: zeros, pipeline_mode=pl.Buffered(1))


def _layer_norm(v, g, b):
    mu = jnp.mean(v, axis=-1, keepdims=True)
    vc = v - mu
    var = jnp.mean(vc * vc, axis=-1, keepdims=True)
    return vc * lax.rsqrt(var + LN_EPS) * g + b


def _gelu(v):
    return 0.5 * v * (1.0 + lax.erf(v * (1.0 / math.sqrt(2.0))))


def _in_proj_kernel(x_ref, w_ref, u_ref, q_ref, k_ref, v_ref, qi_ref, kw_ref, *, n_batch, widths):
    b = pl.program_id(1)
    xb = x_ref[...].astype(BF16)
    ts = xb.shape[0]
    starts = [sum(widths[:n]) for n in range(len(widths))]
    proj = lambda n: lax.dot_general(xb, w_ref[starts[n]:starts[n] + widths[n], :],
                                     (((1,), (1,)), ((), ())), preferred_element_type=F32)
    u = proj(0)
    for s in range(u.shape[1] // LANES):
        u_ref[s, pl.ds(b, ts, stride=n_batch), :] = u[:, s * LANES:(s + 1) * LANES]
    q_ref[...] = proj(1).astype(BF16)
    k_ref[...] = proj(2).astype(BF16)
    v_ref[...] = proj(3).T.astype(BF16)
    qi_ref[...] = proj(4).astype(BF16)
    kw_ref[...] = proj(5)


def _in_proj(x, w, widths, *, ts):
    n_batch, seq, d = x.shape
    w_u, w_q, w_k, w_v, w_qi, w_kw = widths
    n_slabs = w_u // LANES
    row = lambda wd: pl.BlockSpec((None, ts, wd), lambda i, b: (b, i, 0))
    out_shape = (
        jax.ShapeDtypeStruct((n_slabs, seq * n_batch, LANES), F32),
        jax.ShapeDtypeStruct((n_batch, seq, w_q), BF16),
        jax.ShapeDtypeStruct((n_batch, seq, w_k), BF16),
        jax.ShapeDtypeStruct((n_batch, seq // ts, w_v, ts), BF16),
        jax.ShapeDtypeStruct((n_batch, seq, w_qi), BF16),
        jax.ShapeDtypeStruct((n_batch, seq, w_kw), F32),
    )
    return pl.pallas_call(
        functools.partial(_in_proj_kernel, n_batch=n_batch, widths=widths),
        grid=(seq // ts, n_batch),
        in_specs=[row(d), _const_spec(w.shape)],
        out_specs=(
            pl.BlockSpec((n_slabs, ts * n_batch, LANES), lambda i, b: (0, i, 0)),
            row(w_q), row(w_k),
            pl.BlockSpec((None, None, w_v, ts), lambda i, b: (b, i, 0, 0)),
            row(w_qi), row(w_kw),
        ),
        out_shape=out_shape,
        compiler_params=_params("arbitrary", "arbitrary"),
        name="in_proj",
    )(x, w)


def _s5_discretize_kernel(a_re_ref, a_im_ref, log_dt_ref, bt_re_ref, bt_im_ref,
                          abar_re_ref, abar_im_ref, bbar_re_ref, bbar_im_ref):
    a_re = a_re_ref[...]
    a_im = a_im_ref[...]
    dt = jnp.exp(log_dt_ref[...])
    mag = jnp.exp(dt * a_re)
    ang = dt * a_im
    abar_re = mag * jnp.cos(ang)
    abar_im = mag * jnp.sin(ang)
    num_re = abar_re - 1.0
    num_im = abar_im
    den = a_re * a_re + a_im * a_im
    f_re = (num_re * a_re + num_im * a_im) / den
    f_im = (num_im * a_re - num_re * a_im) / den
    abar_re_ref[...] = abar_re
    abar_im_ref[...] = abar_im
    bt_re = bt_re_ref[...]
    bt_im = bt_im_ref[...]
    bbar_re_ref[...] = f_re * bt_re - f_im * bt_im
    bbar_im_ref[...] = f_re * bt_im + f_im * bt_re


def _s5_discretize(a_re, a_im, log_dt, bt_re, bt_im):
    g, p = a_re.shape
    abar_re, abar_im, bbar_re, bbar_im = pl.pallas_call(
        _s5_discretize_kernel,
        out_shape=(jax.ShapeDtypeStruct((g, 1, p), F32), jax.ShapeDtypeStruct((g, 1, p), F32),
                   jax.ShapeDtypeStruct(bt_re.shape, F32), jax.ShapeDtypeStruct(bt_re.shape, F32)),
        name="s5_discretize",
    )(a_re.reshape(g, 1, p), a_im.reshape(g, 1, p), log_dt.reshape(g, 1, 1), bt_re, bt_im)
    return abar_re.reshape(g, p), abar_im.reshape(g, p), bbar_re, bbar_im


def _block_diag_slabs(m):
    n, gl, a, b = m.shape
    eye = jnp.eye(gl, dtype=m.dtype)
    full = m[:, :, :, None, :] * eye[None, :, None, :, None]
    return full.reshape(n, gl * a, gl * b)


def _s5_scan_kernel(u_next_ref, u_prev_ref, bbd_ref, ar_ref, ai_ref, cbd_ref, d_ref, y_ref,
                    buf_a, buf_b, carry_ref, *, ts, n_batch, tiles_per_slab, n_tiles):
    n = pl.program_id(0)
    last = n_tiles - 1
    cur = jnp.minimum(n, last)
    slab_next = jnp.minimum(n + 1, last) // tiles_per_slab
    slab_cur = cur // tiles_per_slab
    slab_prev = jnp.maximum(n - 1, 0) // tiles_per_slab
    half = ar_ref.shape[-1]

    @pl.when(n == 0)
    def _():
        buf_a[...] = jnp.dot(u_prev_ref[...].astype(BF16), bbd_ref[0], preferred_element_type=F32)
        buf_b[...] = jnp.zeros_like(buf_b)
        carry_ref[...] = jnp.zeros_like(carry_ref)

    def stages(cur_buf, other_buf):
        y = jnp.dot(other_buf[...].astype(BF16), cbd_ref[slab_prev], preferred_element_type=F32)
        y_ref[...] = _gelu(y + d_ref[slab_prev] * u_prev_ref[...])
        other_buf[...] = jnp.dot(u_next_ref[...].astype(BF16), bbd_ref[slab_next],
                                 preferred_element_type=F32)
        ar = ar_ref[slab_cur]
        ai = ai_ref[slab_cur]
        first = cur % tiles_per_slab == 0
        re = jnp.where(first, 0.0, carry_ref[:, :half])
        im = jnp.where(first, 0.0, carry_ref[:, half:])
        for t in range(ts):
            rows = slice(t * n_batch, (t + 1) * n_batch)
            bu = cur_buf[rows, :]
            re, im = ar * re - ai * im + bu[:, :half], ar * im + ai * re + bu[:, half:]
            cur_buf[rows, :] = jnp.concatenate([re, im], axis=-1)
        carry_ref[...] = jnp.concatenate([re, im], axis=-1)

    @pl.when(n % 2 == 0)
    def _():
        stages(buf_a, buf_b)

    @pl.when(n % 2 == 1)
    def _():
        stages(buf_b, buf_a)


def _s5_scan(u_tm, bbd, ar, ai, cbd, d, *, ts, n_batch):
    n_slabs, rows, _ = u_tm.shape
    tr = ts * n_batch
    tiles_per_slab = rows // tr
    n_tiles = n_slabs * tiles_per_slab
    n_state2 = bbd.shape[-1]
    tile = lambda shift: pl.BlockSpec(
        (tr, LANES), lambda n: (jnp.clip(n + shift, 0, n_tiles - 1), 0))
    u_flat = u_tm.reshape(n_slabs * rows, LANES)
    y = pl.pallas_call(
        functools.partial(_s5_scan_kernel, ts=ts, n_batch=n_batch,
                          tiles_per_slab=tiles_per_slab, n_tiles=n_tiles),
        grid=(n_tiles + 1,),
        in_specs=[tile(1), tile(-1), _const_spec(bbd.shape), _const_spec(ar.shape),
                  _const_spec(ai.shape), _const_spec(cbd.shape), _const_spec(d.shape)],
        out_specs=tile(-1),
        out_shape=jax.ShapeDtypeStruct(u_flat.shape, F32),
        scratch_shapes=[pltpu.VMEM((tr, n_state2), F32), pltpu.VMEM((tr, n_state2), F32),
                        pltpu.VMEM((n_batch, n_state2), F32)],
        compiler_params=_params("arbitrary"),
        name="s5_scan",
    )(u_flat, u_flat, bbd, ar, ai, cbd, d)
    return y.reshape(u_tm.shape)


def _glu_kernel(y_ref, w_ref, b_ref, o_ref, *, ts, n_batch):
    n_slabs = y_ref.shape[0]
    w = w_ref[...]
    bias = b_ref[...]
    for b in range(n_batch):
        yb = jnp.concatenate(
            [y_ref[s, pl.ds(b, ts, stride=n_batch), :] for s in range(n_slabs)], axis=-1)
        z = jnp.dot(yb.astype(BF16), w, preferred_element_type=F32) + bias
        o_ref[b] = (yb * jax.nn.sigmoid(z)).astype(BF16)


def _glu(y_tm, w_glu, b_glu, *, ts, n_batch):
    n_slabs, rows, _ = y_tm.shape
    seq = rows // n_batch
    width = n_slabs * LANES
    return pl.pallas_call(
        functools.partial(_glu_kernel, ts=ts, n_batch=n_batch),
        grid=(seq // ts,),
        in_specs=[
            pl.BlockSpec((n_slabs, ts * n_batch, LANES), lambda i: (0, i, 0)),
            _const_spec(w_glu.shape),
            _const_spec((1, width)),
        ],
        out_specs=pl.BlockSpec((n_batch, ts, width), lambda i: (0, i, 0)),
        out_shape=jax.ShapeDtypeStruct((n_batch, seq, width), BF16),
        compiler_params=_params("arbitrary"),
        name="glu",
    )(y_tm, w_glu, b_glu.reshape(1, width))


def _attn_kernel(qi_ref, kwq_ref, kwf_ref, q_ref, k_ref, vt_ref, *rest,
                 n_cast, tq, n_keep, idx_w_scale, qk_scale):
    cast_src, o_ref, cast_dst = rest[:n_cast], rest[n_cast], rest[n_cast + 1:2 * n_cast + 1]
    sc_ref, lg_ref, bias_ref, q4_ref, mx_ref, m_ref, ls_ref, ot_ref = rest[2 * n_cast + 1:]
    for src, dst in zip(cast_src, cast_dst):
        dst[...] = src[...].astype(BF16)

    i = pl.program_id(1)
    n_chunks = i + 1
    nt = (((1,), (1,)), ((), ()))
    fold = lambda a: a.reshape(tq // ACC_ROWS, ACC_ROWS, tq)

    key_pos = lax.broadcasted_iota(I32, (tq, tq), 0)
    qry_pos = lax.broadcasted_iota(I32, (tq, tq), 1)
    wi_t = kwq_ref[...].T[IDX_DIM:IDX_DIM + N_IDX_HEADS, :] * idx_w_scale

    def score_chunk(j):
        off = pl.multiple_of(j * tq, tq)
        kic = kwf_ref[pl.ds(off, tq), :][:, :IDX_DIM].astype(BF16)
        s = jnp.zeros((tq, tq), F32)
        for h in range(N_IDX_HEADS):
            r = lax.dot_general(kic, qi_ref[:, h * IDX_DIM:(h + 1) * IDX_DIM], nt,
                                preferred_element_type=F32)
            s = s + wi_t[h:h + 1, :] * jnp.maximum(r, 0.0)
        causal = (j - i) * tq + key_pos <= qry_pos
        sc_ref[j] = jnp.where(causal, s, -jnp.inf)

    _for_each_chunk(n_chunks, score_chunk)

    def decode(key):
        return pltpu.bitcast(jnp.where(key < 0, key ^ jnp.int32(0x7FFFFFFF), key), F32)

    def count_ge(thr, strict=False):
        thr_b = jnp.broadcast_to(thr, (ACC_ROWS, tq))

        def body(j, acc):
            s = fold(sc_ref[j])
            hit = s > thr_b if strict else s >= thr_b
            return acc + jnp.sum(jnp.where(hit, 1.0, 0.0), axis=0)

        acc = lax.fori_loop(0, n_chunks, body, jnp.zeros((ACC_ROWS, tq), F32))
        return jnp.sum(acc, axis=0, keepdims=True)

    keep = jnp.float32(n_keep)
    cnt0 = count_ge(jnp.zeros((1, tq), F32))
    tau = jnp.where(cnt0 >= keep, jnp.int32(0), jnp.int32(INT_MIN))
    cnt = jnp.where(cnt0 >= keep, cnt0, 0.0)

    def bit_body(bi, carry):
        tau, cnt = carry
        cand = tau + jnp.left_shift(jnp.int32(1), jnp.int32(30) - bi)
        c = count_ge(decode(cand))
        return jnp.where(c >= keep, cand, tau), jnp.where(c >= keep, c, cnt)

    tau, cnt = lax.fori_loop(0, 31, bit_body, (tau, cnt))
    thr = jnp.where(tau == jnp.int32(INT_MIN), jnp.finfo(F32).min, decode(tau))

    @pl.when(jnp.max(cnt) > keep)
    def _():
        need = keep - count_ge(thr, strict=True)
        lower = jnp.where(key_pos >= qry_pos, 1.0, 0.0).astype(BF16)

        def tie_chunk(j, run):
            s = sc_ref[j]
            eq = s == thr
            seen = run + jnp.dot(lower, jnp.where(eq, 1.0, 0.0).astype(BF16),
                                 preferred_element_type=F32)
            sc_ref[j] = jnp.where(eq, jnp.where(seen > need, -jnp.inf, s), s)
            return seen[tq - 1:tq, :]

        lax.fori_loop(0, n_chunks, tie_chunk, jnp.zeros((1, tq), F32))

    def mask_chunk(j, carry):
        sc_ref[j] = jnp.where(sc_ref[j] >= thr, 0.0, -1e30)
        return carry

    lax.fori_loop(0, n_chunks, mask_chunk, 0)

    slopes = [2.0 ** (-8.0 * (hd + 1) / N_HEADS) for hd in range(N_HEADS)]
    for hd in range(N_HEADS):
        c, g = divmod(hd, GQA_GROUP)
        q4_ref[c, g * tq:(g + 1) * tq, :] = q_ref[:, hd * HEAD_DIM:(hd + 1) * HEAD_DIM]
        bias_ref[hd] = slopes[hd] * key_pos.astype(F32)
    mx_ref[...] = jnp.full(mx_ref.shape, -jnp.inf, F32)
    ls_ref[...] = jnp.zeros_like(ls_ref)
    ot_ref[...] = jnp.zeros_like(ot_ref)

    def chunk_shift(j, hd):
        return ((j - i) * tq).astype(F32) * slopes[hd]

    def logits_chunk(j):
        off = pl.multiple_of(j * tq, tq)
        mask = sc_ref[j]
        for c in range(N_KV_HEADS):
            kc = k_ref[pl.ds(off, tq), :][:, c * HEAD_DIM:(c + 1) * HEAD_DIM]
            lg4 = lax.dot_general(kc, q4_ref[c], nt, preferred_element_type=F32) * qk_scale
            for g in range(GQA_GROUP):
                hd = c * GQA_GROUP + g
                lg = lg4[:, g * tq:(g + 1) * tq] + bias_ref[hd] + mask
                lg_ref[j, hd] = lg
                mx_ref[hd] = jnp.maximum(mx_ref[hd], jnp.max(fold(lg), axis=0) + chunk_shift(j, hd))

    _for_each_chunk(n_chunks, logits_chunk)
    for hd in range(N_HEADS):
        m_ref[hd] = jnp.max(mx_ref[hd], axis=0, keepdims=True)

    def pv_chunk(j):
        for c in range(N_KV_HEADS):
            vt = vt_ref[j, c * HEAD_DIM:(c + 1) * HEAD_DIM, :]
            for g in range(GQA_GROUP):
                hd = c * GQA_GROUP + g
                p = jnp.exp(lg_ref[j, hd] - (m_ref[hd] - chunk_shift(j, hd)))
                ls_ref[hd] += jnp.sum(fold(p), axis=0)
                ot_ref[hd] += jnp.dot(vt, p.astype(BF16), preferred_element_type=F32)

    _for_each_chunk(n_chunks, pv_chunk)
    for hd in range(N_HEADS):
        denom = jnp.sum(ls_ref[hd], axis=0, keepdims=True)
        o_ref[:, hd * HEAD_DIM:(hd + 1) * HEAD_DIM] = (ot_ref[hd] / denom).T.astype(BF16)


def _for_each_chunk(n_chunks, fn):
    def pair(p, carry):
        fn(2 * p)
        fn(2 * p + 1)
        return carry

    lax.fori_loop(0, n_chunks // 2, pair, 0)

    @pl.when(n_chunks % 2 == 1)
    def _():
        fn(n_chunks - 1)


def _cast_block_rows(n_rows, n_steps):
    bf16_rows = 2 * SUBLANES
    for rows in range(bf16_rows, n_rows + 1, bf16_rows):
        if n_rows % rows == 0 and n_rows // rows <= n_steps:
            return rows
    raise ValueError(f"no row block for {n_rows} rows in {n_steps} steps")


def _attention(qi, kw, q, k, vt, cast_weights):
    n_batch, seq, _ = q.shape
    n_chunks, kv_w, tq = vt.shape[1:]
    n_keep = min(INDEX_TOPK, seq // 4)
    blk = lambda w: pl.BlockSpec((None, tq, w), lambda b, i: (b, i, 0))
    whole = lambda w: pl.BlockSpec((None, seq, w), lambda b, i: (b, 0, 0))

    def cast_spec(a):
        rows = _cast_block_rows(a.shape[0], n_batch * n_chunks)
        last = a.shape[0] // rows - 1
        return pl.BlockSpec((rows, a.shape[1]),
                            lambda b, i: (jnp.minimum(b * n_chunks + i, last), 0))

    cast_specs = [cast_spec(a) for a in cast_weights]
    out = pl.pallas_call(
        functools.partial(_attn_kernel, n_cast=len(cast_weights), tq=tq, n_keep=n_keep,
                          idx_w_scale=(N_IDX_HEADS ** -0.5) * (IDX_DIM ** -0.5),
                          qk_scale=HEAD_DIM ** -0.5),
        grid=(n_batch, n_chunks),
        in_specs=[blk(qi.shape[2]), blk(kw.shape[2]), whole(kw.shape[2]), blk(q.shape[2]),
                  whole(k.shape[2]),
                  pl.BlockSpec((None, n_chunks, kv_w, tq), lambda b, i: (b, 0, 0, 0))] + cast_specs,
        out_specs=[blk(q.shape[2])] + cast_specs,
        out_shape=[jax.ShapeDtypeStruct(q.shape, BF16)]
        + [jax.ShapeDtypeStruct(a.shape, BF16) for a in cast_weights],
        scratch_shapes=[pltpu.VMEM((n_chunks, tq, tq), F32),
                        pltpu.VMEM((n_chunks, N_HEADS, tq, tq), F32),
                        pltpu.VMEM((N_HEADS, tq, tq), F32),
                        pltpu.VMEM((N_KV_HEADS, GQA_GROUP * tq, HEAD_DIM), BF16),
                        pltpu.VMEM((N_HEADS, ACC_ROWS, tq), F32),
                        pltpu.VMEM((N_HEADS, 1, tq), F32),
                        pltpu.VMEM((N_HEADS, ACC_ROWS, tq), F32),
                        pltpu.VMEM((N_HEADS, HEAD_DIM, tq), F32)],
        compiler_params=_params("arbitrary", "arbitrary"),
        name="sparse_attn",
    )(qi, kw, kw, q, k, vt, *cast_weights)
    return out[0], out[1:]


def _out_proj_kernel(ys_ref, ya_ref, x_ref, w_ref, g_ref, b_ref, o_ref):
    ssm_w = ys_ref.shape[1]
    for r in range(0, x_ref.shape[0], LN_ROWS):
        rows = slice(r, r + LN_ROWS)
        mix = jnp.dot(ys_ref[rows, :], w_ref[:ssm_w, :], preferred_element_type=F32)
        mix = mix + jnp.dot(ya_ref[rows, :], w_ref[ssm_w:, :], preferred_element_type=F32)
        o_ref[rows, :] = _layer_norm(DEEPNORM_ALPHA * x_ref[rows, :] + mix, g_ref[...], b_ref[...])


def _out_proj(ys, ya, x, w, g, b, *, tm):
    m, d = x.shape
    blk = lambda wd: pl.BlockSpec((tm, wd), lambda i: (i, 0))
    return pl.pallas_call(
        _out_proj_kernel,
        grid=(m // tm,),
        in_specs=[blk(ys.shape[1]), blk(ya.shape[1]), blk(d), _const_spec(w.shape),
                  _const_spec((1, d)), _const_spec((1, d))],
        out_specs=blk(d),
        out_shape=jax.ShapeDtypeStruct((m, d), F32),
        compiler_params=_params("arbitrary"),
        name="out_proj_ln1",
    )(ys, ya, x, w, g.reshape(1, d), b.reshape(1, d))


def _ffn_kernel(h_ref, halo_ref, wup_ref, wgate_ref, wdown_ref, cw_ref, cb_ref, g_ref, b_ref,
                o_ref, hb_ref, acc_ref, *, last_width):
    i = pl.program_id(1)
    j = pl.program_id(2)
    tm = h_ref.shape[0]
    tf = wup_ref.shape[1]

    @pl.when(j == 0)
    def _():
        hb_ref[:HALO_ROWS, :] = jnp.where(i == 0, 0.0, halo_ref[...]).astype(BF16)
        hb_ref[HALO_ROWS:, :] = h_ref[...].astype(BF16)
        acc_ref[...] = jnp.zeros_like(acc_ref)

    def chunk(width):
        ext = jnp.dot(hb_ref[...], wup_ref[:, :width], preferred_element_type=F32)
        cw = cw_ref[:, :width]
        hc = cb_ref[:, :width]
        for lag in range(CONV_WIDTH):
            start = HALO_ROWS - lag
            hc = hc + cw[CONV_WIDTH - 1 - lag:CONV_WIDTH - lag, :] * ext[start:start + tm, :]
        gate = jnp.dot(hb_ref[HALO_ROWS:, :], wgate_ref[:, :width], preferred_element_type=F32)
        act = (_gelu(hc) * gate).astype(BF16)
        acc_ref[...] += jnp.dot(act, wdown_ref[:width, :], preferred_element_type=F32)

    is_last = j == pl.num_programs(2) - 1
    if last_width == tf:
        chunk(tf)
    else:
        pl.when(jnp.logical_not(is_last))(functools.partial(chunk, tf))
        pl.when(is_last)(functools.partial(chunk, last_width))

    @pl.when(is_last)
    def _():
        o_ref[...] = _layer_norm(DEEPNORM_ALPHA * h_ref[...] + acc_ref[...], g_ref[...], b_ref[...])


def _ffn(h, w_up, w_gate, w_down, conv_w, conv_b, g, b, *, tm, tf):
    n_batch, seq, d = h.shape
    f = w_up.shape[1]
    n_chunks = -(-f // tf)
    halo_blocks = tm // HALO_ROWS
    return pl.pallas_call(
        functools.partial(_ffn_kernel, last_width=f - (n_chunks - 1) * tf),
        grid=(n_batch, seq // tm, n_chunks),
        in_specs=[
            pl.BlockSpec((None, tm, d), lambda bb, i, j: (bb, i, 0)),
            pl.BlockSpec((None, HALO_ROWS, d),
                         lambda bb, i, j: (bb, jnp.maximum(i * halo_blocks - 1, 0), 0)),
            pl.BlockSpec((d, tf), lambda bb, i, j: (0, j)),
            pl.BlockSpec((d, tf), lambda bb, i, j: (0, j)),
            pl.BlockSpec((tf, d), lambda bb, i, j: (j, 0)),
            pl.BlockSpec((CONV_WIDTH, tf), lambda bb, i, j: (0, j)),
            pl.BlockSpec((1, tf), lambda bb, i, j: (0, j)),
            _const_spec((1, d)),
            _const_spec((1, d)),
        ],
        out_specs=pl.BlockSpec((None, tm, d), lambda bb, i, j: (bb, i, 0)),
        out_shape=jax.ShapeDtypeStruct((n_batch, seq, d), F32),
        scratch_shapes=[pltpu.VMEM((HALO_ROWS + tm, d), BF16), pltpu.VMEM((tm, d), F32)],
        compiler_params=_params("arbitrary", "arbitrary", "arbitrary"),
        name="ffn_ln2",
    )(h, h, w_up, w_gate, w_down, conv_w, conv_b.reshape(1, f), g.reshape(1, d), b.reshape(1, d))


def _layer(h, w_in, a_re, a_im, log_dt, b_re, b_im, c_re, c_im, ssm_d, w_glu, b_glu, w_out,
           ln1_g, ln1_b, w_up, w_gate, conv_w, conv_b, w_down, ln2_g, ln2_b):
    n_batch, seq, d = h.shape
    assert n_batch == SUBLANES, "the time-major scan layout puts the batch on the sublanes"
    n_groups, n_state = a_re.shape
    ssm_w = n_groups * SSM_GROUP_CH
    attn_w = N_HEADS * HEAD_DIM
    kv_w = N_KV_HEADS * HEAD_DIM
    qi_w = N_IDX_HEADS * IDX_DIM
    n_slabs = ssm_w // LANES
    assert n_state == SSM_STATE and ssm_w % LANES == 0
    t_seq, t_row, t_ff, t_scan = _tiles(seq, w_up.shape[1])

    widths = (ssm_w, attn_w, kv_w, kv_w, qi_w, LANES)
    assert w_in.shape[1] == sum(widths[:-1]) + IDX_DIM + N_IDX_HEADS
    w_in_b = jnp.pad(jnp.swapaxes(w_in, 0, 1),
                     ((0, sum(widths) - w_in.shape[1]), (0, 0))).astype(BF16)
    u_tm, q, k, vt, qi, kw = _in_proj(h, w_in_b, widths, ts=t_seq)

    abar_re, abar_im, bbar_re, bbar_im = _s5_discretize(
        a_re, a_im, log_dt, jnp.swapaxes(b_re, 1, 2), jnp.swapaxes(b_im, 1, 2))
    gl = GROUPS_PER_SLAB
    to_slabs = lambda m: m.reshape(n_slabs, gl, *m.shape[1:])
    bbd = jnp.concatenate([_block_diag_slabs(to_slabs(bbar_re)),
                           _block_diag_slabs(to_slabs(bbar_im))], axis=-1).astype(BF16)
    cbd_re = _block_diag_slabs(to_slabs(jnp.swapaxes(c_re, 1, 2)))
    cbd_im = _block_diag_slabs(to_slabs(jnp.swapaxes(c_im, 1, 2)))
    cbd = jnp.concatenate([cbd_re, -cbd_im], axis=1).astype(BF16)
    bcast = lambda m: jnp.broadcast_to(m.reshape(n_slabs, 1, gl * n_state), (n_slabs, n_batch, gl * n_state))
    y_tm = _s5_scan(u_tm, bbd, bcast(abar_re), bcast(abar_im), cbd,
                    ssm_d.reshape(n_slabs, 1, LANES), ts=t_scan, n_batch=n_batch)
    y_ssm = _glu(y_tm, w_glu.astype(BF16), b_glu, ts=t_seq, n_batch=n_batch)

    y_attn, (w_out_b, w_up_b, w_gate_b, w_down_b) = _attention(
        qi, kw, q, k, vt, (w_out, w_up, w_gate, w_down))

    h1 = _out_proj(y_ssm.reshape(n_batch * seq, ssm_w), y_attn.reshape(n_batch * seq, attn_w),
                   h.reshape(n_batch * seq, d), w_out_b, ln1_g, ln1_b, tm=t_row)
    return _ffn(h1.reshape(n_batch, seq, d), w_up_b, w_gate_b, w_down_b, conv_w, conv_b,
                ln2_g, ln2_b, tm=t_row, tf=t_ff)


def kernel(x, w_in, ssm_a_re, ssm_a_im, ssm_log_dt, ssm_b_re, ssm_b_im, ssm_c_re, ssm_c_im, ssm_d,
           w_glu, b_glu, w_out, ln1_g, ln1_b, w_up, w_gate, conv_w, conv_b, w_down, ln2_g, ln2_b):
    h = x
    for l in range(w_in.shape[0]):
        h = _layer(h, w_in[l], ssm_a_re[l], ssm_a_im[l], ssm_log_dt[l], ssm_b_re[l], ssm_b_im[l],
                   ssm_c_re[l], ssm_c_im[l], ssm_d[l], w_glu[l], b_glu[l], w_out[l], ln1_g[l],
                   ln1_b[l], w_up[l], w_gate[l], conv_w[l], conv_b[l], w_down[l], ln2_g[l], ln2_b[l])
    return h
```

```python
import functools
import math

import jax
import jax.numpy as jnp
from jax import lax
from jax.experimental import pallas as pl
from jax.experimental.pallas import tpu as pltpu

F32 = jnp.float32
BF16 = jnp.bfloat16
I32 = jnp.int32

LANES = 128
SUBLANES = 8
VMEM_LIMIT_BYTES = 56 * 1024 * 1024

SSM_GROUP_CH = 16
SSM_STATE = 64
N_HEADS = 8
HEAD_DIM = 128
N_KV_HEADS = 2
GQA_GROUP = N_HEADS // N_KV_HEADS
N_IDX_HEADS = 16
IDX_DIM = 64
INDEX_TOPK = 256
CONV_WIDTH = 3
LN_EPS = 1e-5
DEPTH = 1
DEEPNORM_ALPHA = (2.0 * DEPTH) ** 0.25

GROUPS_PER_SLAB = LANES // SSM_GROUP_CH
SLAB_STATE = GROUPS_PER_SLAB * SSM_STATE
INT_MIN = -(2 ** 31)
ACC_ROWS = 4 * SUBLANES
LN_ROWS = 128
HALO_ROWS = 2 * SUBLANES


def _tiles(seq, d_ff):
    t_seq = min(256, seq)
    t_row = min(512, seq)
    t_ff = min(1024, d_ff)
    t_scan = min(256, seq)
    assert seq % t_seq == 0 and seq % t_row == 0 and d_ff % LANES == 0 and seq % t_scan == 0
    return t_seq, t_row, t_ff, t_scan


def _params(*sem):
    return pltpu.CompilerParams(dimension_semantics=sem, vmem_limit_bytes=VMEM_LIMIT_BYTES)


def _const_spec(shape):
    zeros = (0,) * len(shape)
    return pl.BlockSpec(shape, lambda *_: zeros, pipeline_mode=pl.Buffered(1))


def _layer_norm(v, g, b):
    mu = jnp.mean(v, axis=-1, keepdims=True)
    vc = v - mu
    var = jnp.mean(vc * vc, axis=-1, keepdims=True)
    return vc * lax.rsqrt(var + LN_EPS) * g + b


def _gelu(v):
    return 0.5 * v * (1.0 + lax.erf(v * (1.0 / math.sqrt(2.0))))


def _in_proj_kernel(x_ref, w_ref, u_ref, q_ref, k_ref, v_ref, qi_ref, kw_ref, *, n_batch, widths):
    b = pl.program_id(1)
    xb = x_ref[...].astype(BF16)
    ts = xb.shape[0]
    starts = [sum(widths[:n]) for n in range(len(widths))]
    proj = lambda n: lax.dot_general(xb, w_ref[starts[n]:starts[n] + widths[n], :],
                                     (((1,), (1,)), ((), ())), preferred_element_type=F32)
    u = proj(0)
    for s in range(u.shape[1] // LANES):
        u_ref[s, pl.ds(b, ts, stride=n_batch), :] = u[:, s * LANES:(s + 1) * LANES]
    q_ref[...] = proj(1).astype(BF16)
    k_ref[...] = proj(2).astype(BF16)
    v_ref[...] = proj(3).T.astype(BF16)
    qi_ref[...] = proj(4).astype(BF16)
    kw = proj(5)
    kw_ref[:, :kw.shape[1]] = kw
    kw_ref[:, kw.shape[1]:] = jnp.zeros((ts, kw_ref.shape[1] - kw.shape[1]), F32)


def _in_proj(x, w, widths, *, ts):
    n_batch, seq, d = x.shape
    w_u, w_q, w_k, w_v, w_qi, _ = widths
    w_kw = LANES
    n_slabs = w_u // LANES
    row = lambda wd: pl.BlockSpec((None, ts, wd), lambda i, b: (b, i, 0))
    out_shape = (
        jax.ShapeDtypeStruct((n_slabs, seq * n_batch, LANES), F32),
        jax.ShapeDtypeStruct((n_batch, seq, w_q), BF16),
        jax.ShapeDtypeStruct((n_batch, seq, w_k), BF16),
        jax.ShapeDtypeStruct((n_batch, seq // ts, w_v, ts), BF16),
        jax.ShapeDtypeStruct((n_batch, seq, w_qi), BF16),
        jax.ShapeDtypeStruct((n_batch, seq, w_kw), F32),
    )
    return pl.pallas_call(
        functools.partial(_in_proj_kernel, n_batch=n_batch, widths=widths),
        grid=(seq // ts, n_batch),
        in_specs=[row(d), _const_spec(w.shape)],
        out_specs=(
            pl.BlockSpec((n_slabs, ts * n_batch, LANES), lambda i, b: (0, i, 0)),
            row(w_q), row(w_k),
            pl.BlockSpec((None, None, w_v, ts), lambda i, b: (b, i, 0, 0)),
            row(w_qi), row(w_kw),
        ),
        out_shape=out_shape,
        compiler_params=_params("arbitrary", "arbitrary"),
        name="in_proj",
    )(x, w)


def _s5_discretize_kernel(a_re_ref, a_im_ref, log_dt_ref, bt_re_ref, bt_im_ref,
                          abar_re_ref, abar_im_ref, bbar_re_ref, bbar_im_ref):
    a_re = a_re_ref[...]
    a_im = a_im_ref[...]
    dt = jnp.exp(log_dt_ref[...])
    mag = jnp.exp(dt * a_re)
    ang = dt * a_im
    abar_re = mag * jnp.cos(ang)
    abar_im = mag * jnp.sin(ang)
    num_re = abar_re - 1.0
    num_im = abar_im
    den = a_re * a_re + a_im * a_im
    f_re = (num_re * a_re + num_im * a_im) / den
    f_im = (num_im * a_re - num_re * a_im) / den
    abar_re_ref[...] = abar_re
    abar_im_ref[...] = abar_im
    bt_re = bt_re_ref[...]
    bt_im = bt_im_ref[...]
    bbar_re_ref[...] = f_re * bt_re - f_im * bt_im
    bbar_im_ref[...] = f_re * bt_im + f_im * bt_re


def _s5_discretize(a_re, a_im, log_dt, bt_re, bt_im):
    g, p = a_re.shape
    abar_re, abar_im, bbar_re, bbar_im = pl.pallas_call(
        _s5_discretize_kernel,
        out_shape=(jax.ShapeDtypeStruct((g, 1, p), F32), jax.ShapeDtypeStruct((g, 1, p), F32),
                   jax.ShapeDtypeStruct(bt_re.shape, F32), jax.ShapeDtypeStruct(bt_re.shape, F32)),
        name="s5_discretize",
    )(a_re.reshape(g, 1, p), a_im.reshape(g, 1, p), log_dt.reshape(g, 1, 1), bt_re, bt_im)
    return abar_re.reshape(g, p), abar_im.reshape(g, p), bbar_re, bbar_im


def _block_diag_slabs(m):
    n, gl, a, b = m.shape
    eye = jnp.eye(gl, dtype=m.dtype)
    full = m[:, :, :, None, :] * eye[None, :, None, :, None]
    return full.reshape(n, gl * a, gl * b)


def _s5_scan_kernel(u_next_ref, u_prev_ref, bbd_ref, ar_ref, ai_ref, cbd_ref, d_ref, y_ref,
                    buf_a, buf_b, carry_ref, *, ts, n_batch, tiles_per_slab, n_tiles):
    n = pl.program_id(0)
    last = n_tiles - 1
    cur = jnp.minimum(n, last)
    slab_next = jnp.minimum(n + 1, last) // tiles_per_slab
    slab_cur = cur // tiles_per_slab
    slab_prev = jnp.maximum(n - 1, 0) // tiles_per_slab
    half = ar_ref.shape[-1]

    @pl.when(n == 0)
    def _():
        buf_a[...] = jnp.dot(u_prev_ref[...].astype(BF16), bbd_ref[0], preferred_element_type=F32)
        buf_b[...] = jnp.zeros_like(buf_b)
        carry_ref[...] = jnp.zeros_like(carry_ref)

    def stages(cur_buf, other_buf):
        y = jnp.dot(other_buf[...].astype(BF16), cbd_ref[slab_prev], preferred_element_type=F32)
        y_ref[...] = _gelu(y + d_ref[slab_prev] * u_prev_ref[...])
        other_buf[...] = jnp.dot(u_next_ref[...].astype(BF16), bbd_ref[slab_next],
                                 preferred_element_type=F32)
        ar = ar_ref[slab_cur]
        ai = ai_ref[slab_cur]
        first = cur % tiles_per_slab == 0
        re = jnp.where(first, 0.0, carry_ref[:, :half])
        im = jnp.where(first, 0.0, carry_ref[:, half:])
        for t in range(ts):
            rows = slice(t * n_batch, (t + 1) * n_batch)
            bu = cur_buf[rows, :]
            re, im = ar * re - ai * im + bu[:, :half], ar * im + ai * re + bu[:, half:]
            cur_buf[rows, :] = jnp.concatenate([re, im], axis=-1)
        carry_ref[...] = jnp.concatenate([re, im], axis=-1)

    @pl.when(n % 2 == 0)
    def _():
        stages(buf_a, buf_b)

    @pl.when(n % 2 == 1)
    def _():
        stages(buf_b, buf_a)


def _s5_scan(u_tm, bbd, ar, ai, cbd, d, *, ts, n_batch):
    n_slabs, rows, _ = u_tm.shape
    tr = ts * n_batch
    tiles_per_slab = rows // tr
    n_tiles = n_slabs * tiles_per_slab
    n_state2 = bbd.shape[-1]
    tile = lambda shift: pl.BlockSpec(
        (tr, LANES), lambda n: (jnp.clip(n + shift, 0, n_tiles - 1), 0))
    u_flat = u_tm.reshape(n_slabs * rows, LANES)
    y = pl.pallas_call(
        functools.partial(_s5_scan_kernel, ts=ts, n_batch=n_batch,
                          tiles_per_slab=tiles_per_slab, n_tiles=n_tiles),
        grid=(n_tiles + 1,),
        in_specs=[tile(1), tile(-1), _const_spec(bbd.shape), _const_spec(ar.shape),
                  _const_spec(ai.shape), _const_spec(cbd.shape), _const_spec(d.shape)],
        out_specs=tile(-1),
        out_shape=jax.ShapeDtypeStruct(u_flat.shape, F32),
        scratch_shapes=[pltpu.VMEM((tr, n_state2), F32), pltpu.VMEM((tr, n_state2), F32),
                        pltpu.VMEM((n_batch, n_state2), F32)],
        compiler_params=_params("arbitrary"),
        name="s5_scan",
    )(u_flat, u_flat, bbd, ar, ai, cbd, d)
    return y.reshape(u_tm.shape)


def _glu_kernel(y_ref, w_ref, b_ref, o_ref, *, ts, n_batch):
    n_slabs = y_ref.shape[0]
    w = w_ref[...]
    bias = b_ref[...]
    for b in range(n_batch):
        yb = jnp.concatenate(
            [y_ref[s, pl.ds(b, ts, stride=n_batch), :] for s in range(n_slabs)], axis=-1)
        z = jnp.dot(yb.astype(BF16), w, preferred_element_type=F32) + bias
        o_ref[b] = (yb * jax.nn.sigmoid(z)).astype(BF16)


def _glu(y_tm, w_glu, b_glu, *, ts, n_batch):
    n_slabs, rows, _ = y_tm.shape
    seq = rows // n_batch
    width = n_slabs * LANES
    return pl.pallas_call(
        functools.partial(_glu_kernel, ts=ts, n_batch=n_batch),
        grid=(seq // ts,),
        in_specs=[
            pl.BlockSpec((n_slabs, ts * n_batch, LANES), lambda i: (0, i, 0)),
            _const_spec(w_glu.shape),
            _const_spec((1, width)),
        ],
        out_specs=pl.BlockSpec((n_batch, ts, width), lambda i: (0, i, 0)),
        out_shape=jax.ShapeDtypeStruct((n_batch, seq, width), BF16),
        compiler_params=_params("arbitrary"),
        name="glu",
    )(y_tm, w_glu, b_glu.reshape(1, width))


def _attn_kernel(qi_ref, kwq_ref, kwf_ref, q_ref, k_ref, vt_ref, *rest,
                 n_cast, tq, n_keep, idx_w_scale, qk_scale):
    cast_src, o_ref, cast_dst = rest[:n_cast], rest[n_cast], rest[n_cast + 1:2 * n_cast + 1]
    sc_ref, lg_ref, bias_ref, q4_ref, mx_ref, m_ref, ls_ref, ot_ref = rest[2 * n_cast + 1:]
    for src, dst in zip(cast_src, cast_dst):
        dst[...] = src[...].astype(BF16)

    i = pl.program_id(1)
    n_chunks = i + 1
    nt = (((1,), (1,)), ((), ()))
    fold = lambda a: a.reshape(tq // ACC_ROWS, ACC_ROWS, tq)

    key_pos = lax.broadcasted_iota(I32, (tq, tq), 0)
    qry_pos = lax.broadcasted_iota(I32, (tq, tq), 1)
    wi_t = kwq_ref[...].T[IDX_DIM:IDX_DIM + N_IDX_HEADS, :] * idx_w_scale

    def score_chunk(j):
        off = pl.multiple_of(j * tq, tq)
        kic = kwf_ref[pl.ds(off, tq), :][:, :IDX_DIM].astype(BF16)
        s = jnp.zeros((tq, tq), F32)
        for h in range(N_IDX_HEADS):
            r = lax.dot_general(kic, qi_ref[:, h * IDX_DIM:(h + 1) * IDX_DIM], nt,
                                preferred_element_type=F32)
            s = s + wi_t[h:h + 1, :] * jnp.maximum(r, 0.0)
        causal = (j - i) * tq + key_pos <= qry_pos
        sc_ref[j] = jnp.where(causal, s, -jnp.inf)

    _for_each_chunk(n_chunks, score_chunk)

    def decode(key):
        return pltpu.bitcast(jnp.where(key < 0, key ^ jnp.int32(0x7FFFFFFF), key), F32)

    def count_ge(thr, strict=False):
        thr_b = jnp.broadcast_to(thr, (ACC_ROWS, tq))

        def body(j, acc):
            s = fold(sc_ref[j])
            hit = s > thr_b if strict else s >= thr_b
            return acc + jnp.sum(jnp.where(hit, 1.0, 0.0), axis=0)

        acc = lax.fori_loop(0, n_chunks, body, jnp.zeros((ACC_ROWS, tq), F32))
        return jnp.sum(acc, axis=0, keepdims=True)

    keep = jnp.float32(n_keep)
    cnt0 = count_ge(jnp.zeros((1, tq), F32))
    tau = jnp.where(cnt0 >= keep, jnp.int32(0), jnp.int32(INT_MIN))
    cnt = jnp.where(cnt0 >= keep, cnt0, 0.0)

    def bit_body(bi, carry):
        tau, cnt = carry
        cand = tau + jnp.left_shift(jnp.int32(1), jnp.int32(30) - bi)
        c = count_ge(decode(cand))
        return jnp.where(c >= keep, cand, tau), jnp.where(c >= keep, c, cnt)

    tau, cnt = lax.fori_loop(0, 31, bit_body, (tau, cnt))
    thr = jnp.where(tau == jnp.int32(INT_MIN), jnp.finfo(F32).min, decode(tau))

    @pl.when(jnp.max(cnt) > keep)
    def _():
        need = keep - count_ge(thr, strict=True)
        lower = jnp.where(key_pos >= qry_pos, 1.0, 0.0).astype(BF16)

        def tie_chunk(j, run):
            s = sc_ref[j]
            eq = s == thr
            seen = run + jnp.dot(lower, jnp.where(eq, 1.0, 0.0).astype(BF16),
                                 preferred_element_type=F32)
            sc_ref[j] = jnp.where(eq, jnp.where(seen > need, -jnp.inf, s), s)
            return seen[tq - 1:tq, :]

        lax.fori_loop(0, n_chunks, tie_chunk, jnp.zeros((1, tq), F32))

    def mask_chunk(j, carry):
        sc_ref[j] = jnp.where(sc_ref[j] >= thr, 0.0, -1e30)
        return carry

    lax.fori_loop(0, n_chunks, mask_chunk, 0)

    slopes = [2.0 ** (-8.0 * (hd + 1) / N_HEADS) for hd in range(N_HEADS)]
    for hd in range(N_HEADS):
        c, g = divmod(hd, GQA_GROUP)
        q4_ref[c, g * tq:(g + 1) * tq, :] = q_ref[:, hd * HEAD_DIM:(hd + 1) * HEAD_DIM]
        bias_ref[hd] = slopes[hd] * key_pos.astype(F32)
    mx_ref[...] = jnp.full(mx_ref.shape, -jnp.inf, F32)
    ls_ref[...] = jnp.zeros_like(ls_ref)
    ot_ref[...] = jnp.zeros_like(ot_ref)

    def chunk_shift(j, hd):
        return ((j - i) * tq).astype(F32) * slopes[hd]

    def logits_chunk(j):
        off = pl.multiple_of(j * tq, tq)
        mask = sc_ref[j]
        for c in range(N_KV_HEADS):
            kc = k_ref[pl.ds(off, tq), :][:, c * HEAD_DIM:(c + 1) * HEAD_DIM]
            lg4 = lax.dot_general(kc, q4_ref[c], nt, preferred_element_type=F32) * qk_scale
            for g in range(GQA_GROUP):
                hd = c * GQA_GROUP + g
                lg = lg4[:, g * tq:(g + 1) * tq] + bias_ref[hd] + mask
                lg_ref[j, hd] = lg
                mx_ref[hd] = jnp.maximum(mx_ref[hd], jnp.max(fold(lg), axis=0) + chunk_shift(j, hd))

    _for_each_chunk(n_chunks, logits_chunk)
    for hd in range(N_HEADS):
        m_ref[hd] = jnp.max(mx_ref[hd], axis=0, keepdims=True)

    def pv_chunk(j):
        for c in range(N_KV_HEADS):
            vt = vt_ref[j, c * HEAD_DIM:(c + 1) * HEAD_DIM, :]
            for g in range(GQA_GROUP):
                hd = c * GQA_GROUP + g
                p = jnp.exp(lg_ref[j, hd] - (m_ref[hd] - chunk_shift(j, hd)))
                ls_ref[hd] += jnp.sum(fold(p), axis=0)
                ot_ref[hd] += jnp.dot(vt, p.astype(BF16), preferred_element_type=F32)

    _for_each_chunk(n_chunks, pv_chunk)
    for hd in range(N_HEADS):
        denom = jnp.sum(ls_ref[hd], axis=0, keepdims=True)
        o_ref[:, hd * HEAD_DIM:(hd + 1) * HEAD_DIM] = (ot_ref[hd] / denom).T.astype(BF16)


def _for_each_chunk(n_chunks, fn):
    def pair(p, carry):
        fn(2 * p)
        fn(2 * p + 1)
        return carry

    lax.fori_loop(0, n_chunks // 2, pair, 0)

    @pl.when(n_chunks % 2 == 1)
    def _():
        fn(n_chunks - 1)


def _cast_block_rows(n_rows, n_steps):
    bf16_rows = 2 * SUBLANES
    for rows in range(bf16_rows, n_rows + 1, bf16_rows):
        if n_rows % rows == 0 and n_rows // rows <= n_steps:
            return rows
    raise ValueError(f"no row block for {n_rows} rows in {n_steps} steps")


def _attention(qi, kw, q, k, vt, cast_weights):
    n_batch, seq, _ = q.shape
    n_chunks, kv_w, tq = vt.shape[1:]
    n_keep = min(INDEX_TOPK, seq // 4)
    blk = lambda w: pl.BlockSpec((None, tq, w), lambda b, i: (b, i, 0))
    whole = lambda w: pl.BlockSpec((None, seq, w), lambda b, i: (b, 0, 0))

    def cast_spec(a):
        rows = _cast_block_rows(a.shape[0], n_batch * n_chunks)
        last = a.shape[0] // rows - 1
        return pl.BlockSpec((rows, a.shape[1]),
                            lambda b, i: (jnp.minimum(b * n_chunks + i, last), 0))

    cast_specs = [cast_spec(a) for a in cast_weights]
    out = pl.pallas_call(
        functools.partial(_attn_kernel, n_cast=len(cast_weights), tq=tq, n_keep=n_keep,
                          idx_w_scale=(N_IDX_HEADS ** -0.5) * (IDX_DIM ** -0.5),
                          qk_scale=HEAD_DIM ** -0.5),
        grid=(n_batch, n_chunks),
        in_specs=[blk(qi.shape[2]), blk(kw.shape[2]), whole(kw.shape[2]), blk(q.shape[2]),
                  whole(k.shape[2]),
                  pl.BlockSpec((None, n_chunks, kv_w, tq), lambda b, i: (b, 0, 0, 0))] + cast_specs,
        out_specs=[blk(q.shape[2])] + cast_specs,
        out_shape=[jax.ShapeDtypeStruct(q.shape, BF16)]
        + [jax.ShapeDtypeStruct(a.shape, BF16) for a in cast_weights],
        scratch_shapes=[pltpu.VMEM((n_chunks, tq, tq), F32),
                        pltpu.VMEM((n_chunks, N_HEADS, tq, tq), F32),
                        pltpu.VMEM((N_HEADS, tq, tq), F32),
                        pltpu.VMEM((N_KV_HEADS, GQA_GROUP * tq, HEAD_DIM), BF16),
                        pltpu.VMEM((N_HEADS, ACC_ROWS, tq), F32),
                        pltpu.VMEM((N_HEADS, 1, tq), F32),
                        pltpu.VMEM((N_HEADS, ACC_ROWS, tq), F32),
                        pltpu.VMEM((N_HEADS, HEAD_DIM, tq), F32)],
        compiler_params=_params("arbitrary", "arbitrary"),
        name="sparse_attn",
    )(qi, kw, kw, q, k, vt, *cast_weights)
    return out[0], out[1:]


def _out_proj_kernel(ys_ref, ya_ref, x_ref, w_ref, g_ref, b_ref, o_ref):
    ssm_w = ys_ref.shape[1]
    for r in range(0, x_ref.shape[0], LN_ROWS):
        rows = slice(r, r + LN_ROWS)
        mix = jnp.dot(ys_ref[rows, :], w_ref[:ssm_w, :], preferred_element_type=F32)
        mix = mix + jnp.dot(ya_ref[rows, :], w_ref[ssm_w:, :], preferred_element_type=F32)
        o_ref[rows, :] = _layer_norm(DEEPNORM_ALPHA * x_ref[rows, :] + mix, g_ref[...], b_ref[...])


def _out_proj(ys, ya, x, w, g, b, *, tm):
    m, d = x.shape
    blk = lambda wd: pl.BlockSpec((tm, wd), lambda i: (i, 0))
    return pl.pallas_call(
        _out_proj_kernel,
        grid=(m // tm,),
        in_specs=[blk(ys.shape[1]), blk(ya.shape[1]), blk(d), _const_spec(w.shape),
                  _const_spec((1, d)), _const_spec((1, d))],
        out_specs=blk(d),
        out_shape=jax.ShapeDtypeStruct((m, d), F32),
        compiler_params=_params("arbitrary"),
        name="out_proj_ln1",
    )(ys, ya, x, w, g.reshape(1, d), b.reshape(1, d))


def _ffn_kernel(h_ref, halo_ref, wup_ref, wgate_ref, wdown_ref, cw_ref, cb_ref, g_ref, b_ref,
                o_ref, hb_ref, acc_ref, *, last_width):
    i = pl.program_id(1)
    j = pl.program_id(2)
    tm = h_ref.shape[0]
    tf = wup_ref.shape[1]

    @pl.when(j == 0)
    def _():
        hb_ref[:HALO_ROWS, :] = jnp.where(i == 0, 0.0, halo_ref[...]).astype(BF16)
        hb_ref[HALO_ROWS:, :] = h_ref[...].astype(BF16)
        acc_ref[...] = jnp.zeros_like(acc_ref)

    def chunk(width):
        ext = jnp.dot(hb_ref[...], wup_ref[:, :width], preferred_element_type=F32)
        cw = cw_ref[:, :width]
        hc = cb_ref[:, :width]
        for lag in range(CONV_WIDTH):
            start = HALO_ROWS - lag
            hc = hc + cw[CONV_WIDTH - 1 - lag:CONV_WIDTH - lag, :] * ext[start:start + tm, :]
        gate = jnp.dot(hb_ref[HALO_ROWS:, :], wgate_ref[:, :width], preferred_element_type=F32)
        act = (_gelu(hc) * gate).astype(BF16)
        acc_ref[...] += jnp.dot(act, wdown_ref[:width, :], preferred_element_type=F32)

    is_last = j == pl.num_programs(2) - 1
    if last_width == tf:
        chunk(tf)
    else:
        pl.when(jnp.logical_not(is_last))(functools.partial(chunk, tf))
        pl.when(is_last)(functools.partial(chunk, last_width))

    @pl.when(is_last)
    def _():
        o_ref[...] = _layer_norm(DEEPNORM_ALPHA * h_ref[...] + acc_ref[...], g_ref[...], b_ref[...])


def _ffn(h, w_up, w_gate, w_down, conv_w, conv_b, g, b, *, tm, tf):
    n_batch, seq, d = h.shape
    f = w_up.shape[1]
    n_chunks = -(-f // tf)
    halo_blocks = tm // HALO_ROWS
    return pl.pallas_call(
        functools.partial(_ffn_kernel, last_width=f - (n_chunks - 1) * tf),
        grid=(n_batch, seq // tm, n_chunks),
        in_specs=[
            pl.BlockSpec((None, tm, d), lambda bb, i, j: (bb, i, 0)),
            pl.BlockSpec((None, HALO_ROWS, d),
                         lambda bb, i, j: (bb, jnp.maximum(i * halo_blocks - 1, 0), 0)),
            pl.BlockSpec((d, tf), lambda bb, i, j: (0, j)),
            pl.BlockSpec((d, tf), lambda bb, i, j: (0, j)),
            pl.BlockSpec((tf, d), lambda bb, i, j: (j, 0)),
            pl.BlockSpec((CONV_WIDTH, tf), lambda bb, i, j: (0, j)),
            pl.BlockSpec((1, tf), lambda bb, i, j: (0, j)),
            _const_spec((1, d)),
            _const_spec((1, d)),
        ],
        out_specs=pl.BlockSpec((None, tm, d), lambda bb, i, j: (bb, i, 0)),
        out_shape=jax.ShapeDtypeStruct((n_batch, seq, d), F32),
        scratch_shapes=[pltpu.VMEM((HALO_ROWS + tm, d), BF16), pltpu.VMEM((tm, d), F32)],
        compiler_params=_params("arbitrary", "arbitrary", "arbitrary"),
        name="ffn_ln2",
    )(h, h, w_up, w_gate, w_down, conv_w, conv_b.reshape(1, f), g.reshape(1, d), b.reshape(1, d))


def _layer(h, w_in, a_re, a_im, log_dt, b_re, b_im, c_re, c_im, ssm_d, w_glu, b_glu, w_out,
           ln1_g, ln1_b, w_up, w_gate, conv_w, conv_b, w_down, ln2_g, ln2_b):
    n_batch, seq, d = h.shape
    assert n_batch == SUBLANES, "the time-major scan layout puts the batch on the sublanes"
    n_groups, n_state = a_re.shape
    ssm_w = n_groups * SSM_GROUP_CH
    attn_w = N_HEADS * HEAD_DIM
    kv_w = N_KV_HEADS * HEAD_DIM
    qi_w = N_IDX_HEADS * IDX_DIM
    n_slabs = ssm_w // LANES
    assert n_state == SSM_STATE and ssm_w % LANES == 0
    t_seq, t_row, t_ff, t_scan = _tiles(seq, w_up.shape[1])

    widths = (ssm_w, attn_w, kv_w, kv_w, qi_w, IDX_DIM + N_IDX_HEADS)
    assert w_in.shape[1] == sum(widths)
    w_in_b = jnp.swapaxes(w_in, 0, 1).astype(BF16)
    u_tm, q, k, vt, qi, kw = _in_proj(h, w_in_b, widths, ts=t_seq)

    abar_re, abar_im, bbar_re, bbar_im = _s5_discretize(
        a_re, a_im, log_dt, jnp.swapaxes(b_re, 1, 2), jnp.swapaxes(b_im, 1, 2))
    gl = GROUPS_PER_SLAB
    to_slabs = lambda m: m.reshape(n_slabs, gl, *m.shape[1:])
    bbd = jnp.concatenate([_block_diag_slabs(to_slabs(bbar_re)),
                           _block_diag_slabs(to_slabs(bbar_im))], axis=-1).astype(BF16)
    cbd_re = _block_diag_slabs(to_slabs(jnp.swapaxes(c_re, 1, 2)))
    cbd_im = _block_diag_slabs(to_slabs(jnp.swapaxes(c_im, 1, 2)))
    cbd = jnp.concatenate([cbd_re, -cbd_im], axis=1).astype(BF16)
    bcast = lambda m: jnp.broadcast_to(m.reshape(n_slabs, 1, gl * n_state), (n_slabs, n_batch, gl * n_state))
    y_tm = _s5_scan(u_tm, bbd, bcast(abar_re), bcast(abar_im), cbd,
                    ssm_d.reshape(n_slabs, 1, LANES), ts=t_scan, n_batch=n_batch)
    y_ssm = _glu(y_tm, w_glu.astype(BF16), b_glu, ts=t_seq, n_batch=n_batch)

    y_attn, (w_out_b, w_up_b, w_gate_b, w_down_b) = _attention(
        qi, kw, q, k, vt, (w_out, w_up, w_gate, w_down))

    h1 = _out_proj(y_ssm.reshape(n_batch * seq, ssm_w), y_attn.reshape(n_batch * seq, attn_w),
                   h.reshape(n_batch * seq, d), w_out_b, ln1_g, ln1_b, tm=t_row)
    return _ffn(h1.reshape(n_batch, seq, d), w_up_b, w_gate_b, w_down_b, conv_w, conv_b,
                ln2_g, ln2_b, tm=t_row, tf=t_ff)


def kernel(x, w_in, ssm_a_re, ssm_a_im, ssm_log_dt, ssm_b_re, ssm_b_im, ssm_c_re, ssm_c_im, ssm_d,
           w_glu, b_glu, w_out, ln1_g, ln1_b, w_up, w_gate, conv_w, conv_b, w_down, ln2_g, ln2_b):
    h = x
    for l in range(w_in.shape[0]):
        h = _layer(h, w_in[l], ssm_a_re[l], ssm_a_im[l], ssm_log_dt[l], ssm_b_re[l], ssm_b_im[l],
                   ssm_c_re[l], ssm_c_im[l], ssm_d[l], w_glu[l], b_glu[l], w_out[l], ln1_g[l],
                   ln1_b[l], w_up[l], w_gate[l], conv_w[l], conv_b[l], w_down[l], ln2_g[l], ln2_b[l])
    return h
```

```python
import functools
import math

import jax
import jax.numpy as jnp
from jax import lax
from jax.experimental import pallas as pl
from jax.experimental.pallas import tpu as pltpu

F32 = jnp.float32
BF16 = jnp.bfloat16
I32 = jnp.int32

LANES = 128
SUBLANES = 8
VMEM_LIMIT_BYTES = 56 * 1024 * 1024

SSM_GROUP_CH = 16
SSM_STATE = 64
N_HEADS = 8
HEAD_DIM = 128
N_KV_HEADS = 2
GQA_GROUP = N_HEADS // N_KV_HEADS
N_IDX_HEADS = 16
IDX_DIM = 64
INDEX_TOPK = 256
CONV_WIDTH = 3
LN_EPS = 1e-5
DEPTH = 1
DEEPNORM_ALPHA = (2.0 * DEPTH) ** 0.25

GROUPS_PER_SLAB = LANES // SSM_GROUP_CH
SLAB_STATE = GROUPS_PER_SLAB * SSM_STATE
INT_MIN = -(2 ** 31)
ACC_ROWS = 4 * SUBLANES
LN_ROWS = 128
HALO_ROWS = 2 * SUBLANES


def _tiles(seq, d_ff):
    t_seq = min(256, seq)
    t_row = min(512, seq)
    t_out = min(1024, seq)
    t_ff = min(1024, d_ff)
    t_scan = min(256, seq)
    assert seq % t_seq == 0 and seq % t_row == 0 and d_ff % LANES == 0 and seq % t_scan == 0
    assert seq % t_out == 0
    return t_seq, t_row, t_out, t_ff, t_scan


def _params(*sem):
    return pltpu.CompilerParams(dimension_semantics=sem, vmem_limit_bytes=VMEM_LIMIT_BYTES)


def _const_spec(shape):
    zeros = (0,) * len(shape)
    return pl.BlockSpec(shape, lambda *_: zeros, pipeline_mode=pl.Buffered(1))


def _layer_norm(v, g, b):
    mu = jnp.mean(v, axis=-1, keepdims=True)
    vc = v - mu
    var = jnp.mean(vc * vc, axis=-1, keepdims=True)
    return vc * lax.rsqrt(var + LN_EPS) * g + b


def _gelu(v):
    return 0.5 * v * (1.0 + lax.erf(v * (1.0 / math.sqrt(2.0))))


def _in_proj_kernel(x_ref, w_ref, u_ref, q_ref, k_ref, v_ref, qi_ref, kw_ref, *, n_batch, widths):
    b = pl.program_id(1)
    xb = x_ref[...].astype(BF16)
    ts = xb.shape[0]
    starts = [sum(widths[:n]) for n in range(len(widths))]
    proj = lambda n: lax.dot_general(xb, w_ref[starts[n]:starts[n] + widths[n], :],
                                     (((1,), (1,)), ((), ())), preferred_element_type=F32)
    u = proj(0)
    for s in range(u.shape[1] // LANES):
        u_ref[s, pl.ds(b, ts, stride=n_batch), :] = u[:, s * LANES:(s + 1) * LANES]
    q_ref[...] = proj(1).astype(BF16)
    k_ref[...] = proj(2).astype(BF16)
    v_ref[...] = proj(3).T.astype(BF16)
    qi_ref[...] = proj(4).astype(BF16)
    kw = proj(5)
    kw_ref[:, :kw.shape[1]] = kw
    kw_ref[:, kw.shape[1]:] = jnp.zeros((ts, kw_ref.shape[1] - kw.shape[1]), F32)


def _in_proj(x, w, widths, *, ts):
    n_batch, seq, d = x.shape
    w_u, w_q, w_k, w_v, w_qi, _ = widths
    w_kw = LANES
    n_slabs = w_u // LANES
    row = lambda wd: pl.BlockSpec((None, ts, wd), lambda i, b: (b, i, 0))
    out_shape = (
        jax.ShapeDtypeStruct((n_slabs, seq * n_batch, LANES), F32),
        jax.ShapeDtypeStruct((n_batch, seq, w_q), BF16),
        jax.ShapeDtypeStruct((n_batch, seq, w_k), BF16),
        jax.ShapeDtypeStruct((n_batch, seq // ts, w_v, ts), BF16),
        jax.ShapeDtypeStruct((n_batch, seq, w_qi), BF16),
        jax.ShapeDtypeStruct((n_batch, seq, w_kw), F32),
    )
    return pl.pallas_call(
        functools.partial(_in_proj_kernel, n_batch=n_batch, widths=widths),
        grid=(seq // ts, n_batch),
        in_specs=[row(d), _const_spec(w.shape)],
        out_specs=(
            pl.BlockSpec((n_slabs, ts * n_batch, LANES), lambda i, b: (0, i, 0)),
            row(w_q), row(w_k),
            pl.BlockSpec((None, None, w_v, ts), lambda i, b: (b, i, 0, 0)),
            row(w_qi), row(w_kw),
        ),
        out_shape=out_shape,
        compiler_params=_params("arbitrary", "arbitrary"),
        name="in_proj",
    )(x, w)


def _s5_discretize_kernel(a_re_ref, a_im_ref, log_dt_ref, bt_re_ref, bt_im_ref,
                          abar_re_ref, abar_im_ref, bbar_re_ref, bbar_im_ref):
    a_re = a_re_ref[...]
    a_im = a_im_ref[...]
    dt = jnp.exp(log_dt_ref[...])
    mag = jnp.exp(dt * a_re)
    ang = dt * a_im
    abar_re = mag * jnp.cos(ang)
    abar_im = mag * jnp.sin(ang)
    num_re = abar_re - 1.0
    num_im = abar_im
    den = a_re * a_re + a_im * a_im
    f_re = (num_re * a_re + num_im * a_im) / den
    f_im = (num_im * a_re - num_re * a_im) / den
    abar_re_ref[...] = abar_re
    abar_im_ref[...] = abar_im
    bt_re = bt_re_ref[...]
    bt_im = bt_im_ref[...]
    bbar_re_ref[...] = f_re * bt_re - f_im * bt_im
    bbar_im_ref[...] = f_re * bt_im + f_im * bt_re


def _s5_discretize(a_re, a_im, log_dt, bt_re, bt_im):
    g, p = a_re.shape
    abar_re, abar_im, bbar_re, bbar_im = pl.pallas_call(
        _s5_discretize_kernel,
        out_shape=(jax.ShapeDtypeStruct((g, 1, p), F32), jax.ShapeDtypeStruct((g, 1, p), F32),
                   jax.ShapeDtypeStruct(bt_re.shape, F32), jax.ShapeDtypeStruct(bt_re.shape, F32)),
        name="s5_discretize",
    )(a_re.reshape(g, 1, p), a_im.reshape(g, 1, p), log_dt.reshape(g, 1, 1), bt_re, bt_im)
    return abar_re.reshape(g, p), abar_im.reshape(g, p), bbar_re, bbar_im


def _block_diag_slabs(m):
    n, gl, a, b = m.shape
    eye = jnp.eye(gl, dtype=m.dtype)
    full = m[:, :, :, None, :] * eye[None, :, None, :, None]
    return full.reshape(n, gl * a, gl * b)


def _s5_scan_kernel(u_next_ref, u_prev_ref, bbd_ref, ar_ref, ai_ref, cbd_ref, d_ref, y_ref,
                    buf_a, buf_b, carry_ref, *, ts, n_batch, tiles_per_slab, n_tiles):
    n = pl.program_id(0)
    last = n_tiles - 1
    cur = jnp.minimum(n, last)
    slab_next = jnp.minimum(n + 1, last) // tiles_per_slab
    slab_cur = cur // tiles_per_slab
    slab_prev = jnp.maximum(n - 1, 0) // tiles_per_slab
    half = ar_ref.shape[-1]

    @pl.when(n == 0)
    def _():
        buf_a[...] = jnp.dot(u_prev_ref[...].astype(BF16), bbd_ref[0], preferred_element_type=F32)
        buf_b[...] = jnp.zeros_like(buf_b)
        carry_ref[...] = jnp.zeros_like(carry_ref)

    def stages(cur_buf, other_buf):
        y = jnp.dot(other_buf[...].astype(BF16), cbd_ref[slab_prev], preferred_element_type=F32)
        y_ref[...] = _gelu(y + d_ref[slab_prev] * u_prev_ref[...])
        other_buf[...] = jnp.dot(u_next_ref[...].astype(BF16), bbd_ref[slab_next],
                                 preferred_element_type=F32)
        ar = ar_ref[slab_cur]
        ai = ai_ref[slab_cur]
        first = cur % tiles_per_slab == 0
        re = jnp.where(first, 0.0, carry_ref[:, :half])
        im = jnp.where(first, 0.0, carry_ref[:, half:])
        for t in range(ts):
            rows = slice(t * n_batch, (t + 1) * n_batch)
            bu = cur_buf[rows, :]
            re, im = ar * re - ai * im + bu[:, :half], ar * im + ai * re + bu[:, half:]
            cur_buf[rows, :] = jnp.concatenate([re, im], axis=-1)
        carry_ref[...] = jnp.concatenate([re, im], axis=-1)

    @pl.when(n % 2 == 0)
    def _():
        stages(buf_a, buf_b)

    @pl.when(n % 2 == 1)
    def _():
        stages(buf_b, buf_a)


def _s5_scan(u_tm, bbd, ar, ai, cbd, d, *, ts, n_batch):
    n_slabs, rows, _ = u_tm.shape
    tr = ts * n_batch
    tiles_per_slab = rows // tr
    n_tiles = n_slabs * tiles_per_slab
    n_state2 = bbd.shape[-1]
    tile = lambda shift: pl.BlockSpec(
        (tr, LANES), lambda n: (jnp.clip(n + shift, 0, n_tiles - 1), 0))
    u_flat = u_tm.reshape(n_slabs * rows, LANES)
    y = pl.pallas_call(
        functools.partial(_s5_scan_kernel, ts=ts, n_batch=n_batch,
                          tiles_per_slab=tiles_per_slab, n_tiles=n_tiles),
        grid=(n_tiles + 1,),
        in_specs=[tile(1), tile(-1), _const_spec(bbd.shape), _const_spec(ar.shape),
                  _const_spec(ai.shape), _const_spec(cbd.shape), _const_spec(d.shape)],
        out_specs=tile(-1),
        out_shape=jax.ShapeDtypeStruct(u_flat.shape, F32),
        scratch_shapes=[pltpu.VMEM((tr, n_state2), F32), pltpu.VMEM((tr, n_state2), F32),
                        pltpu.VMEM((n_batch, n_state2), F32)],
        compiler_params=_params("arbitrary"),
        name="s5_scan",
    )(u_flat, u_flat, bbd, ar, ai, cbd, d)
    return y.reshape(u_tm.shape)


def _glu_kernel(y_ref, w_ref, b_ref, o_ref, *, ts, n_batch):
    n_slabs = y_ref.shape[0]
    w = w_ref[...]
    bias = b_ref[...]
    for b in range(n_batch):
        yb = jnp.concatenate(
            [y_ref[s, pl.ds(b, ts, stride=n_batch), :] for s in range(n_slabs)], axis=-1)
        z = jnp.dot(yb.astype(BF16), w, preferred_element_type=F32) + bias
        o_ref[b] = (yb * jax.nn.sigmoid(z)).astype(BF16)


def _glu(y_tm, w_glu, b_glu, *, ts, n_batch):
    n_slabs, rows, _ = y_tm.shape
    seq = rows // n_batch
    width = n_slabs * LANES
    return pl.pallas_call(
        functools.partial(_glu_kernel, ts=ts, n_batch=n_batch),
        grid=(seq // ts,),
        in_specs=[
            pl.BlockSpec((n_slabs, ts * n_batch, LANES), lambda i: (0, i, 0)),
            _const_spec(w_glu.shape),
            _const_spec((1, width)),
        ],
        out_specs=pl.BlockSpec((n_batch, ts, width), lambda i: (0, i, 0)),
        out_shape=jax.ShapeDtypeStruct((n_batch, seq, width), BF16),
        compiler_params=_params("arbitrary"),
        name="glu",
    )(y_tm, w_glu, b_glu.reshape(1, width))


def _attn_kernel(qi_ref, kwq_ref, kwf_ref, q_ref, k_ref, vt_ref, *rest,
                 n_cast, tq, n_keep, idx_w_scale, qk_scale):
    cast_src, o_ref, cast_dst = rest[:n_cast], rest[n_cast], rest[n_cast + 1:2 * n_cast + 1]
    sc_ref, lg_ref, bias_ref, q4_ref, mx_ref, m_ref, ls_ref, ot_ref = rest[2 * n_cast + 1:]
    for src, dst in zip(cast_src, cast_dst):
        dst[...] = src[...].astype(BF16)

    i = pl.program_id(1)
    n_chunks = i + 1
    nt = (((1,), (1,)), ((), ()))
    fold = lambda a: a.reshape(tq // ACC_ROWS, ACC_ROWS, tq)

    key_pos = lax.broadcasted_iota(I32, (tq, tq), 0)
    qry_pos = lax.broadcasted_iota(I32, (tq, tq), 1)
    wi_t = kwq_ref[...].T[IDX_DIM:IDX_DIM + N_IDX_HEADS, :] * idx_w_scale

    def score_chunk(j):
        off = pl.multiple_of(j * tq, tq)
        kic = kwf_ref[pl.ds(off, tq), :][:, :IDX_DIM].astype(BF16)
        s = jnp.zeros((tq, tq), F32)
        for h in range(N_IDX_HEADS):
            r = lax.dot_general(kic, qi_ref[:, h * IDX_DIM:(h + 1) * IDX_DIM], nt,
                                preferred_element_type=F32)
            s = s + wi_t[h:h + 1, :] * jnp.maximum(r, 0.0)
        causal = (j - i) * tq + key_pos <= qry_pos
        sc_ref[j] = jnp.where(causal, s, -jnp.inf)

    _for_each_chunk(n_chunks, score_chunk)

    def decode(key):
        return pltpu.bitcast(jnp.where(key < 0, key ^ jnp.int32(0x7FFFFFFF), key), F32)

    def count_ge(thr, strict=False):
        thr_b = jnp.broadcast_to(thr, (ACC_ROWS, tq))

        def body(j, acc):
            s = fold(sc_ref[j])
            hit = s > thr_b if strict else s >= thr_b
            return acc + jnp.sum(jnp.where(hit, 1.0, 0.0), axis=0)

        acc = lax.fori_loop(0, n_chunks, body, jnp.zeros((ACC_ROWS, tq), F32))
        return jnp.sum(acc, axis=0, keepdims=True)

    keep = jnp.float32(n_keep)
    cnt0 = count_ge(jnp.zeros((1, tq), F32))
    tau = jnp.where(cnt0 >= keep, jnp.int32(0), jnp.int32(INT_MIN))
    cnt = jnp.where(cnt0 >= keep, cnt0, 0.0)

    def bit_body(bi, carry):
        tau, cnt = carry
        cand = tau + jnp.left_shift(jnp.int32(1), jnp.int32(30) - bi)
        c = count_ge(decode(cand))
        return jnp.where(c >= keep, cand, tau), jnp.where(c >= keep, c, cnt)

    tau, cnt = lax.fori_loop(0, 31, bit_body, (tau, cnt))
    thr = jnp.where(tau == jnp.int32(INT_MIN), jnp.finfo(F32).min, decode(tau))

    @pl.when(jnp.max(cnt) > keep)
    def _():
        need = keep - count_ge(thr, strict=True)
        lower = jnp.where(key_pos >= qry_pos, 1.0, 0.0).astype(BF16)

        def tie_chunk(j, run):
            s = sc_ref[j]
            eq = s == thr
            seen = run + jnp.dot(lower, jnp.where(eq, 1.0, 0.0).astype(BF16),
                                 preferred_element_type=F32)
            sc_ref[j] = jnp.where(eq, jnp.where(seen > need, -jnp.inf, s), s)
            return seen[tq - 1:tq, :]

        lax.fori_loop(0, n_chunks, tie_chunk, jnp.zeros((1, tq), F32))

    def mask_chunk(j, carry):
        sc_ref[j] = jnp.where(sc_ref[j] >= thr, 0.0, -1e30)
        return carry

    lax.fori_loop(0, n_chunks, mask_chunk, 0)

    slopes = [2.0 ** (-8.0 * (hd + 1) / N_HEADS) for hd in range(N_HEADS)]
    for hd in range(N_HEADS):
        c, g = divmod(hd, GQA_GROUP)
        q4_ref[c, g * tq:(g + 1) * tq, :] = q_ref[:, hd * HEAD_DIM:(hd + 1) * HEAD_DIM]
        bias_ref[hd] = slopes[hd] * key_pos.astype(F32)
    mx_ref[...] = jnp.full(mx_ref.shape, -jnp.inf, F32)
    ls_ref[...] = jnp.zeros_like(ls_ref)
    ot_ref[...] = jnp.zeros_like(ot_ref)

    def chunk_shift(j, hd):
        return ((j - i) * tq).astype(F32) * slopes[hd]

    def logits_chunk(j):
        off = pl.multiple_of(j * tq, tq)
        mask = sc_ref[j]
        for c in range(N_KV_HEADS):
            kc = k_ref[pl.ds(off, tq), :][:, c * HEAD_DIM:(c + 1) * HEAD_DIM]
            lg4 = lax.dot_general(kc, q4_ref[c], nt, preferred_element_type=F32) * qk_scale
            for g in range(GQA_GROUP):
                hd = c * GQA_GROUP + g
                lg = lg4[:, g * tq:(g + 1) * tq] + bias_ref[hd] + mask
                lg_ref[j, hd] = lg
                mx_ref[hd] = jnp.maximum(mx_ref[hd], jnp.max(fold(lg), axis=0) + chunk_shift(j, hd))

    _for_each_chunk(n_chunks, logits_chunk)
    for hd in range(N_HEADS):
        m_ref[hd] = jnp.max(mx_ref[hd], axis=0, keepdims=True)

    def pv_chunk(j):
        for c in range(N_KV_HEADS):
            vt = vt_ref[j, c * HEAD_DIM:(c + 1) * HEAD_DIM, :]
            for g in range(GQA_GROUP):
                hd = c * GQA_GROUP + g
                p = jnp.exp(lg_ref[j, hd] - (m_ref[hd] - chunk_shift(j, hd)))
                ls_ref[hd] += jnp.sum(fold(p), axis=0)
                ot_ref[hd] += jnp.dot(vt, p.astype(BF16), preferred_element_type=F32)

    _for_each_chunk(n_chunks, pv_chunk)
    for hd in range(N_HEADS):
        denom = jnp.sum(ls_ref[hd], axis=0, keepdims=True)
        o_ref[:, hd * HEAD_DIM:(hd + 1) * HEAD_DIM] = (ot_ref[hd] / denom).T.astype(BF16)


def _for_each_chunk(n_chunks, fn):
    def pair(p, carry):
        fn(2 * p)
        fn(2 * p + 1)
        return carry

    lax.fori_loop(0, n_chunks // 2, pair, 0)

    @pl.when(n_chunks % 2 == 1)
    def _():
        fn(n_chunks - 1)


def _cast_block_rows(n_rows, n_steps):
    bf16_rows = 2 * SUBLANES
    for rows in range(bf16_rows, n_rows + 1, bf16_rows):
        if n_rows % rows == 0 and n_rows // rows <= n_steps:
            return rows
    raise ValueError(f"no row block for {n_rows} rows in {n_steps} steps")


def _attention(qi, kw, q, k, vt, cast_weights):
    n_batch, seq, _ = q.shape
    n_chunks, kv_w, tq = vt.shape[1:]
    n_keep = min(INDEX_TOPK, seq // 4)
    blk = lambda w: pl.BlockSpec((None, tq, w), lambda b, i: (b, i, 0))
    whole = lambda w: pl.BlockSpec((None, seq, w), lambda b, i: (b, 0, 0))

    def cast_spec(a):
        rows = _cast_block_rows(a.shape[0], n_batch * n_chunks)
        last = a.shape[0] // rows - 1
        return pl.BlockSpec((rows, a.shape[1]),
                            lambda b, i: (jnp.minimum(b * n_chunks + i, last), 0))

    cast_specs = [cast_spec(a) for a in cast_weights]
    out = pl.pallas_call(
        functools.partial(_attn_kernel, n_cast=len(cast_weights), tq=tq, n_keep=n_keep,
                          idx_w_scale=(N_IDX_HEADS ** -0.5) * (IDX_DIM ** -0.5),
                          qk_scale=HEAD_DIM ** -0.5),
        grid=(n_batch, n_chunks),
        in_specs=[blk(qi.shape[2]), blk(kw.shape[2]), whole(kw.shape[2]), blk(q.shape[2]),
                  whole(k.shape[2]),
                  pl.BlockSpec((None, n_chunks, kv_w, tq), lambda b, i: (b, 0, 0, 0))] + cast_specs,
        out_specs=[blk(q.shape[2])] + cast_specs,
        out_shape=[jax.ShapeDtypeStruct(q.shape, BF16)]
        + [jax.ShapeDtypeStruct(a.shape, BF16) for a in cast_weights],
        scratch_shapes=[pltpu.VMEM((n_chunks, tq, tq), F32),
                        pltpu.VMEM((n_chunks, N_HEADS, tq, tq), F32),
                        pltpu.VMEM((N_HEADS, tq, tq), F32),
                        pltpu.VMEM((N_KV_HEADS, GQA_GROUP * tq, HEAD_DIM), BF16),
                        pltpu.VMEM((N_HEADS, ACC_ROWS, tq), F32),
                        pltpu.VMEM((N_HEADS, 1, tq), F32),
                        pltpu.VMEM((N_HEADS, ACC_ROWS, tq), F32),
                        pltpu.VMEM((N_HEADS, HEAD_DIM, tq), F32)],
        compiler_params=_params("arbitrary", "arbitrary"),
        name="sparse_attn",
    )(qi, kw, kw, q, k, vt, *cast_weights)
    return out[0], out[1:]


def _out_proj_kernel(ys_ref, ya_ref, x_ref, w_ref, g_ref, b_ref, o_ref):
    ssm_w = ys_ref.shape[1]
    for r in range(0, x_ref.shape[0], LN_ROWS):
        rows = slice(r, r + LN_ROWS)
        mix = jnp.dot(ys_ref[rows, :], w_ref[:ssm_w, :], preferred_element_type=F32)
        mix = mix + jnp.dot(ya_ref[rows, :], w_ref[ssm_w:, :], preferred_element_type=F32)
        o_ref[rows, :] = _layer_norm(DEEPNORM_ALPHA * x_ref[rows, :] + mix, g_ref[...], b_ref[...])


def _out_proj(ys, ya, x, w, g, b, *, tm):
    m, d = x.shape
    blk = lambda wd: pl.BlockSpec((tm, wd), lambda i: (i, 0))
    return pl.pallas_call(
        _out_proj_kernel,
        grid=(m // tm,),
        in_specs=[blk(ys.shape[1]), blk(ya.shape[1]), blk(d), _const_spec(w.shape),
                  _const_spec((1, d)), _const_spec((1, d))],
        out_specs=blk(d),
        out_shape=jax.ShapeDtypeStruct((m, d), F32),
        compiler_params=_params("arbitrary"),
        name="out_proj_ln1",
    )(ys, ya, x, w, g.reshape(1, d), b.reshape(1, d))


def _ffn_kernel(h_ref, halo_ref, wup_ref, wgate_ref, wdown_ref, cw_ref, cb_ref, g_ref, b_ref,
                o_ref, hb_ref, acc_ref, *, last_width):
    i = pl.program_id(1)
    j = pl.program_id(2)
    tm = h_ref.shape[0]
    tf = wup_ref.shape[1]

    @pl.when(j == 0)
    def _():
        hb_ref[:HALO_ROWS, :] = jnp.where(i == 0, 0.0, halo_ref[...]).astype(BF16)
        hb_ref[HALO_ROWS:, :] = h_ref[...].astype(BF16)
        acc_ref[...] = jnp.zeros_like(acc_ref)

    def chunk(width):
        ext = jnp.dot(hb_ref[...], wup_ref[:, :width], preferred_element_type=F32)
        cw = cw_ref[:, :width]
        hc = cb_ref[:, :width]
        for lag in range(CONV_WIDTH):
            start = HALO_ROWS - lag
            hc = hc + cw[CONV_WIDTH - 1 - lag:CONV_WIDTH - lag, :] * ext[start:start + tm, :]
        gate = jnp.dot(hb_ref[HALO_ROWS:, :], wgate_ref[:, :width], preferred_element_type=F32)
        act = (_gelu(hc) * gate).astype(BF16)
        acc_ref[...] += jnp.dot(act, wdown_ref[:width, :], preferred_element_type=F32)

    is_last = j == pl.num_programs(2) - 1
    if last_width == tf:
        chunk(tf)
    else:
        pl.when(jnp.logical_not(is_last))(functools.partial(chunk, tf))
        pl.when(is_last)(functools.partial(chunk, last_width))

    @pl.when(is_last)
    def _():
        o_ref[...] = _layer_norm(DEEPNORM_ALPHA * h_ref[...] + acc_ref[...], g_ref[...], b_ref[...])


def _ffn(h, w_up, w_gate, w_down, conv_w, conv_b, g, b, *, tm, tf):
    n_batch, seq, d = h.shape
    f = w_up.shape[1]
    n_chunks = -(-f // tf)
    halo_blocks = tm // HALO_ROWS
    return pl.pallas_call(
        functools.partial(_ffn_kernel, last_width=f - (n_chunks - 1) * tf),
        grid=(n_batch, seq // tm, n_chunks),
        in_specs=[
            pl.BlockSpec((None, tm, d), lambda bb, i, j: (bb, i, 0)),
            pl.BlockSpec((None, HALO_ROWS, d),
                         lambda bb, i, j: (bb, jnp.maximum(i * halo_blocks - 1, 0), 0)),
            pl.BlockSpec((d, tf), lambda bb, i, j: (0, j)),
            pl.BlockSpec((d, tf), lambda bb, i, j: (0, j)),
            pl.BlockSpec((tf, d), lambda bb, i, j: (j, 0)),
            pl.BlockSpec((CONV_WIDTH, tf), lambda bb, i, j: (0, j)),
            pl.BlockSpec((1, tf), lambda bb, i, j: (0, j)),
            _const_spec((1, d)),
            _const_spec((1, d)),
        ],
        out_specs=pl.BlockSpec((None, tm, d), lambda bb, i, j: (bb, i, 0)),
        out_shape=jax.ShapeDtypeStruct((n_batch, seq, d), F32),
        scratch_shapes=[pltpu.VMEM((HALO_ROWS + tm, d), BF16), pltpu.VMEM((tm, d), F32)],
        compiler_params=_params("arbitrary", "arbitrary", "arbitrary"),
        name="ffn_ln2",
    )(h, h, w_up, w_gate, w_down, conv_w, conv_b.reshape(1, f), g.reshape(1, d), b.reshape(1, d))


def _layer(h, w_in, a_re, a_im, log_dt, b_re, b_im, c_re, c_im, ssm_d, w_glu, b_glu, w_out,
           ln1_g, ln1_b, w_up, w_gate, conv_w, conv_b, w_down, ln2_g, ln2_b):
    n_batch, seq, d = h.shape
    assert n_batch == SUBLANES, "the time-major scan layout puts the batch on the sublanes"
    n_groups, n_state = a_re.shape
    ssm_w = n_groups * SSM_GROUP_CH
    attn_w = N_HEADS * HEAD_DIM
    kv_w = N_KV_HEADS * HEAD_DIM
    qi_w = N_IDX_HEADS * IDX_DIM
    n_slabs = ssm_w // LANES
    assert n_state == SSM_STATE and ssm_w % LANES == 0
    t_seq, t_row, t_out, t_ff, t_scan = _tiles(seq, w_up.shape[1])

    widths = (ssm_w, attn_w, kv_w, kv_w, qi_w, IDX_DIM + N_IDX_HEADS)
    assert w_in.shape[1] == sum(widths)
    w_in_b = jnp.swapaxes(w_in, 0, 1).astype(BF16)
    u_tm, q, k, vt, qi, kw = _in_proj(h, w_in_b, widths, ts=t_seq)

    abar_re, abar_im, bbar_re, bbar_im = _s5_discretize(
        a_re, a_im, log_dt, jnp.swapaxes(b_re, 1, 2), jnp.swapaxes(b_im, 1, 2))
    gl = GROUPS_PER_SLAB
    to_slabs = lambda m: m.reshape(n_slabs, gl, *m.shape[1:])
    bbd = jnp.concatenate([_block_diag_slabs(to_slabs(bbar_re)),
                           _block_diag_slabs(to_slabs(bbar_im))], axis=-1).astype(BF16)
    cbd_re = _block_diag_slabs(to_slabs(jnp.swapaxes(c_re, 1, 2)))
    cbd_im = _block_diag_slabs(to_slabs(jnp.swapaxes(c_im, 1, 2)))
    cbd = jnp.concatenate([cbd_re, -cbd_im], axis=1).astype(BF16)
    bcast = lambda m: jnp.broadcast_to(m.reshape(n_slabs, 1, gl * n_state), (n_slabs, n_batch, gl * n_state))
    y_tm = _s5_scan(u_tm, bbd, bcast(abar_re), bcast(abar_im), cbd,
                    ssm_d.reshape(n_slabs, 1, LANES), ts=t_scan, n_batch=n_batch)
    y_ssm = _glu(y_tm, w_glu.astype(BF16), b_glu, ts=t_seq, n_batch=n_batch)

    y_attn, (w_out_b, w_up_b, w_gate_b, w_down_b) = _attention(
        qi, kw, q, k, vt, (w_out, w_up, w_gate, w_down))

    h1 = _out_proj(y_ssm.reshape(n_batch * seq, ssm_w), y_attn.reshape(n_batch * seq, attn_w),
                   h.reshape(n_batch * seq, d), w_out_b, ln1_g, ln1_b, tm=t_out)
    return _ffn(h1.reshape(n_batch, seq, d), w_up_b, w_gate_b, w_down_b, conv_w, conv_b,
                ln2_g, ln2_b, tm=t_row, tf=t_ff)


def kernel(x, w_in, ssm_a_re, ssm_a_im, ssm_log_dt, ssm_b_re, ssm_b_im, ssm_c_re, ssm_c_im, ssm_d,
           w_glu, b_glu, w_out, ln1_g, ln1_b, w_up, w_gate, conv_w, conv_b, w_down, ln2_g, ln2_b):
    h = x
    for l in range(w_in.shape[0]):
        h = _layer(h, w_in[l], ssm_a_re[l], ssm_a_im[l], ssm_log_dt[l], ssm_b_re[l], ssm_b_im[l],
                   ssm_c_re[l], ssm_c_im[l], ssm_d[l], w_glu[l], b_glu[l], w_out[l], ln1_g[l],
                   ln1_b[l], w_up[l], w_gate[l], conv_w[l], conv_b[l], w_down[l], ln2_g[l], ln2_b[l])
    return h
```

```python
import functools
import math

import jax
import jax.numpy as jnp
from jax import lax
from jax.experimental import pallas as pl
from jax.experimental.pallas import tpu as pltpu

F32 = jnp.float32
BF16 = jnp.bfloat16
I32 = jnp.int32

LANES = 128
SUBLANES = 8
VMEM_LIMIT_BYTES = 56 * 1024 * 1024

SSM_GROUP_CH = 16
SSM_STATE = 64
N_HEADS = 8
HEAD_DIM = 128
N_KV_HEADS = 2
GQA_GROUP = N_HEADS // N_KV_HEADS
N_IDX_HEADS = 16
IDX_DIM = 64
INDEX_TOPK = 256
CONV_WIDTH = 3
LN_EPS = 1e-5
DEPTH = 1
DEEPNORM_ALPHA = (2.0 * DEPTH) ** 0.25

GROUPS_PER_SLAB = LANES // SSM_GROUP_CH
SLAB_STATE = GROUPS_PER_SLAB * SSM_STATE
INT_MIN = -(2 ** 31)
ACC_ROWS = 4 * SUBLANES
LN_ROWS = 128
HALO_ROWS = 2 * SUBLANES


def _tiles(seq, d_ff):
    t_seq = min(256, seq)
    t_row = min(512, seq)
    t_ff = min(1024, d_ff)
    t_scan = min(256, seq)
    assert seq % t_seq == 0 and seq % t_row == 0 and d_ff % LANES == 0 and seq % t_scan == 0
    return t_seq, t_row, t_ff, t_scan


def _params(*sem):
    return pltpu.CompilerParams(dimension_semantics=sem, vmem_limit_bytes=VMEM_LIMIT_BYTES)


def _const_spec(shape):
    zeros = (0,) * len(shape)
    return pl.BlockSpec(shape, lambda *_: zeros, pipeline_mode=pl.Buffered(1))


def _layer_norm(v, g, b):
    mu = jnp.mean(v, axis=-1, keepdims=True)
    vc = v - mu
    var = jnp.mean(vc * vc, axis=-1, keepdims=True)
    return vc * lax.rsqrt(var + LN_EPS) * g + b


def _gelu(v):
    return 0.5 * v * (1.0 + lax.erf(v * (1.0 / math.sqrt(2.0))))


def _in_proj_kernel(x_ref, w_ref, u_ref, q_ref, k_ref, v_ref, qi_ref, kw_ref, *, n_batch, widths):
    b = pl.program_id(1)
    xb = x_ref[...].astype(BF16)
    ts = xb.shape[0]
    starts = [sum(widths[:n]) for n in range(len(widths))]
    proj = lambda n: lax.dot_general(xb, w_ref[starts[n]:starts[n] + widths[n], :],
                                     (((1,), (1,)), ((), ())), preferred_element_type=F32)
    u = proj(0)
    for s in range(u.shape[1] // LANES):
        u_ref[s, pl.ds(b, ts, stride=n_batch), :] = u[:, s * LANES:(s + 1) * LANES]
    q_ref[...] = proj(1).astype(BF16)
    k_ref[...] = proj(2).astype(BF16)
    v_ref[...] = proj(3).T.astype(BF16)
    qi_ref[...] = proj(4).astype(BF16)
    kw = proj(5)
    kw_ref[:, :kw.shape[1]] = kw
    kw_ref[:, kw.shape[1]:] = jnp.zeros((ts, kw_ref.shape[1] - kw.shape[1]), F32)


def _in_proj(x, w, widths, *, ts):
    n_batch, seq, d = x.shape
    w_u, w_q, w_k, w_v, w_qi, _ = widths
    w_kw = LANES
    n_slabs = w_u // LANES
    row = lambda wd: pl.BlockSpec((None, ts, wd), lambda i, b: (b, i, 0))
    out_shape = (
        jax.ShapeDtypeStruct((n_slabs, seq * n_batch, LANES), F32),
        jax.ShapeDtypeStruct((n_batch, seq, w_q), BF16),
        jax.ShapeDtypeStruct((n_batch, seq, w_k), BF16),
        jax.ShapeDtypeStruct((n_batch, seq // ts, w_v, ts), BF16),
        jax.ShapeDtypeStruct((n_batch, seq, w_qi), BF16),
        jax.ShapeDtypeStruct((n_batch, seq, w_kw), F32),
    )
    return pl.pallas_call(
        functools.partial(_in_proj_kernel, n_batch=n_batch, widths=widths),
        grid=(seq // ts, n_batch),
        in_specs=[row(d), _const_spec(w.shape)],
        out_specs=(
            pl.BlockSpec((n_slabs, ts * n_batch, LANES), lambda i, b: (0, i, 0)),
            row(w_q), row(w_k),
            pl.BlockSpec((None, None, w_v, ts), lambda i, b: (b, i, 0, 0)),
            row(w_qi), row(w_kw),
        ),
        out_shape=out_shape,
        compiler_params=_params("arbitrary", "arbitrary"),
        name="in_proj",
    )(x, w)


def _s5_discretize_kernel(a_re_ref, a_im_ref, log_dt_ref, bt_re_ref, bt_im_ref,
                          abar_re_ref, abar_im_ref, bbar_re_ref, bbar_im_ref):
    a_re = a_re_ref[...]
    a_im = a_im_ref[...]
    dt = jnp.exp(log_dt_ref[...])
    mag = jnp.exp(dt * a_re)
    ang = dt * a_im
    abar_re = mag * jnp.cos(ang)
    abar_im = mag * jnp.sin(ang)
    num_re = abar_re - 1.0
    num_im = abar_im
    den = a_re * a_re + a_im * a_im
    f_re = (num_re * a_re + num_im * a_im) / den
    f_im = (num_im * a_re - num_re * a_im) / den
    abar_re_ref[...] = abar_re
    abar_im_ref[...] = abar_im
    bt_re = bt_re_ref[...]
    bt_im = bt_im_ref[...]
    bbar_re_ref[...] = f_re * bt_re - f_im * bt_im
    bbar_im_ref[...] = f_re * bt_im + f_im * bt_re


def _s5_discretize(a_re, a_im, log_dt, bt_re, bt_im):
    g, p = a_re.shape
    abar_re, abar_im, bbar_re, bbar_im = pl.pallas_call(
        _s5_discretize_kernel,
        out_shape=(jax.ShapeDtypeStruct((g, 1, p), F32), jax.ShapeDtypeStruct((g, 1, p), F32),
                   jax.ShapeDtypeStruct(bt_re.shape, F32), jax.ShapeDtypeStruct(bt_re.shape, F32)),
        name="s5_discretize",
    )(a_re.reshape(g, 1, p), a_im.reshape(g, 1, p), log_dt.reshape(g, 1, 1), bt_re, bt_im)
    return abar_re.reshape(g, p), abar_im.reshape(g, p), bbar_re, bbar_im


def _block_diag_slabs(m):
    n, gl, a, b = m.shape
    eye = jnp.eye(gl, dtype=m.dtype)
    full = m[:, :, :, None, :] * eye[None, :, None, :, None]
    return full.reshape(n, gl * a, gl * b)


def _s5_scan_kernel(u_next_ref, u_prev_ref, bbd_ref, ar_ref, ai_ref, cbd_ref, d_ref, y_ref,
                    buf_a, buf_b, carry_ref, *, ts, n_batch, tiles_per_slab, n_tiles):
    n = pl.program_id(0)
    last = n_tiles - 1
    cur = jnp.minimum(n, last)
    slab_next = jnp.minimum(n + 1, last) // tiles_per_slab
    slab_cur = cur // tiles_per_slab
    slab_prev = jnp.maximum(n - 1, 0) // tiles_per_slab
    half = ar_ref.shape[-1]

    @pl.when(n == 0)
    def _():
        buf_a[...] = jnp.dot(u_prev_ref[...].astype(BF16), bbd_ref[0], preferred_element_type=F32)
        buf_b[...] = jnp.zeros_like(buf_b)
        carry_ref[...] = jnp.zeros_like(carry_ref)

    def stages(cur_buf, other_buf):
        y = jnp.dot(other_buf[...].astype(BF16), cbd_ref[slab_prev], preferred_element_type=F32)
        y_ref[...] = _gelu(y + d_ref[slab_prev] * u_prev_ref[...])
        other_buf[...] = jnp.dot(u_next_ref[...].astype(BF16), bbd_ref[slab_next],
                                 preferred_element_type=F32)
        ar = ar_ref[slab_cur]
        ai = ai_ref[slab_cur]
        first = cur % tiles_per_slab == 0
        re = jnp.where(first, 0.0, carry_ref[:, :half])
        im = jnp.where(first, 0.0, carry_ref[:, half:])
        for t in range(ts):
            rows = slice(t * n_batch, (t + 1) * n_batch)
            bu = cur_buf[rows, :]
            re, im = ar * re - ai * im + bu[:, :half], ar * im + ai * re + bu[:, half:]
            cur_buf[rows, :] = jnp.concatenate([re, im], axis=-1)
        carry_ref[...] = jnp.concatenate([re, im], axis=-1)

    @pl.when(n % 2 == 0)
    def _():
        stages(buf_a, buf_b)

    @pl.when(n % 2 == 1)
    def _():
        stages(buf_b, buf_a)


def _s5_scan(u_tm, bbd, ar, ai, cbd, d, *, ts, n_batch):
    n_slabs, rows, _ = u_tm.shape
    tr = ts * n_batch
    tiles_per_slab = rows // tr
    n_tiles = n_slabs * tiles_per_slab
    n_state2 = bbd.shape[-1]
    tile = lambda shift: pl.BlockSpec(
        (tr, LANES), lambda n: (jnp.clip(n + shift, 0, n_tiles - 1), 0))
    u_flat = u_tm.reshape(n_slabs * rows, LANES)
    y = pl.pallas_call(
        functools.partial(_s5_scan_kernel, ts=ts, n_batch=n_batch,
                          tiles_per_slab=tiles_per_slab, n_tiles=n_tiles),
        grid=(n_tiles + 1,),
        in_specs=[tile(1), tile(-1), _const_spec(bbd.shape), _const_spec(ar.shape),
                  _const_spec(ai.shape), _const_spec(cbd.shape), _const_spec(d.shape)],
        out_specs=tile(-1),
        out_shape=jax.ShapeDtypeStruct(u_flat.shape, F32),
        scratch_shapes=[pltpu.VMEM((tr, n_state2), F32), pltpu.VMEM((tr, n_state2), F32),
                        pltpu.VMEM((n_batch, n_state2), F32)],
        compiler_params=_params("arbitrary"),
        name="s5_scan",
    )(u_flat, u_flat, bbd, ar, ai, cbd, d)
    return y.reshape(u_tm.shape)


def _glu_kernel(y_ref, w_ref, b_ref, o_ref, *, ts, n_batch):
    n_slabs = y_ref.shape[0]
    w = w_ref[...]
    bias = b_ref[...]
    for b in range(n_batch):
        yb = jnp.concatenate(
            [y_ref[s, pl.ds(b, ts, stride=n_batch), :] for s in range(n_slabs)], axis=-1)
        z = jnp.dot(yb.astype(BF16), w, preferred_element_type=F32) + bias
        o_ref[b] = (yb * jax.nn.sigmoid(z)).astype(BF16)


def _glu(y_tm, w_glu, b_glu, *, ts, n_batch):
    n_slabs, rows, _ = y_tm.shape
    seq = rows // n_batch
    width = n_slabs * LANES
    return pl.pallas_call(
        functools.partial(_glu_kernel, ts=ts, n_batch=n_batch),
        grid=(seq // ts,),
        in_specs=[
            pl.BlockSpec((n_slabs, ts * n_batch, LANES), lambda i: (0, i, 0)),
            _const_spec(w_glu.shape),
            _const_spec((1, width)),
        ],
        out_specs=pl.BlockSpec((n_batch, ts, width), lambda i: (0, i, 0)),
        out_shape=jax.ShapeDtypeStruct((n_batch, seq, width), BF16),
        compiler_params=_params("arbitrary"),
        name="glu",
    )(y_tm, w_glu, b_glu.reshape(1, width))


def _attn_kernel(qi_ref, kwq_ref, kwf_ref, q_ref, k_ref, vt_ref, bias_ref, *rest,
                 n_cast, tq, n_keep, idx_w_scale, qk_scale):
    cast_src, o_ref, cast_dst = rest[:n_cast], rest[n_cast], rest[n_cast + 1:2 * n_cast + 1]
    sc_ref, lg_ref, q4_ref, mx_ref, m_ref, ls_ref, ot_ref = rest[2 * n_cast + 1:]
    for src, dst in zip(cast_src, cast_dst):
        dst[...] = src[...].astype(BF16)

    i = pl.program_id(1)
    n_chunks = i + 1
    nt = (((1,), (1,)), ((), ()))
    fold = lambda a: a.reshape(tq // ACC_ROWS, ACC_ROWS, tq)

    key_pos = lax.broadcasted_iota(I32, (tq, tq), 0)
    qry_pos = lax.broadcasted_iota(I32, (tq, tq), 1)
    wi_t = kwq_ref[...].T[IDX_DIM:IDX_DIM + N_IDX_HEADS, :] * idx_w_scale

    def score_chunk(j):
        off = pl.multiple_of(j * tq, tq)
        kic = kwf_ref[pl.ds(off, tq), :][:, :IDX_DIM].astype(BF16)
        s = jnp.zeros((tq, tq), F32)
        for h in range(N_IDX_HEADS):
            r = lax.dot_general(kic, qi_ref[:, h * IDX_DIM:(h + 1) * IDX_DIM], nt,
                                preferred_element_type=F32)
            s = s + wi_t[h:h + 1, :] * jnp.maximum(r, 0.0)
        causal = (j - i) * tq + key_pos <= qry_pos
        sc_ref[j] = jnp.where(causal, s, -jnp.inf)

    _for_each_chunk(n_chunks, score_chunk)

    def decode(key):
        return pltpu.bitcast(jnp.where(key < 0, key ^ jnp.int32(0x7FFFFFFF), key), F32)

    def count_ge(thr, strict=False):
        thr_b = jnp.broadcast_to(thr, (ACC_ROWS, tq))

        def body(j, acc):
            s = fold(sc_ref[j])
            hit = s > thr_b if strict else s >= thr_b
            return acc + jnp.sum(jnp.where(hit, 1.0, 0.0), axis=0)

        acc = lax.fori_loop(0, n_chunks, body, jnp.zeros((ACC_ROWS, tq), F32))
        return jnp.sum(acc, axis=0, keepdims=True)

    keep = jnp.float32(n_keep)
    cnt0 = count_ge(jnp.zeros((1, tq), F32))
    tau = jnp.where(cnt0 >= keep, jnp.int32(0), jnp.int32(INT_MIN))
    cnt = jnp.where(cnt0 >= keep, cnt0, 0.0)

    def bit_body(bi, carry):
        tau, cnt = carry
        cand = tau + jnp.left_shift(jnp.int32(1), jnp.int32(30) - bi)
        c = count_ge(decode(cand))
        return jnp.where(c >= keep, cand, tau), jnp.where(c >= keep, c, cnt)

    tau, cnt = lax.fori_loop(0, 31, bit_body, (tau, cnt))
    thr = jnp.where(tau == jnp.int32(INT_MIN), jnp.finfo(F32).min, decode(tau))

    @pl.when(jnp.max(cnt) > keep)
    def _():
        need = keep - count_ge(thr, strict=True)
        lower = jnp.where(key_pos >= qry_pos, 1.0, 0.0).astype(BF16)

        def tie_chunk(j, run):
            s = sc_ref[j]
            eq = s == thr
            seen = run + jnp.dot(lower, jnp.where(eq, 1.0, 0.0).astype(BF16),
                                 preferred_element_type=F32)
            sc_ref[j] = jnp.where(eq, jnp.where(seen > need, -jnp.inf, s), s)
            return seen[tq - 1:tq, :]

        lax.fori_loop(0, n_chunks, tie_chunk, jnp.zeros((1, tq), F32))

    def mask_chunk(j, carry):
        sc_ref[j] = jnp.where(sc_ref[j] >= thr, 0.0, -1e30)
        return carry

    lax.fori_loop(0, n_chunks, mask_chunk, 0)

    slopes = [2.0 ** (-8.0 * (hd + 1) / N_HEADS) for hd in range(N_HEADS)]
    for hd in range(N_HEADS):
        c, g = divmod(hd, GQA_GROUP)
        q4_ref[c, g * tq:(g + 1) * tq, :] = q_ref[:, hd * HEAD_DIM:(hd + 1) * HEAD_DIM]
    mx_ref[...] = jnp.full(mx_ref.shape, -jnp.inf, F32)
    ls_ref[...] = jnp.zeros_like(ls_ref)
    ot_ref[...] = jnp.zeros_like(ot_ref)

    def chunk_shift(j, hd):
        return ((j - i) * tq).astype(F32) * slopes[hd]

    def logits_chunk(j):
        off = pl.multiple_of(j * tq, tq)
        mask = sc_ref[j]
        for c in range(N_KV_HEADS):
            kc = k_ref[pl.ds(off, tq), :][:, c * HEAD_DIM:(c + 1) * HEAD_DIM]
            lg4 = lax.dot_general(kc, q4_ref[c], nt, preferred_element_type=F32) * qk_scale
            for g in range(GQA_GROUP):
                hd = c * GQA_GROUP + g
                lg = lg4[:, g * tq:(g + 1) * tq] + bias_ref[hd] + mask
                lg_ref[j, hd] = lg
                mx_ref[hd] = jnp.maximum(mx_ref[hd], jnp.max(fold(lg), axis=0) + chunk_shift(j, hd))

    _for_each_chunk(n_chunks, logits_chunk)
    for hd in range(N_HEADS):
        m_ref[hd] = jnp.max(mx_ref[hd], axis=0, keepdims=True)

    def pv_chunk(j):
        for c in range(N_KV_HEADS):
            vt = vt_ref[j, c * HEAD_DIM:(c + 1) * HEAD_DIM, :]
            for g in range(GQA_GROUP):
                hd = c * GQA_GROUP + g
                p = jnp.exp(lg_ref[j, hd] - (m_ref[hd] - chunk_shift(j, hd)))
                ls_ref[hd] += jnp.sum(fold(p), axis=0)
                ot_ref[hd] += jnp.dot(vt, p.astype(BF16), preferred_element_type=F32)

    _for_each_chunk(n_chunks, pv_chunk)
    for hd in range(N_HEADS):
        denom = jnp.sum(ls_ref[hd], axis=0, keepdims=True)
        o_ref[:, hd * HEAD_DIM:(hd + 1) * HEAD_DIM] = (ot_ref[hd] / denom).T.astype(BF16)


def _for_each_chunk(n_chunks, fn):
    def pair(p, carry):
        fn(2 * p)
        fn(2 * p + 1)
        return carry

    lax.fori_loop(0, n_chunks // 2, pair, 0)

    @pl.when(n_chunks % 2 == 1)
    def _():
        fn(n_chunks - 1)


def _cast_block_rows(n_rows, n_steps):
    bf16_rows = 2 * SUBLANES
    for rows in range(bf16_rows, n_rows + 1, bf16_rows):
        if n_rows % rows == 0 and n_rows // rows <= n_steps:
            return rows
    raise ValueError(f"no row block for {n_rows} rows in {n_steps} steps")


def _attention(qi, kw, q, k, vt, cast_weights):
    n_batch, seq, _ = q.shape
    n_chunks, kv_w, tq = vt.shape[1:]
    n_keep = min(INDEX_TOPK, seq // 4)
    blk = lambda w: pl.BlockSpec((None, tq, w), lambda b, i: (b, i, 0))
    whole = lambda w: pl.BlockSpec((None, seq, w), lambda b, i: (b, 0, 0))

    def cast_spec(a):
        rows = _cast_block_rows(a.shape[0], n_batch * n_chunks)
        last = a.shape[0] // rows - 1
        return pl.BlockSpec((rows, a.shape[1]),
                            lambda b, i: (jnp.minimum(b * n_chunks + i, last), 0))

    cast_specs = [cast_spec(a) for a in cast_weights]
    slopes = jnp.asarray([2.0 ** (-8.0 * (hd + 1) / N_HEADS) for hd in range(N_HEADS)], F32)
    key_pos = lax.broadcasted_iota(F32, (N_HEADS, tq, tq), 1)
    bias = slopes[:, None, None] * key_pos
    out = pl.pallas_call(
        functools.partial(_attn_kernel, n_cast=len(cast_weights), tq=tq, n_keep=n_keep,
                          idx_w_scale=(N_IDX_HEADS ** -0.5) * (IDX_DIM ** -0.5),
                          qk_scale=HEAD_DIM ** -0.5),
        grid=(n_batch, n_chunks),
        in_specs=[blk(qi.shape[2]), blk(kw.shape[2]), whole(kw.shape[2]), blk(q.shape[2]),
                  whole(k.shape[2]),
                  pl.BlockSpec((None, n_chunks, kv_w, tq), lambda b, i: (b, 0, 0, 0)),
                  _const_spec(bias.shape)] + cast_specs,
        out_specs=[blk(q.shape[2])] + cast_specs,
        out_shape=[jax.ShapeDtypeStruct(q.shape, BF16)]
        + [jax.ShapeDtypeStruct(a.shape, BF16) for a in cast_weights],
        scratch_shapes=[pltpu.VMEM((n_chunks, tq, tq), F32),
                        pltpu.VMEM((n_chunks, N_HEADS, tq, tq), F32),
                        pltpu.VMEM((N_KV_HEADS, GQA_GROUP * tq, HEAD_DIM), BF16),
                        pltpu.VMEM((N_HEADS, ACC_ROWS, tq), F32),
                        pltpu.VMEM((N_HEADS, 1, tq), F32),
                        pltpu.VMEM((N_HEADS, ACC_ROWS, tq), F32),
                        pltpu.VMEM((N_HEADS, HEAD_DIM, tq), F32)],
        compiler_params=_params("arbitrary", "arbitrary"),
        name="sparse_attn",
    )(qi, kw, kw, q, k, vt, bias, *cast_weights)
    return out[0], out[1:]


def _out_proj_kernel(ys_ref, ya_ref, x_ref, w_ref, g_ref, b_ref, o_ref):
    ssm_w = ys_ref.shape[1]
    for r in range(0, x_ref.shape[0], LN_ROWS):
        rows = slice(r, r + LN_ROWS)
        mix = jnp.dot(ys_ref[rows, :], w_ref[:ssm_w, :], preferred_element_type=F32)
        mix = mix + jnp.dot(ya_ref[rows, :], w_ref[ssm_w:, :], preferred_element_type=F32)
        o_ref[rows, :] = _layer_norm(DEEPNORM_ALPHA * x_ref[rows, :] + mix, g_ref[...], b_ref[...])


def _out_proj(ys, ya, x, w, g, b, *, tm):
    m, d = x.shape
    blk = lambda wd: pl.BlockSpec((tm, wd), lambda i: (i, 0))
    return pl.pallas_call(
        _out_proj_kernel,
        grid=(m // tm,),
        in_specs=[blk(ys.shape[1]), blk(ya.shape[1]), blk(d), _const_spec(w.shape),
                  _const_spec((1, d)), _const_spec((1, d))],
        out_specs=blk(d),
        out_shape=jax.ShapeDtypeStruct((m, d), F32),
        compiler_params=_params("arbitrary"),
        name="out_proj_ln1",
    )(ys, ya, x, w, g.reshape(1, d), b.reshape(1, d))


def _ffn_kernel(h_ref, halo_ref, wup_ref, wgate_ref, wdown_ref, cw_ref, cb_ref, g_ref, b_ref,
                o_ref, hb_ref, acc_ref, *, last_width):
    i = pl.program_id(1)
    j = pl.program_id(2)
    tm = h_ref.shape[0]
    tf = wup_ref.shape[1]

    @pl.when(j == 0)
    def _():
        hb_ref[:HALO_ROWS, :] = jnp.where(i == 0, 0.0, halo_ref[...]).astype(BF16)
        hb_ref[HALO_ROWS:, :] = h_ref[...].astype(BF16)
        acc_ref[...] = jnp.zeros_like(acc_ref)

    def chunk(width):
        ext = jnp.dot(hb_ref[...], wup_ref[:, :width], preferred_element_type=F32)
        cw = cw_ref[:, :width]
        hc = cb_ref[:, :width]
        for lag in range(CONV_WIDTH):
            start = HALO_ROWS - lag
            hc = hc + cw[CONV_WIDTH - 1 - lag:CONV_WIDTH - lag, :] * ext[start:start + tm, :]
        gate = jnp.dot(hb_ref[HALO_ROWS:, :], wgate_ref[:, :width], preferred_element_type=F32)
        act = (_gelu(hc) * gate).astype(BF16)
        acc_ref[...] += jnp.dot(act, wdown_ref[:width, :], preferred_element_type=F32)

    is_last = j == pl.num_programs(2) - 1
    if last_width == tf:
        chunk(tf)
    else:
        pl.when(jnp.logical_not(is_last))(functools.partial(chunk, tf))
        pl.when(is_last)(functools.partial(chunk, last_width))

    @pl.when(is_last)
    def _():
        o_ref[...] = _layer_norm(DEEPNORM_ALPHA * h_ref[...] + acc_ref[...], g_ref[...], b_ref[...])


def _ffn(h, w_up, w_gate, w_down, conv_w, conv_b, g, b, *, tm, tf):
    n_batch, seq, d = h.shape
    f = w_up.shape[1]
    n_chunks = -(-f // tf)
    halo_blocks = tm // HALO_ROWS
    return pl.pallas_call(
        functools.partial(_ffn_kernel, last_width=f - (n_chunks - 1) * tf),
        grid=(n_batch, seq // tm, n_chunks),
        in_specs=[
            pl.BlockSpec((None, tm, d), lambda bb, i, j: (bb, i, 0)),
            pl.BlockSpec((None, HALO_ROWS, d),
                         lambda bb, i, j: (bb, jnp.maximum(i * halo_blocks - 1, 0), 0)),
            pl.BlockSpec((d, tf), lambda bb, i, j: (0, j)),
            pl.BlockSpec((d, tf), lambda bb, i, j: (0, j)),
            pl.BlockSpec((tf, d), lambda bb, i, j: (j, 0)),
            pl.BlockSpec((CONV_WIDTH, tf), lambda bb, i, j: (0, j)),
            pl.BlockSpec((1, tf), lambda bb, i, j: (0, j)),
            _const_spec((1, d)),
            _const_spec((1, d)),
        ],
        out_specs=pl.BlockSpec((None, tm, d), lambda bb, i, j: (bb, i, 0)),
        out_shape=jax.ShapeDtypeStruct((n_batch, seq, d), F32),
        scratch_shapes=[pltpu.VMEM((HALO_ROWS + tm, d), BF16), pltpu.VMEM((tm, d), F32)],
        compiler_params=_params("arbitrary", "arbitrary", "arbitrary"),
        name="ffn_ln2",
    )(h, h, w_up, w_gate, w_down, conv_w, conv_b.reshape(1, f), g.reshape(1, d), b.reshape(1, d))


def _layer(h, w_in, a_re, a_im, log_dt, b_re, b_im, c_re, c_im, ssm_d, w_glu, b_glu, w_out,
           ln1_g, ln1_b, w_up, w_gate, conv_w, conv_b, w_down, ln2_g, ln2_b):
    n_batch, seq, d = h.shape
    assert n_batch == SUBLANES, "the time-major scan layout puts the batch on the sublanes"
    n_groups, n_state = a_re.shape
    ssm_w = n_groups * SSM_GROUP_CH
    attn_w = N_HEADS * HEAD_DIM
    kv_w = N_KV_HEADS * HEAD_DIM
    qi_w = N_IDX_HEADS * IDX_DIM
    n_slabs = ssm_w // LANES
    assert n_state == SSM_STATE and ssm_w % LANES == 0
    t_seq, t_row, t_ff, t_scan = _tiles(seq, w_up.shape[1])

    widths = (ssm_w, attn_w, kv_w, kv_w, qi_w, IDX_DIM + N_IDX_HEADS)
    assert w_in.shape[1] == sum(widths)
    w_in_b = jnp.swapaxes(w_in, 0, 1).astype(BF16)
    u_tm, q, k, vt, qi, kw = _in_proj(h, w_in_b, widths, ts=t_seq)

    abar_re, abar_im, bbar_re, bbar_im = _s5_discretize(
        a_re, a_im, log_dt, jnp.swapaxes(b_re, 1, 2), jnp.swapaxes(b_im, 1, 2))
    gl = GROUPS_PER_SLAB
    to_slabs = lambda m: m.reshape(n_slabs, gl, *m.shape[1:])
    bbd = jnp.concatenate([_block_diag_slabs(to_slabs(bbar_re)),
                           _block_diag_slabs(to_slabs(bbar_im))], axis=-1).astype(BF16)
    cbd_re = _block_diag_slabs(to_slabs(jnp.swapaxes(c_re, 1, 2)))
    cbd_im = _block_diag_slabs(to_slabs(jnp.swapaxes(c_im, 1, 2)))
    cbd = jnp.concatenate([cbd_re, -cbd_im], axis=1).astype(BF16)
    bcast = lambda m: jnp.broadcast_to(m.reshape(n_slabs, 1, gl * n_state), (n_slabs, n_batch, gl * n_state))
    y_tm = _s5_scan(u_tm, bbd, bcast(abar_re), bcast(abar_im), cbd,
                    ssm_d.reshape(n_slabs, 1, LANES), ts=t_scan, n_batch=n_batch)
    y_ssm = _glu(y_tm, w_glu.astype(BF16), b_glu, ts=t_seq, n_batch=n_batch)

    y_attn, (w_out_b, w_up_b, w_gate_b, w_down_b) = _attention(
        qi, kw, q, k, vt, (w_out, w_up, w_gate, w_down))

    h1 = _out_proj(y_ssm.reshape(n_batch * seq, ssm_w), y_attn.reshape(n_batch * seq, attn_w),
                   h.reshape(n_batch * seq, d), w_out_b, ln1_g, ln1_b, tm=t_row)
    return _ffn(h1.reshape(n_batch, seq, d), w_up_b, w_gate_b, w_down_b, conv_w, conv_b,
                ln2_g, ln2_b, tm=t_row, tf=t_ff)


def kernel(x, w_in, ssm_a_re, ssm_a_im, ssm_log_dt, ssm_b_re, ssm_b_im, ssm_c_re, ssm_c_im, ssm_d,
           w_glu, b_glu, w_out, ln1_g, ln1_b, w_up, w_gate, conv_w, conv_b, w_down, ln2_g, ln2_b):
    h = x
    for l in range(w_in.shape[0]):
        h = _layer(h, w_in[l], ssm_a_re[l], ssm_a_im[l], ssm_log_dt[l], ssm_b_re[l], ssm_b_im[l],
                   ssm_c_re[l], ssm_c_im[l], ssm_d[l], w_glu[l], b_glu[l], w_out[l], ln1_g[l],
                   ln1_b[l], w_up[l], w_gate[l], conv_w[l], conv_b[l], w_down[l], ln2_g[l], ln2_b[l])
    return h
```
